```python
import math
import jax, jax.numpy as jnp
from jax import lax
import numpy as np

D_MODEL = 1024
BATCH = 8
SEQ = 2048
DEPTH = 2

N_HEADS = 16
HEAD_DIM = D_MODEL // N_HEADS
ATTN_W = N_HEADS * HEAD_DIM
DIL_PAIRS = ((128, 1), (512, 4), (2048, 16))
N_DIL = len(DIL_PAIRS)
BLOCK = 128
KV_RANK = 256
N_IDX_HEADS = 8
IDX_DIM = 64
TOPK_MAX = 256
N_MEM = 256
N_CROSS_HEADS = 4
CROSS_HEAD_DIM = 64
CROSS_WIDTH = N_CROSS_HEADS * CROSS_HEAD_DIM
REL_BUCKETS = 32
REL_MAX_DIST = 2048
D_FF_DENSE = 2816
N_EXPERTS = 8
TOP_K_EXPERTS = 2
D_FF_EXPERT = 3584
RMS_EPS = 1e-6
NEG_INF = -1e30
A_IN = N_DIL * ATTN_W + 2 * ATTN_W + CROSS_WIDTH
B_IN = ATTN_W + KV_RANK + N_IDX_HEADS * IDX_DIM + IDX_DIM + N_IDX_HEADS + CROSS_WIDTH
OUT_IN = ATTN_W + CROSS_WIDTH
N_EVEN = (DEPTH + 1) // 2
N_ODD = DEPTH // 2

kernel_name = "hybrid_dilated_dsa_moe_decoder"


def rmsnorm(x, g):
    xf = x.astype(jnp.float32)
    y = xf * lax.rsqrt(jnp.mean(xf * xf, axis=-1, keepdims=True) + RMS_EPS)
    return (y * g.astype(jnp.float32)).astype(x.dtype)


def split_cols(t, sizes):
    offs = np.cumsum(np.array(sizes))[:-1].tolist()
    return jnp.split(t, offs, axis=-1)


def rel_bucket(dist):
    n = jnp.maximum(dist, 0)
    max_exact = REL_BUCKETS // 2
    nf = jnp.maximum(n, 1).astype(jnp.float32)
    large = max_exact + (jnp.log(nf / max_exact) / math.log(REL_MAX_DIST / max_exact)
                         * (REL_BUCKETS - max_exact)).astype(jnp.int32)
    large = jnp.minimum(large, REL_BUCKETS - 1)
    return jnp.where(n < max_exact, n, large)


def dilated_branch(q, k, v, rel_bias, window, dilation):
    B, S, H, hd = q.shape
    L = S // dilation
    Lp = -(-L // BLOCK) * BLOCK
    nb = Lp // BLOCK
    span = window // dilation

    def to_sub(t):
        t = t.reshape(B, L, dilation, H, hd).transpose(0, 2, 1, 3, 4)
        t = jnp.pad(t, ((0, 0), (0, 0), (0, Lp - L), (0, 0), (0, 0)))
        return t.reshape(B, dilation, nb, BLOCK, H, hd)

    def with_prev(t):
        prev = jnp.pad(t, ((0, 0), (0, 0), (1, 0), (0, 0), (0, 0), (0, 0)))[:, :, :-1]
        return jnp.concatenate([prev, t], axis=3)

    qs = to_sub(q)
    kk = with_prev(to_sub(k))
    vv = with_prev(to_sub(v))
    scores = jnp.einsum('bdnqhc,bdnkhc->bdnhqk', qs, kk,
                        preferred_element_type=jnp.float32) * (hd ** -0.5)
    qi = jnp.arange(BLOCK)[:, None]
    ki = jnp.arange(2 * BLOCK)[None, :]
    rel = qi + BLOCK - ki
    bias = jnp.moveaxis(rel_bias[rel_bucket(rel * dilation)], -1, 0).astype(jnp.float32)
    kpos = jnp.arange(nb)[:, None, None] * BLOCK + ki[None] - BLOCK
    allowed = (rel[None] >= 0) & (rel[None] <= span) & (kpos >= 0)
    scores = jnp.where(allowed[:, None], scores + bias, NEG_INF)
    m = jnp.max(scores, axis=-1)
    p = jnp.exp(scores - m[..., None])
    l = jnp.sum(p, axis=-1)
    o = jnp.einsum('bdnhqk,bdnkhc->bdnqhc', p, vv.astype(jnp.float32))

    def from_sub(t):
        t = t.reshape((B, dilation, Lp) + t.shape[4:])[:, :, :L]
        return jnp.swapaxes(t, 1, 2).reshape((B, S) + t.shape[3:])

    return (from_sub(o), from_sub(jnp.swapaxes(m, -1, -2)), from_sub(jnp.swapaxes(l, -1, -2)))


def mixer_dilated(q_groups, k, v, rel_bias):
    B, S, _ = k.shape
    heads = lambda t: t.reshape(B, S, N_HEADS, HEAD_DIM)
    kh, vh = heads(k), heads(v)
    branches = [dilated_branch(heads(qg), kh, vh, rel_bias, w, d)
                for qg, (w, d) in zip(q_groups, DIL_PAIRS)]
    m_max = jnp.max(jnp.stack([b[1] for b in branches]), axis=0)
    num = jnp.zeros_like(branches[0][0])
    den = jnp.zeros_like(m_max)
    for o_g, m_g, l_g in branches:
        a = jnp.exp(m_g - m_max)
        num = num + a[..., None] * o_g
        den = den + a * l_g
    return (num / den[..., None]).astype(k.dtype).reshape(B, S, ATTN_W)


def mixer_dsa(q, c_kv, q_idx, k_idx, w_idx, w_uk, w_uv, rel_bias):
    B, S, _ = q.shape
    topk = min(TOPK_MAX, S // 4)
    nb = S // BLOCK
    qh = q.reshape(B, S, N_HEADS, HEAD_DIM)
    q_lat = jnp.einsum('bshc,rhc->bshr', qh, w_uk)
    q_idx = q_idx.reshape(B, S, N_IDX_HEADS, IDX_DIM)
    k_idx_f = k_idx.astype(jnp.float32)

    def blockify(t):
        return jnp.moveaxis(t.reshape((B, nb, BLOCK) + t.shape[2:]), 1, 0)

    def one_block(args):
        i, ql, qib, wib = args
        s = jnp.einsum('bqhc,bsc->bqhs', qib.astype(jnp.float32), k_idx_f) * (IDX_DIM ** -0.5)
        score = jnp.einsum('bqh,bqhs->bqs', wib.astype(jnp.float32) * (N_IDX_HEADS ** -0.5),
                           jax.nn.relu(s))
        tpos = i * BLOCK + jnp.arange(BLOCK)
        causal = jnp.arange(S)[None, :] <= tpos[:, None]
        score = jnp.where(causal[None], score, NEG_INF)
        _, idx = lax.top_k(score, topk)
        valid = idx <= tpos[None, :, None]
        c_sel = jax.vmap(lambda c, ix: c[ix])(c_kv, idx)
        logits = jnp.einsum('bqhr,bqkr->bhqk', ql, c_sel,
                            preferred_element_type=jnp.float32) * (HEAD_DIM ** -0.5)
        bias = rel_bias[rel_bucket(tpos[None, :, None] - idx)]
        logits = logits + jnp.moveaxis(bias, -1, 1).astype(jnp.float32)
        logits = jnp.where(valid[:, None], logits, NEG_INF)
        p = jax.nn.softmax(logits, axis=-1)
        o_lat = jnp.einsum('bhqk,bqkr->bqhr', p, c_sel.astype(jnp.float32))
        return o_lat.astype(q.dtype)

    o_lat = lax.map(one_block, (jnp.arange(nb), blockify(q_lat), blockify(q_idx), blockify(w_idx)))
    o_lat = jnp.moveaxis(o_lat, 0, 1).reshape(B, S, N_HEADS, KV_RANK)
    o = jnp.einsum('bshr,rhc->bshc', o_lat, w_uv)
    return o.reshape(B, S, ATTN_W)


def cross_attention(qc, mem_n, w_mem_kv):
    B, S, _ = qc.shape
    kv = jnp.einsum('bnd,de->bne', mem_n, w_mem_kv)
    k, v = jnp.split(kv, 2, axis=-1)
    k = k.reshape(B, N_MEM, N_CROSS_HEADS, CROSS_HEAD_DIM)
    v = v.reshape(B, N_MEM, N_CROSS_HEADS, CROSS_HEAD_DIM)
    q = qc.reshape(B, S, N_CROSS_HEADS, CROSS_HEAD_DIM)
    logits = jnp.einsum('bshc,bnhc->bhsn', q, k,
                        preferred_element_type=jnp.float32) * (CROSS_HEAD_DIM ** -0.5)
    p = jax.nn.softmax(logits, axis=-1)
    o = jnp.einsum('bhsn,bnhc->bshc', p, v.astype(jnp.float32))
    return o.astype(qc.dtype).reshape(B, S, CROSS_WIDTH)


def swiglu(h, w_gate, w_up, w_down):
    return (jax.nn.silu(h @ w_gate) * (h @ w_up)) @ w_down


def moe_swiglu(h, w_router, w_gate, w_up, w_down):
    B, S, D = h.shape
    t = h.reshape(B * S, D)
    logits = (t @ w_router).astype(jnp.float32)
    top_val, top_idx = lax.top_k(logits, TOP_K_EXPERTS)
    gate_top = jax.nn.softmax(top_val, axis=-1)
    gates = jnp.sum(jax.nn.one_hot(top_idx, N_EXPERTS, dtype=jnp.float32) * gate_top[..., None], axis=1)
    y = jnp.zeros((B * S, D), jnp.float32)
    for e in range(N_EXPERTS):
        y = y + gates[:, e:e + 1] * swiglu(t, w_gate[e], w_up[e], w_down[e]).astype(jnp.float32)
    return y.astype(h.dtype).reshape(B, S, D)


def setup_inputs(seed: int = 0) -> dict:
    key = jax.random.key(seed)
    ks = iter(jax.random.split(key, 32))
    nrm = lambda shape, fan_in: jax.random.normal(next(ks), shape, jnp.float32) * (fan_in ** -0.5)
    gain = lambda shape: 1.0 + 0.02 * jax.random.normal(next(ks), shape, jnp.float32)
    D = D_MODEL
    return {
        "x": jax.random.normal(next(ks), (BATCH, SEQ, D), jnp.float32),
        "mem": jax.random.normal(next(ks), (BATCH, N_MEM, D), jnp.float32),
        "rel_bias": 0.2 * jax.random.normal(next(ks), (REL_BUCKETS, N_HEADS), jnp.float32),
        "mem_norm": gain((D,)),
        "final_norm": gain((D,)),
        "mixer_norm": gain((DEPTH, D)),
        "ffn_norm": gain((DEPTH, D)),
        "w_mem_kv": nrm((DEPTH, D, 2 * CROSS_WIDTH), D),
        "w_out": nrm((DEPTH, OUT_IN, D), OUT_IN),
        "even_w_in": nrm((N_EVEN, D, A_IN), D),
        "even_w_gate": nrm((N_EVEN, D, D_FF_DENSE), D),
        "even_w_up": nrm((N_EVEN, D, D_FF_DENSE), D),
        "even_w_down": nrm((N_EVEN, D_FF_DENSE, D), D_FF_DENSE),
        "odd_w_in": nrm((N_ODD, D, B_IN), D),
        "odd_kv_norm": gain((N_ODD, KV_RANK)),
        "odd_w_uk": nrm((N_ODD, KV_RANK, N_HEADS, HEAD_DIM), KV_RANK),
        "odd_w_uv": nrm((N_ODD, KV_RANK, N_HEADS, HEAD_DIM), KV_RANK),
        "odd_w_router": nrm((N_ODD, D, N_EXPERTS), D),
        "odd_w_gate": nrm((N_ODD, N_EXPERTS, D, D_FF_EXPERT), D),
        "odd_w_up": nrm((N_ODD, N_EXPERTS, D, D_FF_EXPERT), D),
        "odd_w_down": nrm((N_ODD, N_EXPERTS, D_FF_EXPERT, D), D_FF_EXPERT),
    }


def reference(x, mem, rel_bias, mem_norm, final_norm, mixer_norm, ffn_norm, w_mem_kv, w_out,
              even_w_in, even_w_gate, even_w_up, even_w_down,
              odd_w_in, odd_kv_norm, odd_w_uk, odd_w_uv,
              odd_w_router, odd_w_gate, odd_w_up, odd_w_down):
    mem_n = rmsnorm(mem, mem_norm)
    for i in range(DEPTH):
        j = i // 2
        h = rmsnorm(x, mixer_norm[i])
        if i % 2 == 0:
            parts = split_cols(h @ even_w_in[j], [ATTN_W] * N_DIL + [ATTN_W, ATTN_W, CROSS_WIDTH])
            mix = mixer_dilated(parts[:N_DIL], parts[N_DIL], parts[N_DIL + 1], rel_bias)
            qc = parts[N_DIL + 2]
        else:
            q, c_kv, q_idx, k_idx, w_idx, qc = split_cols(
                h @ odd_w_in[j],
                [ATTN_W, KV_RANK, N_IDX_HEADS * IDX_DIM, IDX_DIM, N_IDX_HEADS, CROSS_WIDTH])
            c_kv = rmsnorm(c_kv, odd_kv_norm[j])
            mix = mixer_dsa(q, c_kv, q_idx, k_idx, w_idx, odd_w_uk[j], odd_w_uv[j], rel_bias)
        cross = cross_attention(qc, mem_n, w_mem_kv[i])
        x = x + jnp.concatenate([mix, cross], axis=-1) @ w_out[i]
        h = rmsnorm(x, ffn_norm[i])
        if i % 2 == 0:
            x = x + swiglu(h, even_w_gate[j], even_w_up[j], even_w_down[j])
        else:
            x = x + moe_swiglu(h, odd_w_router[j], odd_w_gate[j], odd_w_up[j], odd_w_down[j])
    return rmsnorm(x, final_norm)
```

```python
import functools
import math

import jax
import jax.numpy as jnp
from jax import lax
from jax.experimental import pallas as pl
from jax.experimental.pallas import tpu as pltpu

D_MODEL = 1024
N_HEADS = 16
HEAD_DIM = 64
ATTN_W = N_HEADS * HEAD_DIM
DIL_PAIRS = ((128, 1), (512, 4), (2048, 16))
BLOCK = 128
KV_RANK = 256
N_IDX_HEADS = 8
IDX_DIM = 64
TOPK = 256
N_MEM = 256
N_CROSS_HEADS = 4
CROSS_WIDTH = 256
REL_BUCKETS = 32
REL_MAX_DIST = 2048
N_EXPERTS = 8
RMS_EPS = 1e-6
NEG_INF = -1e30
M_INIT = -5e29

F32 = jnp.float32
BF16 = jnp.bfloat16
V7X_VMEM_LIMIT = 56 * 1024 * 1024
DSA_QB = 256
INT_MIN = -(2 ** 31)

_NT = (((1,), (1,)), ((), ()))


def _params(*sem):
    return pltpu.CompilerParams(dimension_semantics=sem, vmem_limit_bytes=V7X_VMEM_LIMIT)


def _rms(x, g):
    return x * lax.rsqrt(jnp.mean(x * x, axis=-1, keepdims=True) + RMS_EPS) * g


def _norm_mm_kernel(x_ref, g_ref, w_ref, o_ref, xn_ref):
    @pl.when(pl.program_id(1) == 0)
    def _():
        xn_ref[...] = _rms(x_ref[...].astype(F32), g_ref[...]).astype(BF16)

    o_ref[...] = jnp.dot(xn_ref[...], w_ref[...], preferred_element_type=F32).astype(o_ref.dtype)


def norm_matmul(x, g, w, out_dtype, tm, tn):
    t, k = x.shape
    n = w.shape[1]
    return pl.pallas_call(
        _norm_mm_kernel,
        grid=(t // tm, n // tn),
        in_specs=[pl.BlockSpec((tm, k), lambda i, j: (i, 0)),
                  pl.BlockSpec((1, k), lambda i, j: (0, 0)),
                  pl.BlockSpec((k, tn), lambda i, j: (0, j))],
        out_specs=pl.BlockSpec((tm, tn), lambda i, j: (i, j)),
        out_shape=jax.ShapeDtypeStruct((t, n), out_dtype),
        scratch_shapes=[pltpu.VMEM((tm, k), BF16)],
        compiler_params=_params("parallel", "arbitrary"),
        name="norm_matmul",
    )(x, g.reshape(1, k), w)


def _dil_kernel(q_ref, kp_ref, kc_ref, vp_ref, vc_ref, tab_ref, o_ref, lse_ref):
    first = (pl.program_id(2) == 0).astype(jnp.int32)
    for h in range(N_HEADS):
        sl = slice(h * HEAD_DIM, (h + 1) * HEAD_DIM)
        q = q_ref[0, :, sl]
        k = jnp.concatenate([kp_ref[0, :, sl], kc_ref[0, :, sl]], axis=0)
        v = jnp.concatenate([vp_ref[0, :, sl], vc_ref[0, :, sl]], axis=0)
        s = lax.dot_general(q, k, _NT, preferred_element_type=F32) * (HEAD_DIM ** -0.5)
        s = s + tab_ref[first, h]
        m = jnp.max(s, axis=-1, keepdims=True)
        p = jnp.exp(s - m)
        l = jnp.sum(p, axis=-1, keepdims=True)
        o = jnp.dot(p.astype(BF16), v, preferred_element_type=F32)
        o_ref[0, :, sl] = o / l
        lse_ref[0, :, sl] = jnp.broadcast_to(m + jnp.log(l), (BLOCK, HEAD_DIM))


def dilated_branch(q123, kv, table, group, dilation):
    b, s, _ = q123.shape
    ln = s // dilation
    nb = ln // BLOCK
    qv = q123.reshape(b, ln, dilation * 3 * ATTN_W)
    kvv = kv.reshape(b, ln, dilation * 2 * ATTN_W)
    blk = (1, BLOCK, ATTN_W)
    prev = lambda n: jnp.maximum(n - 1, 0)
    out = jax.ShapeDtypeStruct((b, ln, dilation * ATTN_W), F32)
    o, lse = pl.pallas_call(
        _dil_kernel,
        grid=(b, dilation, nb),
        in_specs=[pl.BlockSpec(blk, lambda bi, r, n: (bi, n, 3 * r + group)),
                  pl.BlockSpec(blk, lambda bi, r, n: (bi, prev(n), 2 * r)),
                  pl.BlockSpec(blk, lambda bi, r, n: (bi, n, 2 * r)),
                  pl.BlockSpec(blk, lambda bi, r, n: (bi, prev(n), 2 * r + 1)),
                  pl.BlockSpec(blk, lambda bi, r, n: (bi, n, 2 * r + 1)),
                  pl.BlockSpec(table.shape, lambda bi, r, n: (0, 0, 0, 0))],
        out_specs=[pl.BlockSpec(blk, lambda bi, r, n: (bi, n, r)),
                   pl.BlockSpec(blk, lambda bi, r, n: (bi, n, r))],
        out_shape=[out, out],
        compiler_params=_params("parallel", "parallel", "arbitrary"),
        name=f"dilated_attn_d{dilation}",
    )(qv, kvv, kvv, kvv, kvv, table)
    return o.reshape(b, s, ATTN_W), lse.reshape(b, s, ATTN_W)


def _combine_kernel(o1, o2, o3, l1, l2, l3, out_ref):
    a1, a2, a3 = l1[...], l2[...], l3[...]
    m = jnp.maximum(jnp.maximum(a1, a2), a3)
    w1, w2, w3 = jnp.exp(a1 - m), jnp.exp(a2 - m), jnp.exp(a3 - m)
    num = w1 * o1[...] + w2 * o2[...] + w3 * o3[...]
    out_ref[...] = (num / (w1 + w2 + w3)).astype(out_ref.dtype)


def combine_groups(outs, lses, tm):
    t, w = outs[0].shape
    spec = pl.BlockSpec((tm, w), lambda i: (i, 0))
    return pl.pallas_call(
        _combine_kernel,
        grid=(t // tm,),
        in_specs=[spec] * 6,
        out_specs=spec,
        out_shape=jax.ShapeDtypeStruct((t, w), BF16),
        compiler_params=_params("parallel"),
        name="combine_groups",
    )(*outs, *lses)


def _cross_kernel(q_ref, kv_ref, o_ref):
    for h in range(N_CROSS_HEADS):
        sl = slice(h * HEAD_DIM, (h + 1) * HEAD_DIM)
        vsl = slice(CROSS_WIDTH + h * HEAD_DIM, CROSS_WIDTH + (h + 1) * HEAD_DIM)
        s = lax.dot_general(q_ref[0, :, sl], kv_ref[0, :, sl], _NT,
                            preferred_element_type=F32) * (HEAD_DIM ** -0.5)
        m = jnp.max(s, axis=-1, keepdims=True)
        p = jnp.exp(s - m)
        l = jnp.sum(p, axis=-1, keepdims=True)
        o = jnp.dot(p.astype(BF16), kv_ref[0, :, vsl], preferred_element_type=F32)
        o_ref[0, :, sl] = (o / l).astype(o_ref.dtype)


def cross_attention(qsrc, qcol, kv, tm):
    b, s, _ = qsrc.shape
    return pl.pallas_call(
        _cross_kernel,
        grid=(b, s // tm),
        in_specs=[pl.BlockSpec((1, tm, CROSS_WIDTH), lambda bi, i: (bi, i, qcol)),
                  pl.BlockSpec((1, N_MEM, 2 * CROSS_WIDTH), lambda bi, i: (bi, 0, 0))],
        out_specs=pl.BlockSpec((1, tm, CROSS_WIDTH), lambda bi, i: (bi, i, 0)),
        out_shape=jax.ShapeDtypeStruct((b, s, CROSS_WIDTH), BF16),
        compiler_params=_params("parallel", "parallel"),
        name="cross_attn",
    )(qsrc, kv)


def _outproj_kernel(mix_ref, cr_ref, wa_ref, wb_ref, x_ref, o_ref):
    acc = jnp.dot(mix_ref[...], wa_ref[...], preferred_element_type=F32)
    acc = acc + jnp.dot(cr_ref[...], wb_ref[...], preferred_element_type=F32)
    o_ref[...] = x_ref[...] + acc


def out_projection(mix, cross, wa, wb, x, tm):
    t, d = x.shape
    return pl.pallas_call(
        _outproj_kernel,
        grid=(t // tm,),
        in_specs=[pl.BlockSpec((tm, ATTN_W), lambda i: (i, 0)),
                  pl.BlockSpec((tm, CROSS_WIDTH), lambda i: (i, 0)),
                  pl.BlockSpec(wa.shape, lambda i: (0, 0)),
                  pl.BlockSpec(wb.shape, lambda i: (0, 0)),
                  pl.BlockSpec((tm, d), lambda i: (i, 0))],
        out_specs=pl.BlockSpec((tm, d), lambda i: (i, 0)),
        out_shape=jax.ShapeDtypeStruct((t, d), F32),
        compiler_params=_params("parallel"),
        name="out_projection",
    )(mix, cross, wa, wb, x)


def _swiglu_kernel(x_ref, g_ref, wg_ref, wu_ref, wd_ref, o_ref, hn_ref):
    @pl.when(pl.program_id(1) == 0)
    def _():
        x = x_ref[...]
        hn_ref[...] = _rms(x, g_ref[...]).astype(BF16)
        o_ref[...] = x

    hn = hn_ref[...]
    a = jnp.dot(hn, wg_ref[...], preferred_element_type=F32)
    u = jnp.dot(hn, wu_ref[...], preferred_element_type=F32)
    act = (a * jax.nn.sigmoid(a) * u).astype(BF16)
    o_ref[...] += jnp.dot(act, wd_ref[...], preferred_element_type=F32)


def swiglu_ffn(x, g, wg, wu, wd, tm, tf):
    t, d = x.shape
    f = wg.shape[1]
    return pl.pallas_call(
        _swiglu_kernel,
        grid=(t // tm, f // tf),
        in_specs=[pl.BlockSpec((tm, d), lambda i, j: (i, 0)),
                  pl.BlockSpec((1, d), lambda i, j: (0, 0)),
                  pl.BlockSpec((d, tf), lambda i, j: (0, j)),
                  pl.BlockSpec((d, tf), lambda i, j: (0, j)),
                  pl.BlockSpec((tf, d), lambda i, j: (j, 0))],
        out_specs=pl.BlockSpec((tm, d), lambda i, j: (i, 0)),
        out_shape=jax.ShapeDtypeStruct((t, d), F32),
        scratch_shapes=[pltpu.VMEM((tm, d), BF16)],
        compiler_params=_params("parallel", "arbitrary"),
        name="swiglu_ffn",
    )(x, g.reshape(1, d), wg, wu, wd)


def _router_kernel(x_ref, g_ref, wr_ref, gates_ref):
    hn = _rms(x_ref[...], g_ref[...])
    logits = jnp.dot(hn, wr_ref[...], preferred_element_type=F32,
                     precision=lax.Precision.HIGHEST)
    lane = lax.broadcasted_iota(jnp.int32, logits.shape, 1)
    lg = jnp.where(lane < N_EXPERTS, logits, -jnp.inf)
    m1 = jnp.max(lg, axis=-1, keepdims=True)
    i1 = jnp.min(jnp.where(lg == m1, lane, 128), axis=-1, keepdims=True)
    lg2 = jnp.where(lane == i1, -jnp.inf, lg)
    m2 = jnp.max(lg2, axis=-1, keepdims=True)
    i2 = jnp.min(jnp.where(lg2 == m2, lane, 128), axis=-1, keepdims=True)
    e = jnp.exp(m2 - m1)
    g1 = 1.0 / (1.0 + e)
    g2 = e / (1.0 + e)
    gates_ref[...] = jnp.where(lane == i1, g1, 0.0) + jnp.where(lane == i2, g2, 0.0)


def router_gates(x, g, wr_pad, tm):
    t, d = x.shape
    return pl.pallas_call(
        _router_kernel,
        grid=(t // tm,),
        in_specs=[pl.BlockSpec((tm, d), lambda i: (i, 0)),
                  pl.BlockSpec((1, d), lambda i: (0, 0)),
                  pl.BlockSpec((d, 128), lambda i: (0, 0))],
        out_specs=pl.BlockSpec((tm, 128), lambda i: (i, 0)),
        out_shape=jax.ShapeDtypeStruct((t, 128), F32),
        compiler_params=_params("parallel"),
        name="router",
    )(x, g.reshape(1, d), wr_pad)


def _moe_kernel(x_ref, g_ref, gates_ref, wg_ref, wu_ref, wd_ref, o_ref, hn_ref):
    e = pl.program_id(1)

    @pl.when((e == 0) & (pl.program_id(2) == 0))
    def _():
        x = x_ref[...]
        hn_ref[...] = _rms(x, g_ref[...]).astype(BF16)
        o_ref[...] = x

    hn = hn_ref[...]
    a = jnp.dot(hn, wg_ref[...], preferred_element_type=F32)
    u = jnp.dot(hn, wu_ref[...], preferred_element_type=F32)
    act = (a * jax.nn.sigmoid(a) * u).astype(BF16)
    gates = gates_ref[...]
    lane = lax.broadcasted_iota(jnp.int32, gates.shape, 1)
    ge = jnp.sum(jnp.where(lane == e, gates, 0.0), axis=-1, keepdims=True)
    o_ref[...] += ge * jnp.dot(act, wd_ref[...], preferred_element_type=F32)


def moe_ffn(x, g, gates, wg, wu, wd, tm, tf):
    t, d = x.shape
    ne, _, f = wg.shape
    return pl.pallas_call(
        _moe_kernel,
        grid=(t // tm, ne, f // tf),
        in_specs=[pl.BlockSpec((tm, d), lambda i, e, j: (i, 0)),
                  pl.BlockSpec((1, d), lambda i, e, j: (0, 0)),
                  pl.BlockSpec((tm, 128), lambda i, e, j: (i, 0)),
                  pl.BlockSpec((None, d, tf), lambda i, e, j: (e, 0, j)),
                  pl.BlockSpec((None, d, tf), lambda i, e, j: (e, 0, j)),
                  pl.BlockSpec((None, tf, d), lambda i, e, j: (e, j, 0))],
        out_specs=pl.BlockSpec((tm, d), lambda i, e, j: (i, 0)),
        out_shape=jax.ShapeDtypeStruct((t, d), F32),
        scratch_shapes=[pltpu.VMEM((tm, d), BF16)],
        compiler_params=_params("parallel", "arbitrary", "arbitrary"),
        name="moe_ffn",
    )(x, g.reshape(1, d), gates, wg, wu, wd)


def _kvnorm_kernel(c_ref, g_ref, o_ref, ot_ref):
    y = _rms(c_ref[0].astype(F32), g_ref[...])
    o_ref[0, 0] = y.astype(BF16)
    ot_ref[0, 0] = y.T.astype(BF16)


def kv_latent_norm(p1, ccol, g):
    b, s, _ = p1.shape
    nc = s // DSA_QB
    out = jax.ShapeDtypeStruct((b, nc, DSA_QB, KV_RANK), BF16)
    blk = pl.BlockSpec((1, 1, DSA_QB, KV_RANK), lambda bi, j: (bi, j, 0, 0))
    return pl.pallas_call(
        _kvnorm_kernel,
        grid=(b, nc),
        in_specs=[pl.BlockSpec((1, DSA_QB, KV_RANK), lambda bi, j: (bi, j, ccol)),
                  pl.BlockSpec((1, KV_RANK), lambda bi, j: (0, 0))],
        out_specs=[blk, blk],
        out_shape=[out, out],
        compiler_params=_params("parallel", "parallel"),
        name="kv_latent_norm",
    )(p1, g.reshape(1, KV_RANK))


def _select_kernel(qi_ref, kall_ref, kq_ref, mask_ref, keys_ref, jcut_ref, *, nchunks_total):
    qb = DSA_QB
    i = pl.program_id(1)
    nch = i + 1
    shape = (qb, qb)
    row = lax.broadcasted_iota(jnp.int32, shape, 0)
    qpos = i * qb + lax.broadcasted_iota(jnp.int32, shape, 1)

    wt = kq_ref[0].astype(F32).T
    wscale = (N_IDX_HEADS ** -0.5) * (IDX_DIM ** -0.5)

    def score_chunk(j, carry):
        start = pl.multiple_of(j * qb, qb)
        kc = kall_ref[0, pl.ds(start, qb), :][:, :IDX_DIM]
        acc = jnp.zeros(shape, F32)
        for h in range(N_IDX_HEADS):
            qh = qi_ref[0, :, h * IDX_DIM:(h + 1) * IDX_DIM]
            s = lax.dot_general(kc, qh, _NT, preferred_element_type=F32)
            acc = acc + jnp.maximum(s, 0.0) * (wt[IDX_DIM + h:IDX_DIM + h + 1, :] * wscale)
        acc = jnp.where(j * qb + row <= qpos, acc, NEG_INF)
        acc = jnp.where(acc == 0.0, 0.0, acc)
        bits = pltpu.bitcast(acc, jnp.int32)
        keys_ref[j] = bits ^ ((bits >> 31) & 0x7FFFFFFF)
        return carry

    lax.fori_loop(0, nch, score_chunk, 0)

    def count(pred):
        def body(j, c):
            hit = jnp.where(pred(keys_ref[j], j), 1, 0)
            return c + jnp.sum(hit.reshape(qb // 8, 8, qb), axis=0)
        c = lax.fori_loop(0, nch, body, jnp.zeros((8, qb), jnp.int32))
        return jnp.sum(c, axis=0, keepdims=True)

    def count_ge(cand):
        return count(lambda key, j: key >= cand)

    thr = jnp.where(count_ge(jnp.zeros((1, qb), jnp.int32)) >= TOPK, 0, INT_MIN)

    def bit_body(t, thr):
        cand = thr + jnp.left_shift(jnp.int32(1), 30 - t)
        return jnp.where(count_ge(cand) >= TOPK, cand, thr)

    thr = lax.fori_loop(0, 31, bit_body, thr)

    n_gt = count(lambda key, j: key > thr)
    n_ge = count_ge(thr)
    need = TOPK - n_gt

    jcut_ref[...] = jnp.full((1, qb), 1 << 30, jnp.int32)

    @pl.when(jnp.max(n_ge) > TOPK)
    def _():
        def idx_body(t, x):
            cand = x + jnp.left_shift(jnp.int32(1), 10 - t)
            below = count(lambda key, j: (key == thr) & (j * qb + row <= cand - 1))
            return jnp.where(below < need, cand, x)
        jcut_ref[...] = lax.fori_loop(0, 11, idx_body, jnp.zeros((1, qb), jnp.int32))

    jcut = jcut_ref[...]

    def write_chunk(j, carry):
        key = keys_ref[j]
        spos = j * qb + row
        sel = (key > thr) | ((key == thr) & (spos <= jcut))
        sel = sel & (spos <= qpos)
        start = pl.multiple_of(j * qb, qb)
        mask_ref[0, pl.ds(start, qb), :] = jnp.where(sel, 0.0, NEG_INF)
        return carry

    lax.fori_loop(0, nch, write_chunk, 0)

    def fill_chunk(j, carry):
        start = pl.multiple_of(j * qb, qb)
        mask_ref[0, pl.ds(start, qb), :] = jnp.full(shape, NEG_INF, F32)
        return carry

    lax.fori_loop(nch, nchunks_total, fill_chunk, 0)


def dsa_select(p1, qicol, kwcol):
    b, s, _ = p1.shape
    nq = s // DSA_QB
    return pl.pallas_call(
        functools.partial(_select_kernel, nchunks_total=nq),
        grid=(b, nq),
        in_specs=[pl.BlockSpec((1, DSA_QB, N_IDX_HEADS * IDX_DIM), lambda bi, i: (bi, i, qicol)),
                  pl.BlockSpec((1, s, 128), lambda bi, i: (bi, 0, kwcol)),
                  pl.BlockSpec((1, DSA_QB, 128), lambda bi, i: (bi, i, kwcol))],
        out_specs=pl.BlockSpec((1, s, DSA_QB), lambda bi, i: (bi, 0, i)),
        out_shape=jax.ShapeDtypeStruct((b, s, s), F32),
        scratch_shapes=[pltpu.VMEM((nq, DSA_QB, DSA_QB), jnp.int32),
                        pltpu.VMEM((1, DSA_QB), jnp.int32)],
        compiler_params=_params("parallel", "arbitrary"),
        name="dsa_select",
    )(p1, p1, p1)


def _dsa_attn_kernel(q_ref, c_ref, ct_ref, mask_ref, tz_ref, wuk_ref, wuvt_ref, o_ref,
                     ql_ref, ot_ref, acc_ref):
    qb = DSA_QB
    i = pl.program_id(1)

    for h in range(N_HEADS):
        qh = q_ref[0, :, h * HEAD_DIM:(h + 1) * HEAD_DIM]
        ql = lax.dot_general(wuk_ref[h], qh, _NT, preferred_element_type=F32)
        ql_ref[h] = (ql * (HEAD_DIM ** -0.5)).astype(BF16)

    def head_body(h, carry):
        acc_ref[...] = jnp.zeros((KV_RANK, qb), F32)

        def chunk(j, ml):
            m, l = ml
            s = jnp.dot(c_ref[0, j], ql_ref[h], preferred_element_type=F32)
            d = 2 * (i - j)
            t_diag = tz_ref[d, h]
            bias = jnp.concatenate(
                [jnp.concatenate([t_diag, tz_ref[d + 1, h]], axis=1),
                 jnp.concatenate([tz_ref[jnp.maximum(d - 1, 0), h], t_diag], axis=1)], axis=0)
            start = pl.multiple_of(j * qb, qb)
            s = s + bias + mask_ref[0, pl.ds(start, qb), :]
            m_new = jnp.maximum(m, jnp.max(s, axis=0, keepdims=True))
            p = jnp.exp(s - m_new)
            alpha = jnp.exp(m - m_new)
            l = alpha * l + jnp.sum(p, axis=0, keepdims=True)
            pv = jnp.dot(ct_ref[0, j], p.astype(BF16), preferred_element_type=F32)
            acc_ref[...] = alpha * acc_ref[...] + pv
            return m_new, l

        init = (jnp.full((1, qb), M_INIT, F32), jnp.zeros((1, qb), F32))
        _, l = lax.fori_loop(0, i + 1, chunk, init)
        o_lat = (acc_ref[...] / l).astype(BF16)
        ot_ref[h] = jnp.dot(wuvt_ref[h], o_lat, preferred_element_type=F32)
        return carry

    lax.fori_loop(0, N_HEADS, head_body, 0)
    o_ref[0] = ot_ref[...].reshape(ATTN_W, qb).T.astype(o_ref.dtype)


def dsa_attention(p1, qcol, c, ct, mask, tz, wuk, wuvt):
    b, s, _ = p1.shape
    nq = s // DSA_QB
    const = lambda shape: pl.BlockSpec(shape, lambda bi, i: (0,) * len(shape))
    return pl.pallas_call(
        _dsa_attn_kernel,
        grid=(b, nq),
        in_specs=[pl.BlockSpec((1, DSA_QB, ATTN_W), lambda bi, i: (bi, i, qcol)),
                  pl.BlockSpec((1, nq, DSA_QB, KV_RANK), lambda bi, i: (bi, 0, 0, 0)),
                  pl.BlockSpec((1, nq, KV_RANK, DSA_QB), lambda bi, i: (bi, 0, 0, 0)),
                  pl.BlockSpec((1, s, DSA_QB), lambda bi, i: (bi, 0, i)),
                  pl.BlockSpec(tz.shape, lambda bi, i: (0, 0, 0, 0), pipeline_mode=pl.Buffered(1)),
                  const(wuk.shape), const(wuvt.shape)],
        out_specs=pl.BlockSpec((1, DSA_QB, ATTN_W), lambda bi, i: (bi, i, 0)),
        out_shape=jax.ShapeDtypeStruct((b, s, ATTN_W), BF16),
        scratch_shapes=[pltpu.VMEM((N_HEADS, KV_RANK, DSA_QB), BF16),
                        pltpu.VMEM((N_HEADS, HEAD_DIM, DSA_QB), F32),
                        pltpu.VMEM((KV_RANK, DSA_QB), F32)],
        compiler_params=_params("parallel", "arbitrary"),
        name="dsa_attention",
    )(p1, c, ct, mask, tz, wuk, wuvt)


def _rmsnorm_kernel(x_ref, g_ref, o_ref):
    o_ref[...] = _rms(x_ref[...], g_ref[...])


def rmsnorm(x, g, tm):
    t, d = x.shape
    return pl.pallas_call(
        _rmsnorm_kernel,
        grid=(t // tm,),
        in_specs=[pl.BlockSpec((tm, d), lambda i: (i, 0)), pl.BlockSpec((1, d), lambda i: (0, 0))],
        out_specs=pl.BlockSpec((tm, d), lambda i: (i, 0)),
        out_shape=jax.ShapeDtypeStruct((t, d), F32),
        compiler_params=_params("parallel"),
        name="final_rmsnorm",
    )(x, g.reshape(1, d))


def _rel_bucket(dist):
    n = jnp.maximum(dist, 0)
    max_exact = REL_BUCKETS // 2
    nf = jnp.maximum(n, 1).astype(F32)
    large = max_exact + (jnp.log(nf / max_exact) / math.log(REL_MAX_DIST / max_exact)
                         * (REL_BUCKETS - max_exact)).astype(jnp.int32)
    large = jnp.minimum(large, REL_BUCKETS - 1)
    return jnp.where(n < max_exact, n, large)


def _dilated_table(rel_bias, window, dilation):
    qi = jnp.arange(BLOCK)[:, None]
    ki = jnp.arange(2 * BLOCK)[None, :]
    rel = qi + BLOCK - ki
    bias = jnp.moveaxis(rel_bias[_rel_bucket(rel * dilation)], -1, 0).astype(F32)
    band = (rel >= 0) & (rel <= window // dilation)
    t0 = jnp.where(band[None], bias, NEG_INF)
    t1 = jnp.where((band & (ki >= BLOCK))[None], bias, NEG_INF)
    return jnp.stack([t0, t1])


def _dsa_bias_tiles(rel_bias, s):
    nd = s // BLOCK
    key = jnp.arange(BLOCK)[:, None]
    qry = jnp.arange(BLOCK)[None, :]
    dist = jnp.arange(nd)[:, None, None] * BLOCK + (qry - key)[None]
    return jnp.moveaxis(rel_bias[_rel_bucket(dist)], -1, 1).astype(F32)


def kernel(x, mem, rel_bias, mem_norm, final_norm, mixer_norm, ffn_norm, w_mem_kv, w_out,
           even_w_in, even_w_gate, even_w_up, even_w_down,
           odd_w_in, odd_kv_norm, odd_w_uk, odd_w_uv,
           odd_w_router, odd_w_gate, odd_w_up, odd_w_down):
    b, s, d = x.shape
    t = b * s
    depth = mixer_norm.shape[0]
    xt = x.reshape(t, d)
    mem2 = mem.reshape(b * N_MEM, d)
    dil_tables = [_dilated_table(rel_bias, w, dl) for w, dl in DIL_PAIRS]
    tz = _dsa_bias_tiles(rel_bias, s)

    for i in range(depth):
        j = i // 2
        kvm = norm_matmul(mem2, mem_norm, w_mem_kv[i].astype(BF16), BF16, 512, 512)
        kvm = kvm.reshape(b, N_MEM, 2 * CROSS_WIDTH)
        wo = w_out[i].astype(BF16)
        if i % 2 == 0:
            w_in = even_w_in[j].astype(BF16)
            q123 = norm_matmul(xt, mixer_norm[i], w_in[:, :3 * ATTN_W], BF16, 512, 1024)
            kv = norm_matmul(xt, mixer_norm[i], w_in[:, 3 * ATTN_W:5 * ATTN_W], BF16, 512, 1024)
            qc = norm_matmul(xt, mixer_norm[i], w_in[:, 5 * ATTN_W:], BF16, 512, CROSS_WIDTH)
            q123 = q123.reshape(b, s, 3 * ATTN_W)
            kv = kv.reshape(b, s, 2 * ATTN_W)
            outs, lses = [], []
            for grp, (_, dl) in enumerate(DIL_PAIRS):
                o, lse = dilated_branch(q123, kv, dil_tables[grp], grp, dl)
                outs.append(o.reshape(t, ATTN_W))
                lses.append(lse.reshape(t, ATTN_W))
            mix = combine_groups(outs, lses, 512)
            cross = cross_attention(qc.reshape(b, s, CROSS_WIDTH), 0, kvm, 512)
        else:
            o_q, o_c, o_qi, o_ki, o_wi, o_qc = 0, 1024, 1280, 1792, 1856, 1864
            w = odd_w_in[j]
            pad = jnp.zeros((d, 128 - IDX_DIM - N_IDX_HEADS), w.dtype)
            w_in = jnp.concatenate(
                [w[:, o_q:o_c], w[:, o_qi:o_ki], w[:, o_c:o_qi], w[:, o_qc:],
                 w[:, o_ki:o_wi], w[:, o_wi:o_qc], pad], axis=1).astype(BF16)
            p1 = norm_matmul(xt, mixer_norm[i], w_in, BF16, 512, w_in.shape[1])
            p1 = p1.reshape(b, s, w_in.shape[1])
            c, ct = kv_latent_norm(p1, 6, odd_kv_norm[j])
            mask = dsa_select(p1, 2, 16)
            wuk = jnp.transpose(odd_w_uk[j], (1, 0, 2)).astype(BF16)
            wuvt = jnp.transpose(odd_w_uv[j], (1, 2, 0)).astype(BF16)
            mix = dsa_attention(p1, 0, c, ct, mask, tz, wuk, wuvt).reshape(t, ATTN_W)
            cross = cross_attention(p1, 7, kvm, 512)
        xt = out_projection(mix, cross.reshape(t, CROSS_WIDTH), wo[:ATTN_W], wo[ATTN_W:], xt, 512)
        if i % 2 == 0:
            xt = swiglu_ffn(xt, ffn_norm[i], even_w_gate[j].astype(BF16), even_w_up[j].astype(BF16),
                            even_w_down[j].astype(BF16), 512, 1408)
        else:
            wr = jnp.pad(odd_w_router[j], ((0, 0), (0, 128 - N_EXPERTS)))
            gates = router_gates(xt, ffn_norm[i], wr, 512)
            xt = moe_ffn(xt, ffn_norm[i], gates, odd_w_gate[j].astype(BF16), odd_w_up[j].astype(BF16),
                         odd_w_down[j].astype(BF16), 512, 512)
    return rmsnorm(xt, final_norm, 512).reshape(b, s, d)
```

```python
import functools
import math

import jax
import jax.numpy as jnp
from jax import lax
from jax.experimental import pallas as pl
from jax.experimental.pallas import tpu as pltpu

D_MODEL = 1024
N_HEADS = 16
HEAD_DIM = 64
ATTN_W = N_HEADS * HEAD_DIM
DIL_PAIRS = ((128, 1), (512, 4), (2048, 16))
BLOCK = 128
KV_RANK = 256
N_IDX_HEADS = 8
IDX_DIM = 64
TOPK = 256
N_MEM = 256
N_CROSS_HEADS = 4
CROSS_WIDTH = 256
REL_BUCKETS = 32
REL_MAX_DIST = 2048
N_EXPERTS = 8
RMS_EPS = 1e-6
NEG_INF = -1e30
M_INIT = -5e29

F32 = jnp.float32
BF16 = jnp.bfloat16
V7X_VMEM_LIMIT = 56 * 1024 * 1024
DSA_QB = 256
DSA_HEAD_GROUP = 8
INT_MIN = -(2 ** 31)

_NT = (((1,), (1,)), ((), ()))


def _params(*sem):
    return pltpu.CompilerParams(dimension_semantics=sem, vmem_limit_bytes=V7X_VMEM_LIMIT)


def _rms(x, g):
    return x * lax.rsqrt(jnp.mean(x * x, axis=-1, keepdims=True) + RMS_EPS) * g


def _norm_mm_kernel(x_ref, g_ref, w_ref, o_ref, xn_ref):
    @pl.when(pl.program_id(1) == 0)
    def _():
        xn_ref[...] = _rms(x_ref[...].astype(F32), g_ref[...]).astype(BF16)

    o_ref[...] = jnp.dot(xn_ref[...], w_ref[...], preferred_element_type=F32).astype(o_ref.dtype)


def norm_matmul(x, g, w, out_dtype, tm, tn):
    t, k = x.shape
    n = w.shape[1]
    return pl.pallas_call(
        _norm_mm_kernel,
        grid=(t // tm, n // tn),
        in_specs=[pl.BlockSpec((tm, k), lambda i, j: (i, 0)),
                  pl.BlockSpec((1, k), lambda i, j: (0, 0)),
                  pl.BlockSpec((k, tn), lambda i, j: (0, j))],
        out_specs=pl.BlockSpec((tm, tn), lambda i, j: (i, j)),
        out_shape=jax.ShapeDtypeStruct((t, n), out_dtype),
        scratch_shapes=[pltpu.VMEM((tm, k), BF16)],
        compiler_params=_params("parallel", "arbitrary"),
        name="norm_matmul",
    )(x, g.reshape(1, k), w)


def _dil_kernel(q_ref, kp_ref, kc_ref, vp_ref, vc_ref, tab_ref, o_ref, lse_ref):
    first = (pl.program_id(2) == 0).astype(jnp.int32)
    for h in range(N_HEADS):
        sl = slice(h * HEAD_DIM, (h + 1) * HEAD_DIM)
        q = q_ref[0, :, sl]
        k = jnp.concatenate([kp_ref[0, :, sl], kc_ref[0, :, sl]], axis=0)
        v = jnp.concatenate([vp_ref[0, :, sl], vc_ref[0, :, sl]], axis=0)
        s = lax.dot_general(q, k, _NT, preferred_element_type=F32) * (HEAD_DIM ** -0.5)
        s = s + tab_ref[first, h]
        m = jnp.max(s, axis=-1, keepdims=True)
        p = jnp.exp(s - m)
        l = jnp.sum(p, axis=-1, keepdims=True)
        o = jnp.dot(p.astype(BF16), v, preferred_element_type=F32)
        o_ref[0, :, sl] = o / l
        lse_ref[0, :, sl] = jnp.broadcast_to(m + jnp.log(l), (BLOCK, HEAD_DIM))


def dilated_branch(q123, kv, table, group, dilation):
    b, s, _ = q123.shape
    ln = s // dilation
    nb = ln // BLOCK
    qv = q123.reshape(b, ln, dilation * 3 * ATTN_W)
    kvv = kv.reshape(b, ln, dilation * 2 * ATTN_W)
    blk = (1, BLOCK, ATTN_W)
    prev = lambda n: jnp.maximum(n - 1, 0)
    out = jax.ShapeDtypeStruct((b, ln, dilation * ATTN_W), F32)
    o, lse = pl.pallas_call(
        _dil_kernel,
        grid=(b, dilation, nb),
        in_specs=[pl.BlockSpec(blk, lambda bi, r, n: (bi, n, 3 * r + group)),
                  pl.BlockSpec(blk, lambda bi, r, n: (bi, prev(n), 2 * r)),
                  pl.BlockSpec(blk, lambda bi, r, n: (bi, n, 2 * r)),
                  pl.BlockSpec(blk, lambda bi, r, n: (bi, prev(n), 2 * r + 1)),
                  pl.BlockSpec(blk, lambda bi, r, n: (bi, n, 2 * r + 1)),
                  pl.BlockSpec(table.shape, lambda bi, r, n: (0, 0, 0, 0))],
        out_specs=[pl.BlockSpec(blk, lambda bi, r, n: (bi, n, r)),
                   pl.BlockSpec(blk, lambda bi, r, n: (bi, n, r))],
        out_shape=[out, out],
        compiler_params=_params("parallel", "parallel", "arbitrary"),
        name=f"dilated_attn_d{dilation}",
    )(qv, kvv, kvv, kvv, kvv, table)
    return o.reshape(b, s, ATTN_W), lse.reshape(b, s, ATTN_W)


def _combine_kernel(o1, o2, o3, l1, l2, l3, out_ref):
    a1, a2, a3 = l1[...], l2[...], l3[...]
    m = jnp.maximum(jnp.maximum(a1, a2), a3)
    w1, w2, w3 = jnp.exp(a1 - m), jnp.exp(a2 - m), jnp.exp(a3 - m)
    num = w1 * o1[...] + w2 * o2[...] + w3 * o3[...]
    out_ref[...] = (num / (w1 + w2 + w3)).astype(out_ref.dtype)


def combine_groups(outs, lses, tm):
    t, w = outs[0].shape
    spec = pl.BlockSpec((tm, w), lambda i: (i, 0))
    return pl.pallas_call(
        _combine_kernel,
        grid=(t // tm,),
        in_specs=[spec] * 6,
        out_specs=spec,
        out_shape=jax.ShapeDtypeStruct((t, w), BF16),
        compiler_params=_params("parallel"),
        name="combine_groups",
    )(*outs, *lses)


def _cross_kernel(q_ref, kv_ref, o_ref):
    for h in range(N_CROSS_HEADS):
        sl = slice(h * HEAD_DIM, (h + 1) * HEAD_DIM)
        vsl = slice(CROSS_WIDTH + h * HEAD_DIM, CROSS_WIDTH + (h + 1) * HEAD_DIM)
        s = lax.dot_general(q_ref[0, :, sl], kv_ref[0, :, sl], _NT,
                            preferred_element_type=F32) * (HEAD_DIM ** -0.5)
        m = jnp.max(s, axis=-1, keepdims=True)
        p = jnp.exp(s - m)
        l = jnp.sum(p, axis=-1, keepdims=True)
        o = jnp.dot(p.astype(BF16), kv_ref[0, :, vsl], preferred_element_type=F32)
        o_ref[0, :, sl] = (o / l).astype(o_ref.dtype)


def cross_attention(qsrc, qcol, kv, tm):
    b, s, _ = qsrc.shape
    return pl.pallas_call(
        _cross_kernel,
        grid=(b, s // tm),
        in_specs=[pl.BlockSpec((1, tm, CROSS_WIDTH), lambda bi, i: (bi, i, qcol)),
                  pl.BlockSpec((1, N_MEM, 2 * CROSS_WIDTH), lambda bi, i: (bi, 0, 0))],
        out_specs=pl.BlockSpec((1, tm, CROSS_WIDTH), lambda bi, i: (bi, i, 0)),
        out_shape=jax.ShapeDtypeStruct((b, s, CROSS_WIDTH), BF16),
        compiler_params=_params("parallel", "parallel"),
        name="cross_attn",
    )(qsrc, kv)


def _outproj_kernel(mix_ref, cr_ref, wa_ref, wb_ref, x_ref, o_ref):
    acc = jnp.dot(mix_ref[...], wa_ref[...], preferred_element_type=F32)
    acc = acc + jnp.dot(cr_ref[...], wb_ref[...], preferred_element_type=F32)
    o_ref[...] = x_ref[...] + acc


def out_projection(mix, cross, wa, wb, x, tm):
    t, d = x.shape
    return pl.pallas_call(
        _outproj_kernel,
        grid=(t // tm,),
        in_specs=[pl.BlockSpec((tm, ATTN_W), lambda i: (i, 0)),
                  pl.BlockSpec((tm, CROSS_WIDTH), lambda i: (i, 0)),
                  pl.BlockSpec(wa.shape, lambda i: (0, 0)),
                  pl.BlockSpec(wb.shape, lambda i: (0, 0)),
                  pl.BlockSpec((tm, d), lambda i: (i, 0))],
        out_specs=pl.BlockSpec((tm, d), lambda i: (i, 0)),
        out_shape=jax.ShapeDtypeStruct((t, d), F32),
        compiler_params=_params("parallel"),
        name="out_projection",
    )(mix, cross, wa, wb, x)


def _swiglu_kernel(x_ref, g_ref, wg_ref, wu_ref, wd_ref, o_ref, hn_ref):
    @pl.when(pl.program_id(1) == 0)
    def _():
        x = x_ref[...]
        hn_ref[...] = _rms(x, g_ref[...]).astype(BF16)
        o_ref[...] = x

    hn = hn_ref[...]
    a = jnp.dot(hn, wg_ref[...], preferred_element_type=F32)
    u = jnp.dot(hn, wu_ref[...], preferred_element_type=F32)
    act = (a * jax.nn.sigmoid(a) * u).astype(BF16)
    o_ref[...] += jnp.dot(act, wd_ref[...], preferred_element_type=F32)


def swiglu_ffn(x, g, wg, wu, wd, tm, tf):
    t, d = x.shape
    f = wg.shape[1]
    return pl.pallas_call(
        _swiglu_kernel,
        grid=(t // tm, f // tf),
        in_specs=[pl.BlockSpec((tm, d), lambda i, j: (i, 0)),
                  pl.BlockSpec((1, d), lambda i, j: (0, 0)),
                  pl.BlockSpec((d, tf), lambda i, j: (0, j)),
                  pl.BlockSpec((d, tf), lambda i, j: (0, j)),
                  pl.BlockSpec((tf, d), lambda i, j: (j, 0))],
        out_specs=pl.BlockSpec((tm, d), lambda i, j: (i, 0)),
        out_shape=jax.ShapeDtypeStruct((t, d), F32),
        scratch_shapes=[pltpu.VMEM((tm, d), BF16)],
        compiler_params=_params("parallel", "arbitrary"),
        name="swiglu_ffn",
    )(x, g.reshape(1, d), wg, wu, wd)


def _router_kernel(x_ref, g_ref, wr_ref, gates_ref):
    hn = _rms(x_ref[...], g_ref[...])
    logits = jnp.dot(hn, wr_ref[...], preferred_element_type=F32,
                     precision=lax.Precision.HIGHEST)
    lane = lax.broadcasted_iota(jnp.int32, logits.shape, 1)
    lg = jnp.where(lane < N_EXPERTS, logits, -jnp.inf)
    m1 = jnp.max(lg, axis=-1, keepdims=True)
    i1 = jnp.min(jnp.where(lg == m1, lane, 128), axis=-1, keepdims=True)
    lg2 = jnp.where(lane == i1, -jnp.inf, lg)
    m2 = jnp.max(lg2, axis=-1, keepdims=True)
    i2 = jnp.min(jnp.where(lg2 == m2, lane, 128), axis=-1, keepdims=True)
    e = jnp.exp(m2 - m1)
    g1 = 1.0 / (1.0 + e)
    g2 = e / (1.0 + e)
    gates_ref[...] = jnp.where(lane == i1, g1, 0.0) + jnp.where(lane == i2, g2, 0.0)


def router_gates(x, g, wr_pad, tm):
    t, d = x.shape
    return pl.pallas_call(
        _router_kernel,
        grid=(t // tm,),
        in_specs=[pl.BlockSpec((tm, d), lambda i: (i, 0)),
                  pl.BlockSpec((1, d), lambda i: (0, 0)),
                  pl.BlockSpec((d, 128), lambda i: (0, 0))],
        out_specs=pl.BlockSpec((tm, 128), lambda i: (i, 0)),
        out_shape=jax.ShapeDtypeStruct((t, 128), F32),
        compiler_params=_params("parallel"),
        name="router",
    )(x, g.reshape(1, d), wr_pad)


def _moe_kernel(x_ref, g_ref, gates_ref, wg_ref, wu_ref, wd_ref, o_ref, hn_ref):
    e = pl.program_id(1)

    @pl.when((e == 0) & (pl.program_id(2) == 0))
    def _():
        x = x_ref[...]
        hn_ref[...] = _rms(x, g_ref[...]).astype(BF16)
        o_ref[...] = x

    hn = hn_ref[...]
    a = jnp.dot(hn, wg_ref[...], preferred_element_type=F32)
    u = jnp.dot(hn, wu_ref[...], preferred_element_type=F32)
    act = (a * jax.nn.sigmoid(a) * u).astype(BF16)
    gates = gates_ref[...]
    lane = lax.broadcasted_iota(jnp.int32, gates.shape, 1)
    ge = jnp.sum(jnp.where(lane == e, gates, 0.0), axis=-1, keepdims=True)
    o_ref[...] += ge * jnp.dot(act, wd_ref[...], preferred_element_type=F32)


def moe_ffn(x, g, gates, wg, wu, wd, tm, tf):
    t, d = x.shape
    ne, _, f = wg.shape
    return pl.pallas_call(
        _moe_kernel,
        grid=(t // tm, ne, f // tf),
        in_specs=[pl.BlockSpec((tm, d), lambda i, e, j: (i, 0)),
                  pl.BlockSpec((1, d), lambda i, e, j: (0, 0)),
                  pl.BlockSpec((tm, 128), lambda i, e, j: (i, 0)),
                  pl.BlockSpec((None, d, tf), lambda i, e, j: (e, 0, j)),
                  pl.BlockSpec((None, d, tf), lambda i, e, j: (e, 0, j)),
                  pl.BlockSpec((None, tf, d), lambda i, e, j: (e, j, 0))],
        out_specs=pl.BlockSpec((tm, d), lambda i, e, j: (i, 0)),
        out_shape=jax.ShapeDtypeStruct((t, d), F32),
        scratch_shapes=[pltpu.VMEM((tm, d), BF16)],
        compiler_params=_params("parallel", "arbitrary", "arbitrary"),
        name="moe_ffn",
    )(x, g.reshape(1, d), gates, wg, wu, wd)


def _kvnorm_kernel(c_ref, g_ref, o_ref, ot_ref):
    y = _rms(c_ref[0].astype(F32), g_ref[...])
    o_ref[0, 0] = y.astype(BF16)
    ot_ref[0, 0] = y.T.astype(BF16)


def kv_latent_norm(p1, ccol, g):
    b, s, _ = p1.shape
    nc = s // DSA_QB
    out = jax.ShapeDtypeStruct((b, nc, DSA_QB, KV_RANK), BF16)
    blk = pl.BlockSpec((1, 1, DSA_QB, KV_RANK), lambda bi, j: (bi, j, 0, 0))
    return pl.pallas_call(
        _kvnorm_kernel,
        grid=(b, nc),
        in_specs=[pl.BlockSpec((1, DSA_QB, KV_RANK), lambda bi, j: (bi, j, ccol)),
                  pl.BlockSpec((1, KV_RANK), lambda bi, j: (0, 0))],
        out_specs=[blk, blk],
        out_shape=[out, out],
        compiler_params=_params("parallel", "parallel"),
        name="kv_latent_norm",
    )(p1, g.reshape(1, KV_RANK))


def _select_kernel(qi_ref, kall_ref, kq_ref, mask_ref, keys_ref, jcut_ref, *, nchunks_total):
    qb = DSA_QB
    i = pl.program_id(1)
    nch = i + 1
    shape = (qb, qb)
    row = lax.broadcasted_iota(jnp.int32, shape, 0)
    qpos = i * qb + lax.broadcasted_iota(jnp.int32, shape, 1)

    wt = kq_ref[0].astype(F32).T
    wscale = (N_IDX_HEADS ** -0.5) * (IDX_DIM ** -0.5)

    def score_chunk(j, carry):
        start = pl.multiple_of(j * qb, qb)
        kc = kall_ref[0, pl.ds(start, qb), :][:, :IDX_DIM]
        acc = jnp.zeros(shape, F32)
        for h in range(N_IDX_HEADS):
            qh = qi_ref[0, :, h * IDX_DIM:(h + 1) * IDX_DIM]
            s = lax.dot_general(kc, qh, _NT, preferred_element_type=F32)
            acc = acc + jnp.maximum(s, 0.0) * (wt[IDX_DIM + h:IDX_DIM + h + 1, :] * wscale)
        acc = jnp.where(j * qb + row <= qpos, acc, NEG_INF)
        acc = jnp.where(acc == 0.0, 0.0, acc)
        bits = pltpu.bitcast(acc, jnp.int32)
        keys_ref[j] = bits ^ ((bits >> 31) & 0x7FFFFFFF)
        return carry

    lax.fori_loop(0, nch, score_chunk, 0)

    def count(pred):
        def body(j, c):
            hit = jnp.where(pred(keys_ref[j], j), 1, 0)
            return c + jnp.sum(hit.reshape(qb // 8, 8, qb), axis=0)
        c = lax.fori_loop(0, nch, body, jnp.zeros((8, qb), jnp.int32))
        return jnp.sum(c, axis=0, keepdims=True)

    def count_ge(cand):
        return count(lambda key, j: key >= cand)

    thr = jnp.where(count_ge(jnp.zeros((1, qb), jnp.int32)) >= TOPK, 0, INT_MIN)

    def bit_body(t, thr):
        cand = thr + jnp.left_shift(jnp.int32(1), 30 - t)
        return jnp.where(count_ge(cand) >= TOPK, cand, thr)

    thr = lax.fori_loop(0, 31, bit_body, thr)

    n_gt = count(lambda key, j: key > thr)
    n_ge = count_ge(thr)
    need = TOPK - n_gt

    jcut_ref[...] = jnp.full((1, qb), 1 << 30, jnp.int32)

    @pl.when(jnp.max(n_ge) > TOPK)
    def _():
        def idx_body(t, x):
            cand = x + jnp.left_shift(jnp.int32(1), 10 - t)
            below = count(lambda key, j: (key == thr) & (j * qb + row <= cand - 1))
            return jnp.where(below < need, cand, x)
        jcut_ref[...] = lax.fori_loop(0, 11, idx_body, jnp.zeros((1, qb), jnp.int32))

    jcut = jcut_ref[...]

    def write_chunk(j, carry):
        key = keys_ref[j]
        spos = j * qb + row
        sel = (key > thr) | ((key == thr) & (spos <= jcut))
        sel = sel & (spos <= qpos)
        start = pl.multiple_of(j * qb, qb)
        mask_ref[0, pl.ds(start, qb), :] = jnp.where(sel, 0.0, NEG_INF)
        return carry

    lax.fori_loop(0, nch, write_chunk, 0)

    def fill_chunk(j, carry):
        start = pl.multiple_of(j * qb, qb)
        mask_ref[0, pl.ds(start, qb), :] = jnp.full(shape, NEG_INF, F32)
        return carry

    lax.fori_loop(nch, nchunks_total, fill_chunk, 0)


def dsa_select(p1, qicol, kwcol):
    b, s, _ = p1.shape
    nq = s // DSA_QB
    return pl.pallas_call(
        functools.partial(_select_kernel, nchunks_total=nq),
        grid=(b, nq),
        in_specs=[pl.BlockSpec((1, DSA_QB, N_IDX_HEADS * IDX_DIM), lambda bi, i: (bi, i, qicol)),
                  pl.BlockSpec((1, s, 128), lambda bi, i: (bi, 0, kwcol)),
                  pl.BlockSpec((1, DSA_QB, 128), lambda bi, i: (bi, i, kwcol))],
        out_specs=pl.BlockSpec((1, s, DSA_QB), lambda bi, i: (bi, 0, i)),
        out_shape=jax.ShapeDtypeStruct((b, s, s), F32),
        scratch_shapes=[pltpu.VMEM((nq, DSA_QB, DSA_QB), jnp.int32),
                        pltpu.VMEM((1, DSA_QB), jnp.int32)],
        compiler_params=_params("parallel", "arbitrary"),
        name="dsa_select",
    )(p1, p1, p1)


def _dsa_attn_kernel(q_ref, c_ref, ct_ref, mask_ref, tz_ref, wuk_ref, wuvt_ref, o_ref,
                     ql_ref, ot_ref, acc_ref):
    qb = DSA_QB
    hg = DSA_HEAD_GROUP
    width = hg * qb
    i = pl.program_id(1)

    for h in range(N_HEADS):
        qh = q_ref[0, :, h * HEAD_DIM:(h + 1) * HEAD_DIM]
        ql = lax.dot_general(wuk_ref[h], qh, _NT, preferred_element_type=F32)
        ql_ref[h // hg, :, (h % hg) * qb:(h % hg + 1) * qb] = (ql * (HEAD_DIM ** -0.5)).astype(BF16)

    def group_body(g, carry):
        acc_ref[...] = jnp.zeros((KV_RANK, width), F32)

        def chunk(j, ml):
            m, l = ml
            s = jnp.dot(c_ref[0, j], ql_ref[g], preferred_element_type=F32)
            d = 2 * (i - j)
            start = pl.multiple_of(j * qb, qb)
            mk = mask_ref[0, pl.ds(start, qb), :]
            tiles = []
            for hh in range(hg):
                h = g * hg + hh
                t_diag = tz_ref[d, h]
                tiles.append(mk + jnp.concatenate(
                    [jnp.concatenate([t_diag, tz_ref[d + 1, h]], axis=1),
                     jnp.concatenate([tz_ref[jnp.maximum(d - 1, 0), h], t_diag], axis=1)], axis=0))
            s = s + jnp.concatenate(tiles, axis=1)
            m_new = jnp.maximum(m, jnp.max(s, axis=0, keepdims=True))
            p = jnp.exp(s - m_new)
            alpha = jnp.exp(m - m_new)
            l = alpha * l + jnp.sum(p, axis=0, keepdims=True)
            pv = jnp.dot(ct_ref[0, j], p.astype(BF16), preferred_element_type=F32)
            acc_ref[...] = alpha * acc_ref[...] + pv
            return m_new, l

        init = (jnp.full((1, width), M_INIT, F32), jnp.zeros((1, width), F32))
        _, l = lax.fori_loop(0, i + 1, chunk, init)
        o_lat = (acc_ref[...] / l).astype(BF16)
        for hh in range(hg):
            h = g * hg + hh
            ot_ref[h] = jnp.dot(wuvt_ref[h], o_lat[:, hh * qb:(hh + 1) * qb],
                                preferred_element_type=F32)
        return carry

    lax.fori_loop(0, N_HEADS // hg, group_body, 0)
    o_ref[0] = ot_ref[...].reshape(ATTN_W, qb).T.astype(o_ref.dtype)


def dsa_attention(p1, qcol, c, ct, mask, tz, wuk, wuvt):
    b, s, _ = p1.shape
    nq = s // DSA_QB
    const = lambda shape: pl.BlockSpec(shape, lambda bi, i: (0,) * len(shape))
    return pl.pallas_call(
        _dsa_attn_kernel,
        grid=(b, nq),
        in_specs=[pl.BlockSpec((1, DSA_QB, ATTN_W), lambda bi, i: (bi, i, qcol)),
                  pl.BlockSpec((1, nq, DSA_QB, KV_RANK), lambda bi, i: (bi, 0, 0, 0)),
                  pl.BlockSpec((1, nq, KV_RANK, DSA_QB), lambda bi, i: (bi, 0, 0, 0)),
                  pl.BlockSpec((1, s, DSA_QB), lambda bi, i: (bi, 0, i)),
                  pl.BlockSpec(tz.shape, lambda bi, i: (0, 0, 0, 0), pipeline_mode=pl.Buffered(1)),
                  const(wuk.shape), const(wuvt.shape)],
        out_specs=pl.BlockSpec((1, DSA_QB, ATTN_W), lambda bi, i: (bi, i, 0)),
        out_shape=jax.ShapeDtypeStruct((b, s, ATTN_W), BF16),
        scratch_shapes=[pltpu.VMEM((N_HEADS // DSA_HEAD_GROUP, KV_RANK, DSA_HEAD_GROUP * DSA_QB), BF16),
                        pltpu.VMEM((N_HEADS, HEAD_DIM, DSA_QB), F32),
                        pltpu.VMEM((KV_RANK, DSA_HEAD_GROUP * DSA_QB), F32)],
        compiler_params=_params("parallel", "arbitrary"),
        name="dsa_attention",
    )(p1, c, ct, mask, tz, wuk, wuvt)


def _rmsnorm_kernel(x_ref, g_ref, o_ref):
    o_ref[...] = _rms(x_ref[...], g_ref[...])


def rmsnorm(x, g, tm):
    t, d = x.shape
    return pl.pallas_call(
        _rmsnorm_kernel,
        grid=(t // tm,),
        in_specs=[pl.BlockSpec((tm, d), lambda i: (i, 0)), pl.BlockSpec((1, d), lambda i: (0, 0))],
        out_specs=pl.BlockSpec((tm, d), lambda i: (i, 0)),
        out_shape=jax.ShapeDtypeStruct((t, d), F32),
        compiler_params=_params("parallel"),
        name="final_rmsnorm",
    )(x, g.reshape(1, d))


def _rel_bucket(dist):
    n = jnp.maximum(dist, 0)
    max_exact = REL_BUCKETS // 2
    nf = jnp.maximum(n, 1).astype(F32)
    large = max_exact + (jnp.log(nf / max_exact) / math.log(REL_MAX_DIST / max_exact)
                         * (REL_BUCKETS - max_exact)).astype(jnp.int32)
    large = jnp.minimum(large, REL_BUCKETS - 1)
    return jnp.where(n < max_exact, n, large)


def _bias_table_kernel(bkt_ref, rb_ref, o_ref):
    bkt = bkt_ref[0]
    for h in range(N_HEADS):
        acc = jnp.full(bkt.shape, NEG_INF, F32)
        for k in range(REL_BUCKETS):
            acc = jnp.where(bkt == k, rb_ref[k, h], acc)
        o_ref[0, h] = acc


def bias_table(bkt, rel_bias):
    n, r, c = bkt.shape
    return pl.pallas_call(
        _bias_table_kernel,
        grid=(n,),
        in_specs=[pl.BlockSpec((1, r, c), lambda i: (i, 0, 0)),
                  pl.BlockSpec(memory_space=pltpu.SMEM)],
        out_specs=pl.BlockSpec((1, N_HEADS, r, c), lambda i: (i, 0, 0, 0)),
        out_shape=jax.ShapeDtypeStruct((n, N_HEADS, r, c), F32),
        compiler_params=_params("parallel"),
        name="bias_table",
    )(bkt, rel_bias.astype(F32))


def _dilated_buckets(window, dilation):
    qi = jnp.arange(BLOCK)[:, None]
    ki = jnp.arange(2 * BLOCK)[None, :]
    rel = qi + BLOCK - ki
    bkt = _rel_bucket(rel * dilation)
    band = (rel >= 0) & (rel <= window // dilation)
    return jnp.stack([jnp.where(band, bkt, -1),
                      jnp.where(band & (ki >= BLOCK), bkt, -1)]).astype(jnp.int32)


def _dsa_buckets(s):
    nd = s // BLOCK
    key = jnp.arange(BLOCK)[:, None]
    qry = jnp.arange(BLOCK)[None, :]
    dist = jnp.arange(nd)[:, None, None] * BLOCK + (qry - key)[None]
    return _rel_bucket(dist).astype(jnp.int32)


def kernel(x, mem, rel_bias, mem_norm, final_norm, mixer_norm, ffn_norm, w_mem_kv, w_out,
           even_w_in, even_w_gate, even_w_up, even_w_down,
           odd_w_in, odd_kv_norm, odd_w_uk, odd_w_uv,
           odd_w_router, odd_w_gate, odd_w_up, odd_w_down):
    b, s, d = x.shape
    t = b * s
    depth = mixer_norm.shape[0]
    xt = x.reshape(t, d)
    mem2 = mem.reshape(b * N_MEM, d)
    dil_tables = [bias_table(_dilated_buckets(w, dl), rel_bias) for w, dl in DIL_PAIRS]
    tz = bias_table(_dsa_buckets(s), rel_bias)

    for i in range(depth):
        j = i // 2
        kvm = norm_matmul(mem2, mem_norm, w_mem_kv[i].astype(BF16), BF16, 512, 512)
        kvm = kvm.reshape(b, N_MEM, 2 * CROSS_WIDTH)
        wo = w_out[i].astype(BF16)
        if i % 2 == 0:
            w_in = even_w_in[j].astype(BF16)
            q123 = norm_matmul(xt, mixer_norm[i], w_in[:, :3 * ATTN_W], BF16, 512, 1024)
            kv = norm_matmul(xt, mixer_norm[i], w_in[:, 3 * ATTN_W:5 * ATTN_W], BF16, 512, 1024)
            qc = norm_matmul(xt, mixer_norm[i], w_in[:, 5 * ATTN_W:], BF16, 512, CROSS_WIDTH)
            q123 = q123.reshape(b, s, 3 * ATTN_W)
            kv = kv.reshape(b, s, 2 * ATTN_W)
            outs, lses = [], []
            for grp, (_, dl) in enumerate(DIL_PAIRS):
                o, lse = dilated_branch(q123, kv, dil_tables[grp], grp, dl)
                outs.append(o.reshape(t, ATTN_W))
                lses.append(lse.reshape(t, ATTN_W))
            mix = combine_groups(outs, lses, 512)
            cross = cross_attention(qc.reshape(b, s, CROSS_WIDTH), 0, kvm, 512)
        else:
            o_q, o_c, o_qi, o_ki, o_wi, o_qc = 0, 1024, 1280, 1792, 1856, 1864
            w = odd_w_in[j]
            pad = jnp.zeros((d, 128 - IDX_DIM - N_IDX_HEADS), w.dtype)
            w_in = jnp.concatenate(
                [w[:, o_q:o_c], w[:, o_qi:o_ki], w[:, o_c:o_qi], w[:, o_qc:],
                 w[:, o_ki:o_wi], w[:, o_wi:o_qc], pad], axis=1).astype(BF16)
            p1 = norm_matmul(xt, mixer_norm[i], w_in, BF16, 512, w_in.shape[1])
            p1 = p1.reshape(b, s, w_in.shape[1])
            c, ct = kv_latent_norm(p1, 6, odd_kv_norm[j])
            mask = dsa_select(p1, 2, 16)
            wuk = jnp.transpose(odd_w_uk[j], (1, 0, 2)).astype(BF16)
            wuvt = jnp.transpose(odd_w_uv[j], (1, 2, 0)).astype(BF16)
            mix = dsa_attention(p1, 0, c, ct, mask, tz, wuk, wuvt).reshape(t, ATTN_W)
            cross = cross_attention(p1, 7, kvm, 512)
        xt = out_projection(mix, cross.reshape(t, CROSS_WIDTH), wo[:ATTN_W], wo[ATTN_W:], xt, 512)
        if i % 2 == 0:
            xt = swiglu_ffn(xt, ffn_norm[i], even_w_gate[j].astype(BF16), even_w_up[j].astype(BF16),
                            even_w_down[j].astype(BF16), 512, 1408)
        else:
            wr = jnp.pad(odd_w_router[j], ((0, 0), (0, 128 - N_EXPERTS)))
            gates = router_gates(xt, ffn_norm[i], wr, 512)
            xt = moe_ffn(xt, ffn_norm[i], gates, odd_w_gate[j].astype(BF16), odd_w_up[j].astype(BF16),
                         odd_w_down[j].astype(BF16), 512, 512)
    return rmsnorm(xt, final_norm, 512).reshape(b, s, d)
```

```python
import functools
import math

import jax
import jax.numpy as jnp
from jax import lax
from jax.experimental import pallas as pl
from jax.experimental.pallas import tpu as pltpu

D_MODEL = 1024
N_HEADS = 16
HEAD_DIM = 64
ATTN_W = N_HEADS * HEAD_DIM
DIL_PAIRS = ((128, 1), (512, 4), (2048, 16))
BLOCK = 128
KV_RANK = 256
N_IDX_HEADS = 8
IDX_DIM = 64
TOPK = 256
N_MEM = 256
N_CROSS_HEADS = 4
CROSS_WIDTH = 256
REL_BUCKETS = 32
REL_MAX_DIST = 2048
N_EXPERTS = 8
RMS_EPS = 1e-6
NEG_INF = -1e30
M_INIT = -5e29

F32 = jnp.float32
BF16 = jnp.bfloat16
V7X_VMEM_LIMIT = 56 * 1024 * 1024
DSA_QB = 256
DSA_HEAD_GROUP = 8
INT_MIN = -(2 ** 31)
MOE_TILE = 512
ROUTE_TM = 256

_NT = (((1,), (1,)), ((), ()))


def _params(*sem):
    return pltpu.CompilerParams(dimension_semantics=sem, vmem_limit_bytes=V7X_VMEM_LIMIT)


def _rms(x, g):
    return x * lax.rsqrt(jnp.mean(x * x, axis=-1, keepdims=True) + RMS_EPS) * g


def _norm_mm_kernel(x_ref, g_ref, w_ref, o_ref, xn_ref):
    @pl.when(pl.program_id(1) == 0)
    def _():
        xn_ref[...] = _rms(x_ref[...].astype(F32), g_ref[...]).astype(BF16)

    o_ref[...] = jnp.dot(xn_ref[...], w_ref[...], preferred_element_type=F32).astype(o_ref.dtype)


def norm_matmul(x, g, w, out_dtype, tm, tn):
    t, k = x.shape
    n = w.shape[1]
    return pl.pallas_call(
        _norm_mm_kernel,
        grid=(t // tm, n // tn),
        in_specs=[pl.BlockSpec((tm, k), lambda i, j: (i, 0)),
                  pl.BlockSpec((1, k), lambda i, j: (0, 0)),
                  pl.BlockSpec((k, tn), lambda i, j: (0, j))],
        out_specs=pl.BlockSpec((tm, tn), lambda i, j: (i, j)),
        out_shape=jax.ShapeDtypeStruct((t, n), out_dtype),
        scratch_shapes=[pltpu.VMEM((tm, k), BF16)],
        compiler_params=_params("parallel", "arbitrary"),
        name="norm_matmul",
    )(x, g.reshape(1, k), w)


def _dil_kernel(q_ref, kp_ref, kc_ref, vp_ref, vc_ref, tab_ref, o_ref, lse_ref):
    first = (pl.program_id(2) == 0).astype(jnp.int32)
    for h in range(N_HEADS):
        sl = slice(h * HEAD_DIM, (h + 1) * HEAD_DIM)
        q = q_ref[0, :, sl]
        k = jnp.concatenate([kp_ref[0, :, sl], kc_ref[0, :, sl]], axis=0)
        v = jnp.concatenate([vp_ref[0, :, sl], vc_ref[0, :, sl]], axis=0)
        s = lax.dot_general(q, k, _NT, preferred_element_type=F32) * (HEAD_DIM ** -0.5)
        s = s + tab_ref[first, h]
        m = jnp.max(s, axis=-1, keepdims=True)
        p = jnp.exp(s - m)
        l = jnp.sum(p, axis=-1, keepdims=True)
        o = jnp.dot(p.astype(BF16), v, preferred_element_type=F32)
        o_ref[0, :, sl] = o / l
        lse_ref[0, :, sl] = jnp.broadcast_to(m + jnp.log(l), (BLOCK, HEAD_DIM))


def dilated_branch(q123, kv, table, group, dilation):
    b, s, _ = q123.shape
    ln = s // dilation
    nb = ln // BLOCK
    qv = q123.reshape(b, ln, dilation * 3 * ATTN_W)
    kvv = kv.reshape(b, ln, dilation * 2 * ATTN_W)
    blk = (1, BLOCK, ATTN_W)
    prev = lambda n: jnp.maximum(n - 1, 0)
    out = jax.ShapeDtypeStruct((b, ln, dilation * ATTN_W), F32)
    o, lse = pl.pallas_call(
        _dil_kernel,
        grid=(b, dilation, nb),
        in_specs=[pl.BlockSpec(blk, lambda bi, r, n: (bi, n, 3 * r + group)),
                  pl.BlockSpec(blk, lambda bi, r, n: (bi, prev(n), 2 * r)),
                  pl.BlockSpec(blk, lambda bi, r, n: (bi, n, 2 * r)),
                  pl.BlockSpec(blk, lambda bi, r, n: (bi, prev(n), 2 * r + 1)),
                  pl.BlockSpec(blk, lambda bi, r, n: (bi, n, 2 * r + 1)),
                  pl.BlockSpec(table.shape, lambda bi, r, n: (0, 0, 0, 0))],
        out_specs=[pl.BlockSpec(blk, lambda bi, r, n: (bi, n, r)),
                   pl.BlockSpec(blk, lambda bi, r, n: (bi, n, r))],
        out_shape=[out, out],
        compiler_params=_params("parallel", "parallel", "arbitrary"),
        name=f"dilated_attn_d{dilation}",
    )(qv, kvv, kvv, kvv, kvv, table)
    return o.reshape(b, s, ATTN_W), lse.reshape(b, s, ATTN_W)


def _combine_kernel(o1, o2, o3, l1, l2, l3, out_ref):
    a1, a2, a3 = l1[...], l2[...], l3[...]
    m = jnp.maximum(jnp.maximum(a1, a2), a3)
    w1, w2, w3 = jnp.exp(a1 - m), jnp.exp(a2 - m), jnp.exp(a3 - m)
    num = w1 * o1[...] + w2 * o2[...] + w3 * o3[...]
    out_ref[...] = (num / (w1 + w2 + w3)).astype(out_ref.dtype)


def combine_groups(outs, lses, tm):
    t, w = outs[0].shape
    spec = pl.BlockSpec((tm, w), lambda i: (i, 0))
    return pl.pallas_call(
        _combine_kernel,
        grid=(t // tm,),
        in_specs=[spec] * 6,
        out_specs=spec,
        out_shape=jax.ShapeDtypeStruct((t, w), BF16),
        compiler_params=_params("parallel"),
        name="combine_groups",
    )(*outs, *lses)


def _cross_kernel(q_ref, kv_ref, o_ref):
    for h in range(N_CROSS_HEADS):
        sl = slice(h * HEAD_DIM, (h + 1) * HEAD_DIM)
        vsl = slice(CROSS_WIDTH + h * HEAD_DIM, CROSS_WIDTH + (h + 1) * HEAD_DIM)
        s = lax.dot_general(q_ref[0, :, sl], kv_ref[0, :, sl], _NT,
                            preferred_element_type=F32) * (HEAD_DIM ** -0.5)
        m = jnp.max(s, axis=-1, keepdims=True)
        p = jnp.exp(s - m)
        l = jnp.sum(p, axis=-1, keepdims=True)
        o = jnp.dot(p.astype(BF16), kv_ref[0, :, vsl], preferred_element_type=F32)
        o_ref[0, :, sl] = (o / l).astype(o_ref.dtype)


def cross_attention(qsrc, qcol, kv, tm):
    b, s, _ = qsrc.shape
    return pl.pallas_call(
        _cross_kernel,
        grid=(b, s // tm),
        in_specs=[pl.BlockSpec((1, tm, CROSS_WIDTH), lambda bi, i: (bi, i, qcol)),
                  pl.BlockSpec((1, N_MEM, 2 * CROSS_WIDTH), lambda bi, i: (bi, 0, 0))],
        out_specs=pl.BlockSpec((1, tm, CROSS_WIDTH), lambda bi, i: (bi, i, 0)),
        out_shape=jax.ShapeDtypeStruct((b, s, CROSS_WIDTH), BF16),
        compiler_params=_params("parallel", "parallel"),
        name="cross_attn",
    )(qsrc, kv)


def _outproj_kernel(mix_ref, cr_ref, wa_ref, wb_ref, x_ref, o_ref):
    acc = jnp.dot(mix_ref[...], wa_ref[...], preferred_element_type=F32)
    acc = acc + jnp.dot(cr_ref[...], wb_ref[...], preferred_element_type=F32)
    o_ref[...] = x_ref[...] + acc


def out_projection(mix, cross, wa, wb, x, tm):
    t, d = x.shape
    return pl.pallas_call(
        _outproj_kernel,
        grid=(t // tm,),
        in_specs=[pl.BlockSpec((tm, ATTN_W), lambda i: (i, 0)),
                  pl.BlockSpec((tm, CROSS_WIDTH), lambda i: (i, 0)),
                  pl.BlockSpec(wa.shape, lambda i: (0, 0)),
                  pl.BlockSpec(wb.shape, lambda i: (0, 0)),
                  pl.BlockSpec((tm, d), lambda i: (i, 0))],
        out_specs=pl.BlockSpec((tm, d), lambda i: (i, 0)),
        out_shape=jax.ShapeDtypeStruct((t, d), F32),
        compiler_params=_params("parallel"),
        name="out_projection",
    )(mix, cross, wa, wb, x)


def _swiglu_kernel(x_ref, g_ref, wg_ref, wu_ref, wd_ref, o_ref, hn_ref):
    @pl.when(pl.program_id(1) == 0)
    def _():
        x = x_ref[...]
        hn_ref[...] = _rms(x, g_ref[...]).astype(BF16)
        o_ref[...] = x

    hn = hn_ref[...]
    a = jnp.dot(hn, wg_ref[...], preferred_element_type=F32)
    u = jnp.dot(hn, wu_ref[...], preferred_element_type=F32)
    act = (a * jax.nn.sigmoid(a) * u).astype(BF16)
    o_ref[...] += jnp.dot(act, wd_ref[...], preferred_element_type=F32)


def swiglu_ffn(x, g, wg, wu, wd, tm, tf):
    t, d = x.shape
    f = wg.shape[1]
    return pl.pallas_call(
        _swiglu_kernel,
        grid=(t // tm, f // tf),
        in_specs=[pl.BlockSpec((tm, d), lambda i, j: (i, 0)),
                  pl.BlockSpec((1, d), lambda i, j: (0, 0)),
                  pl.BlockSpec((d, tf), lambda i, j: (0, j)),
                  pl.BlockSpec((d, tf), lambda i, j: (0, j)),
                  pl.BlockSpec((tf, d), lambda i, j: (j, 0))],
        out_specs=pl.BlockSpec((tm, d), lambda i, j: (i, 0)),
        out_shape=jax.ShapeDtypeStruct((t, d), F32),
        scratch_shapes=[pltpu.VMEM((tm, d), BF16)],
        compiler_params=_params("parallel", "arbitrary"),
        name="swiglu_ffn",
    )(x, g.reshape(1, d), wg, wu, wd)


def _router_kernel(x_ref, g_ref, wr_ref, gates_ref, route_ref, counts_ref, run_ref):
    @pl.when(pl.program_id(0) == 0)
    def _():
        run_ref[...] = jnp.zeros(run_ref.shape, F32)

    hn = _rms(x_ref[...], g_ref[...])
    logits = jnp.dot(hn, wr_ref[...], preferred_element_type=F32,
                     precision=lax.Precision.HIGHEST)
    tm = logits.shape[0]
    lane = lax.broadcasted_iota(jnp.int32, logits.shape, 1)
    lg = jnp.where(lane < N_EXPERTS, logits, -jnp.inf)
    m1 = jnp.max(lg, axis=-1, keepdims=True)
    i1 = jnp.min(jnp.where(lg == m1, lane, 128), axis=-1, keepdims=True)
    lg2 = jnp.where(lane == i1, -jnp.inf, lg)
    m2 = jnp.max(lg2, axis=-1, keepdims=True)
    i2 = jnp.min(jnp.where(lg2 == m2, lane, 128), axis=-1, keepdims=True)
    e = jnp.exp(m2 - m1)
    gates_ref[...] = jnp.where(lane == 0, 1.0 / (1.0 + e), jnp.where(lane == 1, e / (1.0 + e), 0.0))

    assign = jnp.where((lane == i1) | (lane == i2), 1.0, 0.0)
    r = lax.broadcasted_iota(jnp.int32, (tm, tm), 0)
    c = lax.broadcasted_iota(jnp.int32, (tm, tm), 1)
    lower = jnp.where(r > c, 1.0, 0.0).astype(BF16)
    before = jnp.dot(lower, assign.astype(BF16), preferred_element_type=F32) + run_ref[...]
    rank1 = jnp.sum(jnp.where(lane == i1, before, 0.0), axis=-1, keepdims=True).astype(jnp.int32)
    rank2 = jnp.sum(jnp.where(lane == i2, before, 0.0), axis=-1, keepdims=True).astype(jnp.int32)
    route_ref[...] = jnp.where(lane == 0, i1, jnp.where(lane == 1, i2, jnp.where(
        lane == 2, rank1, jnp.where(lane == 3, rank2, 0))))
    run = run_ref[...] + jnp.sum(assign, axis=0, keepdims=True)
    run_ref[...] = run
    counts_ref[...] = run.astype(jnp.int32)


def router(x, g, wr_pad, tm):
    t, d = x.shape
    return pl.pallas_call(
        _router_kernel,
        grid=(t // tm,),
        in_specs=[pl.BlockSpec((tm, d), lambda i: (i, 0)),
                  pl.BlockSpec((1, d), lambda i: (0, 0)),
                  pl.BlockSpec((d, 128), lambda i: (0, 0))],
        out_specs=[pl.BlockSpec((tm, 128), lambda i: (i, 0)),
                   pl.BlockSpec((tm, 128), lambda i: (i, 0)),
                   pl.BlockSpec((1, 128), lambda i: (0, 0))],
        out_shape=[jax.ShapeDtypeStruct((t, 128), F32),
                   jax.ShapeDtypeStruct((t, 128), jnp.int32),
                   jax.ShapeDtypeStruct((1, 128), jnp.int32)],
        scratch_shapes=[pltpu.VMEM((1, 128), F32)],
        compiler_params=_params("arbitrary"),
        name="router",
    )(x, g.reshape(1, d), wr_pad)


def _row_copy(src, src_row, dst, dst_row, sem):
    return pltpu.make_async_copy(src.at[pl.ds(src_row, 1)], dst.at[pl.ds(dst_row, 1)], sem)


def _scatter_rows_kernel(d1_ref, d2_ref, x_ref, xs_in_ref, xs_ref, sem):
    del xs_in_ref
    tm = x_ref.shape[0]
    base = pl.program_id(0) * tm

    def issue(r, carry):
        _row_copy(x_ref, r, xs_ref, d1_ref[base + r], sem).start()
        _row_copy(x_ref, r, xs_ref, d2_ref[base + r], sem).start()
        return carry

    lax.fori_loop(0, tm, issue, 0, unroll=8)

    def drain(r, carry):
        _row_copy(x_ref, r, xs_ref, d1_ref[base + r], sem).wait()
        _row_copy(x_ref, r, xs_ref, d2_ref[base + r], sem).wait()
        return carry

    lax.fori_loop(0, tm, drain, 0, unroll=8)


def scatter_rows(x, dest1, dest2, n_rows, tm):
    t, d = x.shape
    zeros = jnp.zeros((n_rows, d), x.dtype)
    return pl.pallas_call(
        _scatter_rows_kernel,
        grid_spec=pltpu.PrefetchScalarGridSpec(
            num_scalar_prefetch=2,
            grid=(t // tm,),
            in_specs=[pl.BlockSpec((tm, d), lambda i, d1, d2: (i, 0)),
                      pl.BlockSpec(memory_space=pl.ANY)],
            out_specs=pl.BlockSpec(memory_space=pl.ANY),
            scratch_shapes=[pltpu.SemaphoreType.DMA(())]),
        out_shape=jax.ShapeDtypeStruct((n_rows, d), x.dtype),
        input_output_aliases={3: 0},
        compiler_params=_params("arbitrary"),
        name="moe_scatter_rows",
    )(dest1, dest2, x, zeros)


def _moe_group_kernel(te_ref, nu_ref, x_ref, g_ref, wg_ref, wu_ref, wd_ref, o_ref, hn_ref):
    i = pl.program_id(0)
    j = pl.program_id(1)
    used = i < nu_ref[0]

    @pl.when(j == 0)
    def _():
        o_ref[...] = jnp.zeros(o_ref.shape, F32)

    @pl.when(used & (j == 0))
    def _():
        hn_ref[...] = _rms(x_ref[...], g_ref[...]).astype(BF16)

    @pl.when(used)
    def _():
        hn = hn_ref[...]
        a = jnp.dot(hn, wg_ref[...], preferred_element_type=F32)
        u = jnp.dot(hn, wu_ref[...], preferred_element_type=F32)
        act = (a * jax.nn.sigmoid(a) * u).astype(BF16)
        o_ref[...] += jnp.dot(act, wd_ref[...], preferred_element_type=F32)


def moe_group_ffn(xs, g, tile_expert, n_used, wg, wu, wd, tm, tf):
    p, d = xs.shape
    f = wg.shape[2]
    nj = f // tf
    chunk = lambda i, j, te, nu: jnp.where(i < nu[0], j, nj - 1)
    return pl.pallas_call(
        _moe_group_kernel,
        grid_spec=pltpu.PrefetchScalarGridSpec(
            num_scalar_prefetch=2,
            grid=(p // tm, nj),
            in_specs=[pl.BlockSpec((tm, d), lambda i, j, te, nu: (i, 0)),
                      pl.BlockSpec((1, d), lambda i, j, te, nu: (0, 0)),
                      pl.BlockSpec((None, d, tf), lambda i, j, te, nu: (te[i], 0, chunk(i, j, te, nu))),
                      pl.BlockSpec((None, d, tf), lambda i, j, te, nu: (te[i], 0, chunk(i, j, te, nu))),
                      pl.BlockSpec((None, tf, d), lambda i, j, te, nu: (te[i], chunk(i, j, te, nu), 0))],
            out_specs=pl.BlockSpec((tm, d), lambda i, j, te, nu: (i, 0)),
            scratch_shapes=[pltpu.VMEM((tm, d), BF16)]),
        out_shape=jax.ShapeDtypeStruct((p, d), F32),
        compiler_params=_params("arbitrary", "arbitrary"),
        name="moe_group_ffn",
    )(tile_expert, n_used, xs, g.reshape(1, d), wg, wu, wd)


def _moe_combine_kernel(d1_ref, d2_ref, x_ref, gates_ref, ys_ref, o_ref, y1_ref, y2_ref, sem):
    tm = x_ref.shape[0]
    base = pl.program_id(0) * tm

    def issue(r, carry):
        _row_copy(ys_ref, d1_ref[base + r], y1_ref, r, sem).start()
        _row_copy(ys_ref, d2_ref[base + r], y2_ref, r, sem).start()
        return carry

    lax.fori_loop(0, tm, issue, 0, unroll=8)

    def drain(r, carry):
        _row_copy(ys_ref, d1_ref[base + r], y1_ref, r, sem).wait()
        _row_copy(ys_ref, d2_ref[base + r], y2_ref, r, sem).wait()
        return carry

    lax.fori_loop(0, tm, drain, 0, unroll=8)
    gates = gates_ref[...]
    o_ref[...] = x_ref[...] + gates[:, 0:1] * y1_ref[...] + gates[:, 1:2] * y2_ref[...]


def moe_combine(x, gates, ys, dest1, dest2, tm):
    t, d = x.shape
    return pl.pallas_call(
        _moe_combine_kernel,
        grid_spec=pltpu.PrefetchScalarGridSpec(
            num_scalar_prefetch=2,
            grid=(t // tm,),
            in_specs=[pl.BlockSpec((tm, d), lambda i, d1, d2: (i, 0)),
                      pl.BlockSpec((tm, 128), lambda i, d1, d2: (i, 0)),
                      pl.BlockSpec(memory_space=pl.ANY)],
            out_specs=pl.BlockSpec((tm, d), lambda i, d1, d2: (i, 0)),
            scratch_shapes=[pltpu.VMEM((tm, d), F32), pltpu.VMEM((tm, d), F32),
                            pltpu.SemaphoreType.DMA(())]),
        out_shape=jax.ShapeDtypeStruct((t, d), F32),
        compiler_params=_params("arbitrary"),
        name="moe_combine",
    )(dest1, dest2, x, gates, ys)


def moe_ffn(x, g, wr_pad, wg, wu, wd):
    t, d = x.shape
    ne = wg.shape[0]
    gates, route, counts = router(x, g, wr_pad, ROUTE_TM)
    counts = counts[0, :ne]
    padded = (counts + MOE_TILE - 1) // MOE_TILE * MOE_TILE
    ends = jnp.cumsum(padded)
    starts = ends - padded
    expert_ids = jnp.arange(ne, dtype=jnp.int32)[None, :]
    start_of = lambda e: jnp.sum(jnp.where(e[:, None] == expert_ids, starts[None, :], 0), axis=1)
    dest1 = (start_of(route[:, 0]) + route[:, 2]).astype(jnp.int32)
    dest2 = (start_of(route[:, 1]) + route[:, 3]).astype(jnp.int32)
    n_tiles = (2 * t) // MOE_TILE + ne
    tile_start = jnp.arange(n_tiles, dtype=jnp.int32) * MOE_TILE
    tile_expert = jnp.minimum(jnp.sum(tile_start[:, None] >= ends[None, :], axis=1), ne - 1).astype(jnp.int32)
    n_used = (ends[-1:] // MOE_TILE).astype(jnp.int32)
    xs = scatter_rows(x, dest1, dest2, n_tiles * MOE_TILE, ROUTE_TM)
    ys = moe_group_ffn(xs, g, tile_expert, n_used, wg, wu, wd, MOE_TILE, 512)
    return moe_combine(x, gates, ys, dest1, dest2, ROUTE_TM)


def _kvnorm_kernel(c_ref, g_ref, o_ref, ot_ref):
    y = _rms(c_ref[0].astype(F32), g_ref[...])
    o_ref[0, 0] = y.astype(BF16)
    ot_ref[0, 0] = y.T.astype(BF16)


def kv_latent_norm(p1, ccol, g):
    b, s, _ = p1.shape
    nc = s // DSA_QB
    out = jax.ShapeDtypeStruct((b, nc, DSA_QB, KV_RANK), BF16)
    blk = pl.BlockSpec((1, 1, DSA_QB, KV_RANK), lambda bi, j: (bi, j, 0, 0))
    return pl.pallas_call(
        _kvnorm_kernel,
        grid=(b, nc),
        in_specs=[pl.BlockSpec((1, DSA_QB, KV_RANK), lambda bi, j: (bi, j, ccol)),
                  pl.BlockSpec((1, KV_RANK), lambda bi, j: (0, 0))],
        out_specs=[blk, blk],
        out_shape=[out, out],
        compiler_params=_params("parallel", "parallel"),
        name="kv_latent_norm",
    )(p1, g.reshape(1, KV_RANK))


def _select_kernel(qi_ref, kall_ref, kq_ref, mask_ref, keys_ref, jcut_ref, *, nchunks_total):
    qb = DSA_QB
    i = pl.program_id(1)
    nch = i + 1
    shape = (qb, qb)
    row = lax.broadcasted_iota(jnp.int32, shape, 0)
    qpos = i * qb + lax.broadcasted_iota(jnp.int32, shape, 1)

    wt = kq_ref[0].astype(F32).T
    wscale = (N_IDX_HEADS ** -0.5) * (IDX_DIM ** -0.5)

    def score_chunk(j, carry):
        start = pl.multiple_of(j * qb, qb)
        kc = kall_ref[0, pl.ds(start, qb), :][:, :IDX_DIM]
        acc = jnp.zeros(shape, F32)
        for h in range(N_IDX_HEADS):
            qh = qi_ref[0, :, h * IDX_DIM:(h + 1) * IDX_DIM]
            s = lax.dot_general(kc, qh, _NT, preferred_element_type=F32)
            acc = acc + jnp.maximum(s, 0.0) * (wt[IDX_DIM + h:IDX_DIM + h + 1, :] * wscale)
        acc = jnp.where(j * qb + row <= qpos, acc, NEG_INF)
        acc = jnp.where(acc == 0.0, 0.0, acc)
        bits = pltpu.bitcast(acc, jnp.int32)
        keys_ref[j] = bits ^ ((bits >> 31) & 0x7FFFFFFF)
        return carry

    lax.fori_loop(0, nch, score_chunk, 0)

    def count(pred):
        def body(j, c):
            hit = jnp.where(pred(keys_ref[j], j), 1, 0)
            return c + jnp.sum(hit.reshape(qb // 8, 8, qb), axis=0)
        c = lax.fori_loop(0, nch, body, jnp.zeros((8, qb), jnp.int32))
        return jnp.sum(c, axis=0, keepdims=True)

    def count_ge(cand):
        return count(lambda key, j: key >= cand)

    thr = jnp.where(count_ge(jnp.zeros((1, qb), jnp.int32)) >= TOPK, 0, INT_MIN)

    def bit_body(t, thr):
        cand = thr + jnp.left_shift(jnp.int32(1), 30 - t)
        return jnp.where(count_ge(cand) >= TOPK, cand, thr)

    thr = lax.fori_loop(0, 31, bit_body, thr)

    n_gt = count(lambda key, j: key > thr)
    n_ge = count_ge(thr)
    need = TOPK - n_gt

    jcut_ref[...] = jnp.full((1, qb), 1 << 30, jnp.int32)

    @pl.when(jnp.max(n_ge) > TOPK)
    def _():
        def idx_body(t, x):
            cand = x + jnp.left_shift(jnp.int32(1), 10 - t)
            below = count(lambda key, j: (key == thr) & (j * qb + row <= cand - 1))
            return jnp.where(below < need, cand, x)
        jcut_ref[...] = lax.fori_loop(0, 11, idx_body, jnp.zeros((1, qb), jnp.int32))

    jcut = jcut_ref[...]

    def write_chunk(j, carry):
        key = keys_ref[j]
        spos = j * qb + row
        sel = (key > thr) | ((key == thr) & (spos <= jcut))
        sel = sel & (spos <= qpos)
        start = pl.multiple_of(j * qb, qb)
        mask_ref[0, pl.ds(start, qb), :] = jnp.where(sel, 0.0, NEG_INF)
        return carry

    lax.fori_loop(0, nch, write_chunk, 0)

    def fill_chunk(j, carry):
        start = pl.multiple_of(j * qb, qb)
        mask_ref[0, pl.ds(start, qb), :] = jnp.full(shape, NEG_INF, F32)
        return carry

    lax.fori_loop(nch, nchunks_total, fill_chunk, 0)


def dsa_select(p1, qicol, kwcol):
    b, s, _ = p1.shape
    nq = s // DSA_QB
    return pl.pallas_call(
        functools.partial(_select_kernel, nchunks_total=nq),
        grid=(b, nq),
        in_specs=[pl.BlockSpec((1, DSA_QB, N_IDX_HEADS * IDX_DIM), lambda bi, i: (bi, i, qicol)),
                  pl.BlockSpec((1, s, 128), lambda bi, i: (bi, 0, kwcol)),
                  pl.BlockSpec((1, DSA_QB, 128), lambda bi, i: (bi, i, kwcol))],
        out_specs=pl.BlockSpec((1, s, DSA_QB), lambda bi, i: (bi, 0, i)),
        out_shape=jax.ShapeDtypeStruct((b, s, s), F32),
        scratch_shapes=[pltpu.VMEM((nq, DSA_QB, DSA_QB), jnp.int32),
                        pltpu.VMEM((1, DSA_QB), jnp.int32)],
        compiler_params=_params("parallel", "arbitrary"),
        name="dsa_select",
    )(p1, p1, p1)


def _dsa_attn_kernel(q_ref, c_ref, ct_ref, mask_ref, tz_ref, wuk_ref, wuvt_ref, o_ref,
                     ql_ref, ot_ref, acc_ref):
    qb = DSA_QB
    hg = DSA_HEAD_GROUP
    width = hg * qb
    i = pl.program_id(1)

    for h in range(N_HEADS):
        qh = q_ref[0, :, h * HEAD_DIM:(h + 1) * HEAD_DIM]
        ql = lax.dot_general(wuk_ref[h], qh, _NT, preferred_element_type=F32)
        ql_ref[h // hg, :, (h % hg) * qb:(h % hg + 1) * qb] = (ql * (HEAD_DIM ** -0.5)).astype(BF16)

    def group_body(g, carry):
        acc_ref[...] = jnp.zeros((KV_RANK, width), F32)

        def chunk(j, ml):
            m, l = ml
            s = jnp.dot(c_ref[0, j], ql_ref[g], preferred_element_type=F32)
            d = 2 * (i - j)
            start = pl.multiple_of(j * qb, qb)
            mk = mask_ref[0, pl.ds(start, qb), :]
            tiles = []
            for hh in range(hg):
                h = g * hg + hh
                t_diag = tz_ref[d, h]
                tiles.append(mk + jnp.concatenate(
                    [jnp.concatenate([t_diag, tz_ref[d + 1, h]], axis=1),
                     jnp.concatenate([tz_ref[jnp.maximum(d - 1, 0), h], t_diag], axis=1)], axis=0))
            s = s + jnp.concatenate(tiles, axis=1)
            m_new = jnp.maximum(m, jnp.max(s, axis=0, keepdims=True))
            p = jnp.exp(s - m_new)
            alpha = jnp.exp(m - m_new)
            l = alpha * l + jnp.sum(p, axis=0, keepdims=True)
            pv = jnp.dot(ct_ref[0, j], p.astype(BF16), preferred_element_type=F32)
            acc_ref[...] = alpha * acc_ref[...] + pv
            return m_new, l

        init = (jnp.full((1, width), M_INIT, F32), jnp.zeros((1, width), F32))
        _, l = lax.fori_loop(0, i + 1, chunk, init)
        o_lat = (acc_ref[...] / l).astype(BF16)
        for hh in range(hg):
            h = g * hg + hh
            ot_ref[h] = jnp.dot(wuvt_ref[h], o_lat[:, hh * qb:(hh + 1) * qb],
                                preferred_element_type=F32)
        return carry

    lax.fori_loop(0, N_HEADS // hg, group_body, 0)
    o_ref[0] = ot_ref[...].reshape(ATTN_W, qb).T.astype(o_ref.dtype)


def dsa_attention(p1, qcol, c, ct, mask, tz, wuk, wuvt):
    b, s, _ = p1.shape
    nq = s // DSA_QB
    const = lambda shape: pl.BlockSpec(shape, lambda bi, i: (0,) * len(shape))
    return pl.pallas_call(
        _dsa_attn_kernel,
        grid=(b, nq),
        in_specs=[pl.BlockSpec((1, DSA_QB, ATTN_W), lambda bi, i: (bi, i, qcol)),
                  pl.BlockSpec((1, nq, DSA_QB, KV_RANK), lambda bi, i: (bi, 0, 0, 0)),
                  pl.BlockSpec((1, nq, KV_RANK, DSA_QB), lambda bi, i: (bi, 0, 0, 0)),
                  pl.BlockSpec((1, s, DSA_QB), lambda bi, i: (bi, 0, i)),
                  pl.BlockSpec(tz.shape, lambda bi, i: (0, 0, 0, 0), pipeline_mode=pl.Buffered(1)),
                  const(wuk.shape), const(wuvt.shape)],
        out_specs=pl.BlockSpec((1, DSA_QB, ATTN_W), lambda bi, i: (bi, i, 0)),
        out_shape=jax.ShapeDtypeStruct((b, s, ATTN_W), BF16),
        scratch_shapes=[pltpu.VMEM((N_HEADS // DSA_HEAD_GROUP, KV_RANK, DSA_HEAD_GROUP * DSA_QB), BF16),
                        pltpu.VMEM((N_HEADS, HEAD_DIM, DSA_QB), F32),
                        pltpu.VMEM((KV_RANK, DSA_HEAD_GROUP * DSA_QB), F32)],
        compiler_params=_params("parallel", "arbitrary"),
        name="dsa_attention",
    )(p1, c, ct, mask, tz, wuk, wuvt)


def _rmsnorm_kernel(x_ref, g_ref, o_ref):
    o_ref[...] = _rms(x_ref[...], g_ref[...])


def rmsnorm(x, g, tm):
    t, d = x.shape
    return pl.pallas_call(
        _rmsnorm_kernel,
        grid=(t // tm,),
        in_specs=[pl.BlockSpec((tm, d), lambda i: (i, 0)), pl.BlockSpec((1, d), lambda i: (0, 0))],
        out_specs=pl.BlockSpec((tm, d), lambda i: (i, 0)),
        out_shape=jax.ShapeDtypeStruct((t, d), F32),
        compiler_params=_params("parallel"),
        name="final_rmsnorm",
    )(x, g.reshape(1, d))


def _rel_bucket(dist):
    n = jnp.maximum(dist, 0)
    max_exact = REL_BUCKETS // 2
    nf = jnp.maximum(n, 1).astype(F32)
    large = max_exact + (jnp.log(nf / max_exact) / math.log(REL_MAX_DIST / max_exact)
                         * (REL_BUCKETS - max_exact)).astype(jnp.int32)
    large = jnp.minimum(large, REL_BUCKETS - 1)
    return jnp.where(n < max_exact, n, large)


def _bias_table_kernel(bkt_ref, rb_ref, o_ref):
    bkt = bkt_ref[0]
    for h in range(N_HEADS):
        acc = jnp.full(bkt.shape, NEG_INF, F32)
        for k in range(REL_BUCKETS):
            acc = jnp.where(bkt == k, rb_ref[k, h], acc)
        o_ref[0, h] = acc


def bias_table(bkt, rel_bias):
    n, r, c = bkt.shape
    return pl.pallas_call(
        _bias_table_kernel,
        grid=(n,),
        in_specs=[pl.BlockSpec((1, r, c), lambda i: (i, 0, 0)),
                  pl.BlockSpec(memory_space=pltpu.SMEM)],
        out_specs=pl.BlockSpec((1, N_HEADS, r, c), lambda i: (i, 0, 0, 0)),
        out_shape=jax.ShapeDtypeStruct((n, N_HEADS, r, c), F32),
        compiler_params=_params("parallel"),
        name="bias_table",
    )(bkt, rel_bias.astype(F32))


def _dilated_buckets(window, dilation):
    qi = jnp.arange(BLOCK)[:, None]
    ki = jnp.arange(2 * BLOCK)[None, :]
    rel = qi + BLOCK - ki
    bkt = _rel_bucket(rel * dilation)
    band = (rel >= 0) & (rel <= window // dilation)
    return jnp.stack([jnp.where(band, bkt, -1),
                      jnp.where(band & (ki >= BLOCK), bkt, -1)]).astype(jnp.int32)


def _dsa_buckets(s):
    nd = s // BLOCK
    key = jnp.arange(BLOCK)[:, None]
    qry = jnp.arange(BLOCK)[None, :]
    dist = jnp.arange(nd)[:, None, None] * BLOCK + (qry - key)[None]
    return _rel_bucket(dist).astype(jnp.int32)


def kernel(x, mem, rel_bias, mem_norm, final_norm, mixer_norm, ffn_norm, w_mem_kv, w_out,
           even_w_in, even_w_gate, even_w_up, even_w_down,
           odd_w_in, odd_kv_norm, odd_w_uk, odd_w_uv,
           odd_w_router, odd_w_gate, odd_w_up, odd_w_down):
    b, s, d = x.shape
    t = b * s
    depth = mixer_norm.shape[0]
    xt = x.reshape(t, d)
    mem2 = mem.reshape(b * N_MEM, d)
    dil_tables = [bias_table(_dilated_buckets(w, dl), rel_bias) for w, dl in DIL_PAIRS]
    tz = bias_table(_dsa_buckets(s), rel_bias)

    for i in range(depth):
        j = i // 2
        kvm = norm_matmul(mem2, mem_norm, w_mem_kv[i].astype(BF16), BF16, 512, 512)
        kvm = kvm.reshape(b, N_MEM, 2 * CROSS_WIDTH)
        wo = w_out[i].astype(BF16)
        if i % 2 == 0:
            w_in = even_w_in[j].astype(BF16)
            q123 = norm_matmul(xt, mixer_norm[i], w_in[:, :3 * ATTN_W], BF16, 512, 1024)
            kv = norm_matmul(xt, mixer_norm[i], w_in[:, 3 * ATTN_W:5 * ATTN_W], BF16, 512, 1024)
            qc = norm_matmul(xt, mixer_norm[i], w_in[:, 5 * ATTN_W:], BF16, 512, CROSS_WIDTH)
            q123 = q123.reshape(b, s, 3 * ATTN_W)
            kv = kv.reshape(b, s, 2 * ATTN_W)
            outs, lses = [], []
            for grp, (_, dl) in enumerate(DIL_PAIRS):
                o, lse = dilated_branch(q123, kv, dil_tables[grp], grp, dl)
                outs.append(o.reshape(t, ATTN_W))
                lses.append(lse.reshape(t, ATTN_W))
            mix = combine_groups(outs, lses, 512)
            cross = cross_attention(qc.reshape(b, s, CROSS_WIDTH), 0, kvm, 512)
        else:
            o_q, o_c, o_qi, o_ki, o_wi, o_qc = 0, 1024, 1280, 1792, 1856, 1864
            w = odd_w_in[j]
            pad = jnp.zeros((d, 128 - IDX_DIM - N_IDX_HEADS), w.dtype)
            w_in = jnp.concatenate(
                [w[:, o_q:o_c], w[:, o_qi:o_ki], w[:, o_c:o_qi], w[:, o_qc:],
                 w[:, o_ki:o_wi], w[:, o_wi:o_qc], pad], axis=1).astype(BF16)
            p1 = norm_matmul(xt, mixer_norm[i], w_in, BF16, 512, w_in.shape[1])
            p1 = p1.reshape(b, s, w_in.shape[1])
            c, ct = kv_latent_norm(p1, 6, odd_kv_norm[j])
            mask = dsa_select(p1, 2, 16)
            wuk = jnp.transpose(odd_w_uk[j], (1, 0, 2)).astype(BF16)
            wuvt = jnp.transpose(odd_w_uv[j], (1, 2, 0)).astype(BF16)
            mix = dsa_attention(p1, 0, c, ct, mask, tz, wuk, wuvt).reshape(t, ATTN_W)
            cross = cross_attention(p1, 7, kvm, 512)
        xt = out_projection(mix, cross.reshape(t, CROSS_WIDTH), wo[:ATTN_W], wo[ATTN_W:], xt, 512)
        if i % 2 == 0:
            xt = swiglu_ffn(xt, ffn_norm[i], even_w_gate[j].astype(BF16), even_w_up[j].astype(BF16),
                            even_w_down[j].astype(BF16), 512, 1408)
        else:
            wr = jnp.pad(odd_w_router[j], ((0, 0), (0, 128 - N_EXPERTS)))
            xt = moe_ffn(xt, ffn_norm[i], wr, odd_w_gate[j].astype(BF16), odd_w_up[j].astype(BF16),
                         odd_w_down[j].astype(BF16))
    return rmsnorm(xt, final_norm, 512).reshape(b, s, d)
```

```python
import functools
import math

import jax
import jax.numpy as jnp
from jax import lax
from jax.experimental import pallas as pl
from jax.experimental.pallas import tpu as pltpu

D_MODEL = 1024
N_HEADS = 16
HEAD_DIM = 64
ATTN_W = N_HEADS * HEAD_DIM
DIL_PAIRS = ((128, 1), (512, 4), (2048, 16))
BLOCK = 128
LANES = 128
KV_RANK = 256
N_IDX_HEADS = 8
IDX_DIM = 64
TOPK = 256
N_MEM = 256
N_CROSS_HEADS = 4
CROSS_WIDTH = 256
REL_BUCKETS = 32
REL_MAX_DIST = 2048
N_EXPERTS = 8
RMS_EPS = 1e-6
NEG_INF = -1e30
M_INIT = -5e29
LOG2_E = math.log2(math.e)

F32 = jnp.float32
BF16 = jnp.bfloat16
V7X_VMEM_LIMIT = 56 * 1024 * 1024
DSA_QB = 256
DSA_HEAD_GROUP = 8
INT_MIN = -(2 ** 31)
MOE_TILE = 512
ROUTE_TM = 256

_NT = (((1,), (1,)), ((), ()))


def _params(*sem):
    return pltpu.CompilerParams(dimension_semantics=sem, vmem_limit_bytes=V7X_VMEM_LIMIT)


def _rms(x, g):
    return x * lax.rsqrt(jnp.mean(x * x, axis=-1, keepdims=True) + RMS_EPS) * g


def _norm_mm_kernel(x_ref, g_ref, w_ref, o_ref, xn_ref):
    @pl.when(pl.program_id(1) == 0)
    def _():
        xn_ref[...] = _rms(x_ref[...].astype(F32), g_ref[...]).astype(BF16)

    o_ref[...] = jnp.dot(xn_ref[...], w_ref[...], preferred_element_type=F32).astype(o_ref.dtype)


def norm_matmul(x, g, w, out_dtype, tm, tn):
    t, k = x.shape
    n = w.shape[1]
    return pl.pallas_call(
        _norm_mm_kernel,
        grid=(t // tm, n // tn),
        in_specs=[pl.BlockSpec((tm, k), lambda i, j: (i, 0)),
                  pl.BlockSpec((1, k), lambda i, j: (0, 0)),
                  pl.BlockSpec((k, tn), lambda i, j: (0, j))],
        out_specs=pl.BlockSpec((tm, tn), lambda i, j: (i, j)),
        out_shape=jax.ShapeDtypeStruct((t, n), out_dtype),
        scratch_shapes=[pltpu.VMEM((tm, k), BF16)],
        compiler_params=_params("parallel", "arbitrary"),
        name="norm_matmul",
    )(x, g.reshape(1, k), w)


def _even_proj_kernel(x_ref, g_ref, w_ref, q1_ref, q4_ref, q16_ref, k1_ref, k4_ref, k16_ref,
                      v1_ref, v4_ref, v16_ref, xn_ref, y_ref):
    j = pl.program_id(1)
    tm = x_ref.shape[0]

    @pl.when(j == 0)
    def _():
        xn_ref[...] = _rms(x_ref[...], g_ref[...]).astype(BF16)

    y = jnp.dot(xn_ref[...], w_ref[...], preferred_element_type=F32)
    n_lane_blocks = y.shape[1] // LANES
    for c in range(n_lane_blocks):
        y_ref[c] = y[:, c * LANES:(c + 1) * LANES]

    def put(dst_ref, d):
        if d == 1:
            dst_ref[0, 0] = y.astype(BF16)
            return
        for r in range(d):
            for c in range(n_lane_blocks):
                rows = y_ref[c, pl.ds(r, tm // d, stride=d), :]
                dst_ref[0, r, :, c * LANES:(c + 1) * LANES] = rows.astype(BF16)

    column_dsts = (((q1_ref, 1),), ((q4_ref, 4),), ((q16_ref, 16),),
                   ((k1_ref, 1), (k4_ref, 4), (k16_ref, 16)),
                   ((v1_ref, 1), (v4_ref, 4), (v16_ref, 16)))
    for col, dsts in enumerate(column_dsts):
        @pl.when(j == col)
        def _(dsts=dsts):
            for ref, d in dsts:
                put(ref, d)


def even_in_projection(x, g, w, b, s, tm):
    t, kdim = x.shape
    per_b = s // tm
    dils = [dl for _, dl in DIL_PAIRS]
    layouts = dils + dils + dils
    spec = lambda dl: pl.BlockSpec((1, dl, tm // dl, ATTN_W), lambda i, j: (i // per_b, 0, i % per_b, 0))
    return pl.pallas_call(
        _even_proj_kernel,
        grid=(t // tm, 5),
        in_specs=[pl.BlockSpec((tm, kdim), lambda i, j: (i, 0)),
                  pl.BlockSpec((1, kdim), lambda i, j: (0, 0)),
                  pl.BlockSpec((kdim, ATTN_W), lambda i, j: (0, j))],
        out_specs=[spec(dl) for dl in layouts],
        out_shape=[jax.ShapeDtypeStruct((b, dl, s // dl, ATTN_W), BF16) for dl in layouts],
        scratch_shapes=[pltpu.VMEM((tm, kdim), BF16), pltpu.VMEM((ATTN_W // LANES, tm, LANES), F32)],
        compiler_params=_params("parallel", "arbitrary"),
        name="even_in_projection",
    )(x, g.reshape(1, kdim), w)


def _dil_kernel(q_ref, kp_ref, kc_ref, vp_ref, vc_ref, tab_ref, o_ref, lse_ref, s_ref, p_ref, m_ref):
    first = (pl.program_id(2) == 0).astype(jnp.int32)
    pair = 2 * HEAD_DIM
    lo_q = lax.broadcasted_iota(jnp.int32, (BLOCK, pair), 1) < HEAD_DIM
    lo_k = lax.broadcasted_iota(jnp.int32, (2 * BLOCK, pair), 1) < HEAD_DIM
    ones_bd = jnp.concatenate([jnp.where(lo_k, 1.0, 0.0), jnp.where(lo_k, 0.0, 1.0)], axis=0).astype(BF16)
    scale = jnp.asarray(HEAD_DIM ** -0.5, BF16)
    zero = jnp.zeros((), BF16)
    n_pairs = N_HEADS // 2
    for p in range(n_pairs):
        cols = slice(p * pair, (p + 1) * pair)
        q = q_ref[0, 0, :, cols] * scale
        k = jnp.concatenate([kp_ref[0, 0, :, cols], kc_ref[0, 0, :, cols]], axis=0)
        q_ab = jnp.concatenate([jnp.where(lo_q, q, zero), jnp.where(lo_q, zero, q)], axis=0)
        s_ab = lax.dot_general(q_ab, k, _NT, preferred_element_type=F32)
        s_ref[2 * p] = s_ab[:BLOCK] + tab_ref[first, 2 * p]
        s_ref[2 * p + 1] = s_ab[BLOCK:] + tab_ref[first, 2 * p + 1]
    for p in range(n_pairs):
        sa = s_ref[2 * p]
        sb = s_ref[2 * p + 1]
        ma = jnp.max(sa, axis=-1, keepdims=True)
        mb = jnp.max(sb, axis=-1, keepdims=True)
        p_ref[p, :, :2 * BLOCK] = jnp.exp(sa - ma).astype(BF16)
        p_ref[p, :, 2 * BLOCK:] = jnp.exp(sb - mb).astype(BF16)
        m_ref[p] = jnp.where(lo_q, ma, mb)
    for p in range(n_pairs):
        cols = slice(p * pair, (p + 1) * pair)
        v = jnp.concatenate([vp_ref[0, 0, :, cols], vc_ref[0, 0, :, cols]], axis=0)
        v_bd = jnp.concatenate([jnp.where(lo_k, v, zero), jnp.where(lo_k, zero, v)], axis=0)
        ol = jnp.dot(p_ref[p], jnp.concatenate([v_bd, ones_bd], axis=1), preferred_element_type=F32)
        o, l = ol[:, :pair], ol[:, pair:]
        o_ref[0, 0, :, cols] = o / l
        lse_ref[0, 0, :, cols] = m_ref[p] + jnp.log(l)


def dilated_branch(q, k, v, table):
    b, dilation, ln, _ = q.shape
    nb = ln // BLOCK
    blk = (1, 1, BLOCK, ATTN_W)
    cur = pl.BlockSpec(blk, lambda bi, r, n: (bi, r, n, 0))
    prev = pl.BlockSpec(blk, lambda bi, r, n: (bi, r, jnp.maximum(n - 1, 0), 0))
    out = jax.ShapeDtypeStruct(q.shape, F32)
    return pl.pallas_call(
        _dil_kernel,
        grid=(b, dilation, nb),
        in_specs=[cur, prev, cur, prev, cur,
                  pl.BlockSpec(table.shape, lambda bi, r, n: (0, 0, 0, 0))],
        out_specs=[cur, cur],
        out_shape=[out, out],
        scratch_shapes=[pltpu.VMEM((N_HEADS, BLOCK, 2 * BLOCK), F32),
                        pltpu.VMEM((N_HEADS // 2, BLOCK, 4 * BLOCK), BF16),
                        pltpu.VMEM((N_HEADS // 2, BLOCK, 2 * HEAD_DIM), F32)],
        compiler_params=_params("parallel", "parallel", "arbitrary"),
        name=f"dilated_attn_d{dilation}",
    )(q, k, k, v, v, table)


def _combine_kernel(o1, o4, o16, l1, l4, l16, out_ref, so4, sl4, so16, sl16):
    tm = out_ref.shape[1]
    n_lane_blocks = out_ref.shape[2] // LANES

    def to_token_order(src_ref, dst_ref, d):
        for r in range(d):
            for c in range(n_lane_blocks):
                dst_ref[c, pl.ds(r, tm // d, stride=d), :] = src_ref[0, r, :, c * LANES:(c + 1) * LANES]

    to_token_order(o4, so4, 4)
    to_token_order(l4, sl4, 4)
    to_token_order(o16, so16, 16)
    to_token_order(l16, sl16, 16)
    for c in range(n_lane_blocks):
        cols = slice(c * LANES, (c + 1) * LANES)
        a1, a2, a3 = l1[0, 0, :, cols], sl4[c], sl16[c]
        m = jnp.maximum(jnp.maximum(a1, a2), a3)
        w1, w2, w3 = jnp.exp(a1 - m), jnp.exp(a2 - m), jnp.exp(a3 - m)
        num = w1 * o1[0, 0, :, cols] + w2 * so4[c] + w3 * so16[c]
        out_ref[0, :, cols] = (num / (w1 + w2 + w3)).astype(out_ref.dtype)


def combine_groups(outs, lses, tm):
    b, _, s, w = outs[0].shape
    spec = lambda dl: pl.BlockSpec((1, dl, tm // dl, w), lambda bi, i: (bi, 0, i, 0))
    specs = [spec(o.shape[1]) for o in outs]
    return pl.pallas_call(
        _combine_kernel,
        grid=(b, s // tm),
        in_specs=specs + specs,
        out_specs=pl.BlockSpec((1, tm, w), lambda bi, i: (bi, i, 0)),
        out_shape=jax.ShapeDtypeStruct((b, s, w), BF16),
        scratch_shapes=[pltpu.VMEM((w // LANES, tm, LANES), F32)] * 4,
        compiler_params=_params("parallel", "parallel"),
        name="combine_groups",
    )(*outs, *lses)


def _cross_kernel(q_ref, kv_ref, o_ref):
    for h in range(N_CROSS_HEADS):
        sl = slice(h * HEAD_DIM, (h + 1) * HEAD_DIM)
        vsl = slice(CROSS_WIDTH + h * HEAD_DIM, CROSS_WIDTH + (h + 1) * HEAD_DIM)
        s = lax.dot_general(q_ref[0, :, sl], kv_ref[0, :, sl], _NT,
                            preferred_element_type=F32) * (HEAD_DIM ** -0.5)
        m = jnp.max(s, axis=-1, keepdims=True)
        p = jnp.exp(s - m)
        l = jnp.sum(p, axis=-1, keepdims=True)
        o = jnp.dot(p.astype(BF16), kv_ref[0, :, vsl], preferred_element_type=F32)
        o_ref[0, :, sl] = (o / l).astype(o_ref.dtype)


def cross_attention(qsrc, qcol, kv, tm):
    b, s, _ = qsrc.shape
    return pl.pallas_call(
        _cross_kernel,
        grid=(b, s // tm),
        in_specs=[pl.BlockSpec((1, tm, CROSS_WIDTH), lambda bi, i: (bi, i, qcol)),
                  pl.BlockSpec((1, N_MEM, 2 * CROSS_WIDTH), lambda bi, i: (bi, 0, 0))],
        out_specs=pl.BlockSpec((1, tm, CROSS_WIDTH), lambda bi, i: (bi, i, 0)),
        out_shape=jax.ShapeDtypeStruct((b, s, CROSS_WIDTH), BF16),
        compiler_params=_params("parallel", "parallel"),
        name="cross_attn",
    )(qsrc, kv)


def _outproj_kernel(mix_ref, cr_ref, wa_ref, wb_ref, x_ref, o_ref):
    acc = jnp.dot(mix_ref[...], wa_ref[...], preferred_element_type=F32)
    acc = acc + jnp.dot(cr_ref[...], wb_ref[...], preferred_element_type=F32)
    o_ref[...] = x_ref[...] + acc


def out_projection(mix, cross, wa, wb, x, tm):
    t, d = x.shape
    return pl.pallas_call(
        _outproj_kernel,
        grid=(t // tm,),
        in_specs=[pl.BlockSpec((tm, ATTN_W), lambda i: (i, 0)),
                  pl.BlockSpec((tm, CROSS_WIDTH), lambda i: (i, 0)),
                  pl.BlockSpec(wa.shape, lambda i: (0, 0)),
                  pl.BlockSpec(wb.shape, lambda i: (0, 0)),
                  pl.BlockSpec((tm, d), lambda i: (i, 0))],
        out_specs=pl.BlockSpec((tm, d), lambda i: (i, 0)),
        out_shape=jax.ShapeDtypeStruct((t, d), F32),
        compiler_params=_params("parallel"),
        name="out_projection",
    )(mix, cross, wa, wb, x)


def _swiglu_kernel(x_ref, g_ref, wg_ref, wu_ref, wd_ref, o_ref, hn_ref):
    @pl.when(pl.program_id(1) == 0)
    def _():
        x = x_ref[...]
        hn_ref[...] = _rms(x, g_ref[...]).astype(BF16)
        o_ref[...] = x

    hn = hn_ref[...]
    a = jnp.dot(hn, wg_ref[...], preferred_element_type=F32)
    u = jnp.dot(hn, wu_ref[...], preferred_element_type=F32)
    act = (a * jax.nn.sigmoid(a) * u).astype(BF16)
    o_ref[...] += jnp.dot(act, wd_ref[...], preferred_element_type=F32)


def swiglu_ffn(x, g, wg, wu, wd, tm, tf):
    t, d = x.shape
    f = wg.shape[1]
    return pl.pallas_call(
        _swiglu_kernel,
        grid=(t // tm, f // tf),
        in_specs=[pl.BlockSpec((tm, d), lambda i, j: (i, 0)),
                  pl.BlockSpec((1, d), lambda i, j: (0, 0)),
                  pl.BlockSpec((d, tf), lambda i, j: (0, j)),
                  pl.BlockSpec((d, tf), lambda i, j: (0, j)),
                  pl.BlockSpec((tf, d), lambda i, j: (j, 0))],
        out_specs=pl.BlockSpec((tm, d), lambda i, j: (i, 0)),
        out_shape=jax.ShapeDtypeStruct((t, d), F32),
        scratch_shapes=[pltpu.VMEM((tm, d), BF16)],
        compiler_params=_params("parallel", "arbitrary"),
        name="swiglu_ffn",
    )(x, g.reshape(1, d), wg, wu, wd)


def _router_kernel(x_ref, g_ref, wr_ref, gates_ref, route_ref, counts_ref, run_ref):
    @pl.when(pl.program_id(0) == 0)
    def _():
        run_ref[...] = jnp.zeros(run_ref.shape, F32)

    hn = _rms(x_ref[...], g_ref[...])
    logits = jnp.dot(hn, wr_ref[...], preferred_element_type=F32,
                     precision=lax.Precision.HIGHEST)
    tm = logits.shape[0]
    lane = lax.broadcasted_iota(jnp.int32, logits.shape, 1)
    lg = jnp.where(lane < N_EXPERTS, logits, -jnp.inf)
    m1 = jnp.max(lg, axis=-1, keepdims=True)
    i1 = jnp.min(jnp.where(lg == m1, lane, 128), axis=-1, keepdims=True)
    lg2 = jnp.where(lane == i1, -jnp.inf, lg)
    m2 = jnp.max(lg2, axis=-1, keepdims=True)
    i2 = jnp.min(jnp.where(lg2 == m2, lane, 128), axis=-1, keepdims=True)
    e = jnp.exp(m2 - m1)
    gates_ref[...] = jnp.where(lane == 0, 1.0 / (1.0 + e), jnp.where(lane == 1, e / (1.0 + e), 0.0))

    assign = jnp.where((lane == i1) | (lane == i2), 1.0, 0.0)
    r = lax.broadcasted_iota(jnp.int32, (tm, tm), 0)
    c = lax.broadcasted_iota(jnp.int32, (tm, tm), 1)
    lower = jnp.where(r > c, 1.0, 0.0).astype(BF16)
    before = jnp.dot(lower, assign.astype(BF16), preferred_element_type=F32) + run_ref[...]
    rank1 = jnp.sum(jnp.where(lane == i1, before, 0.0), axis=-1, keepdims=True).astype(jnp.int32)
    rank2 = jnp.sum(jnp.where(lane == i2, before, 0.0), axis=-1, keepdims=True).astype(jnp.int32)
    route_ref[...] = jnp.where(lane == 0, i1, jnp.where(lane == 1, i2, jnp.where(
        lane == 2, rank1, jnp.where(lane == 3, rank2, 0))))
    run = run_ref[...] + jnp.sum(assign, axis=0, keepdims=True)
    run_ref[...] = run
    counts_ref[...] = run.astype(jnp.int32)


def router(x, g, wr_pad, tm):
    t, d = x.shape
    return pl.pallas_call(
        _router_kernel,
        grid=(t // tm,),
        in_specs=[pl.BlockSpec((tm, d), lambda i: (i, 0)),
                  pl.BlockSpec((1, d), lambda i: (0, 0)),
                  pl.BlockSpec((d, 128), lambda i: (0, 0))],
        out_specs=[pl.BlockSpec((tm, 128), lambda i: (i, 0)),
                   pl.BlockSpec((tm, 128), lambda i: (i, 0)),
                   pl.BlockSpec((1, 128), lambda i: (0, 0))],
        out_shape=[jax.ShapeDtypeStruct((t, 128), F32),
                   jax.ShapeDtypeStruct((t, 128), jnp.int32),
                   jax.ShapeDtypeStruct((1, 128), jnp.int32)],
        scratch_shapes=[pltpu.VMEM((1, 128), F32)],
        compiler_params=_params("arbitrary"),
        name="router",
    )(x, g.reshape(1, d), wr_pad)


def _row_copy(src, src_row, dst, dst_row, sem):
    return pltpu.make_async_copy(src.at[pl.ds(src_row, 1)], dst.at[pl.ds(dst_row, 1)], sem)


def _scatter_rows_kernel(d1_ref, d2_ref, x_ref, xs_in_ref, xs_ref, sem):
    del xs_in_ref
    tm = x_ref.shape[0]
    base = pl.program_id(0) * tm

    def issue(r, carry):
        _row_copy(x_ref, r, xs_ref, d1_ref[base + r], sem).start()
        _row_copy(x_ref, r, xs_ref, d2_ref[base + r], sem).start()
        return carry

    lax.fori_loop(0, tm, issue, 0, unroll=8)

    for _ in range(2):
        pltpu.make_async_copy(x_ref, xs_ref.at[pl.ds(0, tm)], sem).wait()


def scatter_rows(x, dest1, dest2, n_rows, tm):
    t, d = x.shape
    zeros = jnp.zeros((n_rows, d), x.dtype)
    return pl.pallas_call(
        _scatter_rows_kernel,
        grid_spec=pltpu.PrefetchScalarGridSpec(
            num_scalar_prefetch=2,
            grid=(t // tm,),
            in_specs=[pl.BlockSpec((tm, d), lambda i, d1, d2: (i, 0)),
                      pl.BlockSpec(memory_space=pl.ANY)],
            out_specs=pl.BlockSpec(memory_space=pl.ANY),
            scratch_shapes=[pltpu.SemaphoreType.DMA(())]),
        out_shape=jax.ShapeDtypeStruct((n_rows, d), x.dtype),
        input_output_aliases={3: 0},
        compiler_params=_params("arbitrary"),
        name="moe_scatter_rows",
    )(dest1, dest2, x, zeros)


def _moe_group_kernel(te_ref, nu_ref, x_ref, g_ref, wg_ref, wu_ref, wd_ref, o_ref, hn_ref):
    i = pl.program_id(0)
    j = pl.program_id(1)
    used = i < nu_ref[0]

    @pl.when(j == 0)
    def _():
        o_ref[...] = jnp.zeros(o_ref.shape, F32)

    @pl.when(used & (j == 0))
    def _():
        hn_ref[...] = _rms(x_ref[...], g_ref[...]).astype(BF16)

    @pl.when(used)
    def _():
        hn = hn_ref[...]
        a = jnp.dot(hn, wg_ref[...], preferred_element_type=F32)
        u = jnp.dot(hn, wu_ref[...], preferred_element_type=F32)
        act = (a * jax.nn.sigmoid(a) * u).astype(BF16)
        o_ref[...] += jnp.dot(act, wd_ref[...], preferred_element_type=F32)


def moe_group_ffn(xs, g, tile_expert, n_used, wg, wu, wd, tm, tf):
    p, d = xs.shape
    f = wg.shape[2]
    nj = f // tf
    chunk = lambda i, j, te, nu: jnp.where(i < nu[0], j, nj - 1)
    return pl.pallas_call(
        _moe_group_kernel,
        grid_spec=pltpu.PrefetchScalarGridSpec(
            num_scalar_prefetch=2,
            grid=(p // tm, nj),
            in_specs=[pl.BlockSpec((tm, d), lambda i, j, te, nu: (i, 0)),
                      pl.BlockSpec((1, d), lambda i, j, te, nu: (0, 0)),
                      pl.BlockSpec((None, d, tf), lambda i, j, te, nu: (te[i], 0, chunk(i, j, te, nu))),
                      pl.BlockSpec((None, d, tf), lambda i, j, te, nu: (te[i], 0, chunk(i, j, te, nu))),
                      pl.BlockSpec((None, tf, d), lambda i, j, te, nu: (te[i], chunk(i, j, te, nu), 0))],
            out_specs=pl.BlockSpec((tm, d), lambda i, j, te, nu: (i, 0)),
            scratch_shapes=[pltpu.VMEM((tm, d), BF16)]),
        out_shape=jax.ShapeDtypeStruct((p, d), F32),
        compiler_params=_params("arbitrary", "arbitrary"),
        name="moe_group_ffn",
    )(tile_expert, n_used, xs, g.reshape(1, d), wg, wu, wd)


def _moe_combine_kernel(d1_ref, d2_ref, x_ref, gates_ref, ys_ref, o_ref, y1_ref, y2_ref, sem):
    tm = x_ref.shape[0]
    base = pl.program_id(0) * tm

    def issue(r, carry):
        _row_copy(ys_ref, d1_ref[base + r], y1_ref, r, sem).start()
        _row_copy(ys_ref, d2_ref[base + r], y2_ref, r, sem).start()
        return carry

    lax.fori_loop(0, tm, issue, 0, unroll=8)

    pltpu.make_async_copy(ys_ref.at[pl.ds(0, tm)], y1_ref, sem).wait()
    pltpu.make_async_copy(ys_ref.at[pl.ds(0, tm)], y2_ref, sem).wait()
    gates = gates_ref[...]
    o_ref[...] = x_ref[...] + gates[:, 0:1] * y1_ref[...] + gates[:, 1:2] * y2_ref[...]


def moe_combine(x, gates, ys, dest1, dest2, tm):
    t, d = x.shape
    return pl.pallas_call(
        _moe_combine_kernel,
        grid_spec=pltpu.PrefetchScalarGridSpec(
            num_scalar_prefetch=2,
            grid=(t // tm,),
            in_specs=[pl.BlockSpec((tm, d), lambda i, d1, d2: (i, 0)),
                      pl.BlockSpec((tm, 128), lambda i, d1, d2: (i, 0)),
                      pl.BlockSpec(memory_space=pl.ANY)],
            out_specs=pl.BlockSpec((tm, d), lambda i, d1, d2: (i, 0)),
            scratch_shapes=[pltpu.VMEM((tm, d), F32), pltpu.VMEM((tm, d), F32),
                            pltpu.SemaphoreType.DMA(())]),
        out_shape=jax.ShapeDtypeStruct((t, d), F32),
        compiler_params=_params("arbitrary"),
        name="moe_combine",
    )(dest1, dest2, x, gates, ys)


def moe_ffn(x, g, wr_pad, wg, wu, wd):
    t, d = x.shape
    ne = wg.shape[0]
    gates, route, counts = router(x, g, wr_pad, ROUTE_TM)
    counts = counts[0, :ne]
    padded = (counts + MOE_TILE - 1) // MOE_TILE * MOE_TILE
    ends = jnp.cumsum(padded)
    starts = ends - padded
    expert_ids = jnp.arange(ne, dtype=jnp.int32)[None, :]
    start_of = lambda e: jnp.sum(jnp.where(e[:, None] == expert_ids, starts[None, :], 0), axis=1)
    dest1 = (start_of(route[:, 0]) + route[:, 2]).astype(jnp.int32)
    dest2 = (start_of(route[:, 1]) + route[:, 3]).astype(jnp.int32)
    n_tiles = (2 * t) // MOE_TILE + ne
    tile_start = jnp.arange(n_tiles, dtype=jnp.int32) * MOE_TILE
    tile_expert = jnp.minimum(jnp.sum(tile_start[:, None] >= ends[None, :], axis=1), ne - 1).astype(jnp.int32)
    n_used = (ends[-1:] // MOE_TILE).astype(jnp.int32)
    xs = scatter_rows(x, dest1, dest2, n_tiles * MOE_TILE, ROUTE_TM)
    ys = moe_group_ffn(xs, g, tile_expert, n_used, wg, wu, wd, MOE_TILE, 512)
    return moe_combine(x, gates, ys, dest1, dest2, ROUTE_TM)


def _kvnorm_kernel(c_ref, g_ref, o_ref, ot_ref):
    y = _rms(c_ref[0].astype(F32), g_ref[...])
    o_ref[0, 0] = y.astype(BF16)
    ot_ref[0, 0] = y.T.astype(BF16)


def kv_latent_norm(p1, ccol, g):
    b, s, _ = p1.shape
    nc = s // DSA_QB
    out = jax.ShapeDtypeStruct((b, nc, DSA_QB, KV_RANK), BF16)
    blk = pl.BlockSpec((1, 1, DSA_QB, KV_RANK), lambda bi, j: (bi, j, 0, 0))
    return pl.pallas_call(
        _kvnorm_kernel,
        grid=(b, nc),
        in_specs=[pl.BlockSpec((1, DSA_QB, KV_RANK), lambda bi, j: (bi, j, ccol)),
                  pl.BlockSpec((1, KV_RANK), lambda bi, j: (0, 0))],
        out_specs=[blk, blk],
        out_shape=[out, out],
        compiler_params=_params("parallel", "parallel"),
        name="kv_latent_norm",
    )(p1, g.reshape(1, KV_RANK))


def _select_kernel(qi_ref, kall_ref, kq_ref, mask_ref, keys_ref, jcut_ref, *, nchunks_total):
    qb = DSA_QB
    i = pl.program_id(1)
    nch = i + 1
    shape = (qb, qb)
    row = lax.broadcasted_iota(jnp.int32, shape, 0)
    qpos = i * qb + lax.broadcasted_iota(jnp.int32, shape, 1)

    wt = kq_ref[0].astype(F32).T
    wscale = (N_IDX_HEADS ** -0.5) * (IDX_DIM ** -0.5)

    def score_chunk(j, carry):
        start = pl.multiple_of(j * qb, qb)
        kc = kall_ref[0, pl.ds(start, qb), :][:, :IDX_DIM]
        acc = jnp.zeros(shape, F32)
        for h in range(N_IDX_HEADS):
            qh = qi_ref[0, :, h * IDX_DIM:(h + 1) * IDX_DIM]
            s = lax.dot_general(kc, qh, _NT, preferred_element_type=F32)
            acc = acc + jnp.maximum(s, 0.0) * (wt[IDX_DIM + h:IDX_DIM + h + 1, :] * wscale)
        acc = jnp.where(j * qb + row <= qpos, acc, NEG_INF)
        acc = jnp.where(acc == 0.0, 0.0, acc)
        bits = pltpu.bitcast(acc, jnp.int32)
        keys_ref[j] = bits ^ ((bits >> 31) & 0x7FFFFFFF)
        return carry

    lax.fori_loop(0, nch, score_chunk, 0)

    def count(pred):
        def body(j, c):
            hit = jnp.where(pred(keys_ref[j], j), 1, 0)
            return c + jnp.sum(hit.reshape(qb // 8, 8, qb), axis=0)
        c = lax.fori_loop(0, nch, body, jnp.zeros((8, qb), jnp.int32))
        return jnp.sum(c, axis=0, keepdims=True)

    def count_ge(cand):
        return count(lambda key, j: key >= cand)

    thr = jnp.where(count_ge(jnp.zeros((1, qb), jnp.int32)) >= TOPK, 0, INT_MIN)

    def bit_body(t, thr):
        cand = thr + jnp.left_shift(jnp.int32(1), 30 - t)
        return jnp.where(count_ge(cand) >= TOPK, cand, thr)

    thr = lax.fori_loop(0, 31, bit_body, thr)

    n_gt = count(lambda key, j: key > thr)
    n_ge = count_ge(thr)
    need = TOPK - n_gt

    jcut_ref[...] = jnp.full((1, qb), 1 << 30, jnp.int32)

    @pl.when(jnp.max(n_ge) > TOPK)
    def _():
        def idx_body(t, x):
            cand = x + jnp.left_shift(jnp.int32(1), 10 - t)
            below = count(lambda key, j: (key == thr) & (j * qb + row <= cand - 1))
            return jnp.where(below < need, cand, x)
        jcut_ref[...] = lax.fori_loop(0, 11, idx_body, jnp.zeros((1, qb), jnp.int32))

    jcut = jcut_ref[...]

    def write_chunk(j, carry):
        key = keys_ref[j]
        spos = j * qb + row
        sel = (key > thr) | ((key == thr) & (spos <= jcut))
        sel = sel & (spos <= qpos)
        start = pl.multiple_of(j * qb, qb)
        mask_ref[0, pl.ds(start, qb), :] = jnp.where(sel, 0.0, NEG_INF)
        return carry

    lax.fori_loop(0, nch, write_chunk, 0)

    def fill_chunk(j, carry):
        start = pl.multiple_of(j * qb, qb)
        mask_ref[0, pl.ds(start, qb), :] = jnp.full(shape, NEG_INF, F32)
        return carry

    lax.fori_loop(nch, nchunks_total, fill_chunk, 0)


def dsa_select(p1, qicol, kwcol):
    b, s, _ = p1.shape
    nq = s // DSA_QB
    return pl.pallas_call(
        functools.partial(_select_kernel, nchunks_total=nq),
        grid=(b, nq),
        in_specs=[pl.BlockSpec((1, DSA_QB, N_IDX_HEADS * IDX_DIM), lambda bi, i: (bi, i, qicol)),
                  pl.BlockSpec((1, s, 128), lambda bi, i: (bi, 0, kwcol)),
                  pl.BlockSpec((1, DSA_QB, 128), lambda bi, i: (bi, i, kwcol))],
        out_specs=pl.BlockSpec((1, s, DSA_QB), lambda bi, i: (bi, 0, i)),
        out_shape=jax.ShapeDtypeStruct((b, s, s), F32),
        scratch_shapes=[pltpu.VMEM((nq, DSA_QB, DSA_QB), jnp.int32),
                        pltpu.VMEM((1, DSA_QB), jnp.int32)],
        compiler_params=_params("parallel", "arbitrary"),
        name="dsa_select",
    )(p1, p1, p1)


def _dsa_attn_kernel(q_ref, c_ref, ct_ref, mask_ref, tz_ref, wuk_ref, wuvt_ref, o_ref,
                     ql_ref, ot_ref, acc_ref):
    qb = DSA_QB
    hg = DSA_HEAD_GROUP
    width = hg * qb
    i = pl.program_id(1)

    for h in range(N_HEADS):
        qh = q_ref[0, :, h * HEAD_DIM:(h + 1) * HEAD_DIM]
        ql = lax.dot_general(wuk_ref[h], qh, _NT, preferred_element_type=F32)
        ql_ref[h // hg, :, (h % hg) * qb:(h % hg + 1) * qb] = (
            ql * (HEAD_DIM ** -0.5 * LOG2_E)).astype(BF16)

    def group_body(g, carry):
        acc_ref[...] = jnp.zeros((KV_RANK, width), F32)

        def chunk(j, ml):
            m, l = ml
            s = jnp.dot(c_ref[0, j], ql_ref[g], preferred_element_type=F32)
            d = 2 * (i - j)
            start = pl.multiple_of(j * qb, qb)
            mk = mask_ref[0, pl.ds(start, qb), :]
            tiles = []
            for hh in range(hg):
                h = g * hg + hh
                t_diag = tz_ref[d, h]
                tiles.append(mk + jnp.concatenate(
                    [jnp.concatenate([t_diag, tz_ref[d + 1, h]], axis=1),
                     jnp.concatenate([tz_ref[jnp.maximum(d - 1, 0), h], t_diag], axis=1)], axis=0))
            s = s + jnp.concatenate(tiles, axis=1)
            m_new = jnp.maximum(m, jnp.max(s, axis=0, keepdims=True))
            p = jnp.exp2(s - m_new)
            alpha = jnp.exp2(m - m_new)
            l = alpha * l + jnp.sum(p, axis=0, keepdims=True)
            pv = jnp.dot(ct_ref[0, j], p.astype(BF16), preferred_element_type=F32)
            acc_ref[...] = alpha * acc_ref[...] + pv
            return m_new, l

        init = (jnp.full((1, width), M_INIT, F32), jnp.zeros((1, width), F32))
        _, l = lax.fori_loop(0, i + 1, chunk, init)
        o_lat = (acc_ref[...] / l).astype(BF16)
        for hh in range(hg):
            h = g * hg + hh
            ot_ref[h] = jnp.dot(wuvt_ref[h], o_lat[:, hh * qb:(hh + 1) * qb],
                                preferred_element_type=F32)
        return carry

    lax.fori_loop(0, N_HEADS // hg, group_body, 0)
    o_ref[0] = ot_ref[...].reshape(ATTN_W, qb).T.astype(o_ref.dtype)


def dsa_attention(p1, qcol, c, ct, mask, tz, wuk, wuvt):
    b, s, _ = p1.shape
    nq = s // DSA_QB
    const = lambda shape: pl.BlockSpec(shape, lambda bi, i: (0,) * len(shape))
    return pl.pallas_call(
        _dsa_attn_kernel,
        grid=(b, nq),
        in_specs=[pl.BlockSpec((1, DSA_QB, ATTN_W), lambda bi, i: (bi, i, qcol)),
                  pl.BlockSpec((1, nq, DSA_QB, KV_RANK), lambda bi, i: (bi, 0, 0, 0)),
                  pl.BlockSpec((1, nq, KV_RANK, DSA_QB), lambda bi, i: (bi, 0, 0, 0)),
                  pl.BlockSpec((1, s, DSA_QB), lambda bi, i: (bi, 0, i)),
                  pl.BlockSpec(tz.shape, lambda bi, i: (0, 0, 0, 0), pipeline_mode=pl.Buffered(1)),
                  const(wuk.shape), const(wuvt.shape)],
        out_specs=pl.BlockSpec((1, DSA_QB, ATTN_W), lambda bi, i: (bi, i, 0)),
        out_shape=jax.ShapeDtypeStruct((b, s, ATTN_W), BF16),
        scratch_shapes=[pltpu.VMEM((N_HEADS // DSA_HEAD_GROUP, KV_RANK, DSA_HEAD_GROUP * DSA_QB), BF16),
                        pltpu.VMEM((N_HEADS, HEAD_DIM, DSA_QB), F32),
                        pltpu.VMEM((KV_RANK, DSA_HEAD_GROUP * DSA_QB), F32)],
        compiler_params=_params("parallel", "arbitrary"),
        name="dsa_attention",
    )(p1, c, ct, mask, tz, wuk, wuvt)


def _rmsnorm_kernel(x_ref, g_ref, o_ref):
    o_ref[...] = _rms(x_ref[...], g_ref[...])


def rmsnorm(x, g, tm):
    t, d = x.shape
    return pl.pallas_call(
        _rmsnorm_kernel,
        grid=(t // tm,),
        in_specs=[pl.BlockSpec((tm, d), lambda i: (i, 0)), pl.BlockSpec((1, d), lambda i: (0, 0))],
        out_specs=pl.BlockSpec((tm, d), lambda i: (i, 0)),
        out_shape=jax.ShapeDtypeStruct((t, d), F32),
        compiler_params=_params("parallel"),
        name="final_rmsnorm",
    )(x, g.reshape(1, d))


def _rel_bucket(dist):
    n = jnp.maximum(dist, 0)
    max_exact = REL_BUCKETS // 2
    nf = jnp.maximum(n, 1).astype(F32)
    large = max_exact + (jnp.log(nf / max_exact) / math.log(REL_MAX_DIST / max_exact)
                         * (REL_BUCKETS - max_exact)).astype(jnp.int32)
    large = jnp.minimum(large, REL_BUCKETS - 1)
    return jnp.where(n < max_exact, n, large)


def _bias_table_kernel(bkt_ref, rb_ref, o_ref):
    bkt = bkt_ref[0]
    for h in range(N_HEADS):
        acc = jnp.full(bkt.shape, NEG_INF, F32)
        for k in range(REL_BUCKETS):
            acc = jnp.where(bkt == k, rb_ref[k, h], acc)
        o_ref[0, h] = acc


def bias_table(bkt, rel_bias):
    n, r, c = bkt.shape
    return pl.pallas_call(
        _bias_table_kernel,
        grid=(n,),
        in_specs=[pl.BlockSpec((1, r, c), lambda i: (i, 0, 0)),
                  pl.BlockSpec(memory_space=pltpu.SMEM)],
        out_specs=pl.BlockSpec((1, N_HEADS, r, c), lambda i: (i, 0, 0, 0)),
        out_shape=jax.ShapeDtypeStruct((n, N_HEADS, r, c), F32),
        compiler_params=_params("parallel"),
        name="bias_table",
    )(bkt, rel_bias.astype(F32))


def _dilated_buckets(window, dilation):
    qi = jnp.arange(BLOCK)[:, None]
    ki = jnp.arange(2 * BLOCK)[None, :]
    rel = qi + BLOCK - ki
    bkt = _rel_bucket(rel * dilation)
    band = (rel >= 0) & (rel <= window // dilation)
    return jnp.stack([jnp.where(band, bkt, -1),
                      jnp.where(band & (ki >= BLOCK), bkt, -1)]).astype(jnp.int32)


def _dsa_buckets(s):
    nd = s // BLOCK
    key = jnp.arange(BLOCK)[:, None]
    qry = jnp.arange(BLOCK)[None, :]
    dist = jnp.arange(nd)[:, None, None] * BLOCK + (qry - key)[None]
    return _rel_bucket(dist).astype(jnp.int32)


def kernel(x, mem, rel_bias, mem_norm, final_norm, mixer_norm, ffn_norm, w_mem_kv, w_out,
           even_w_in, even_w_gate, even_w_up, even_w_down,
           odd_w_in, odd_kv_norm, odd_w_uk, odd_w_uv,
           odd_w_router, odd_w_gate, odd_w_up, odd_w_down):
    b, s, d = x.shape
    t = b * s
    depth = mixer_norm.shape[0]
    xt = x.reshape(t, d)
    mem2 = mem.reshape(b * N_MEM, d)
    dil_tables = [bias_table(_dilated_buckets(w, dl), rel_bias) for w, dl in DIL_PAIRS]
    tz = bias_table(_dsa_buckets(s), rel_bias * LOG2_E)

    for i in range(depth):
        j = i // 2
        kvm = norm_matmul(mem2, mem_norm, w_mem_kv[i].astype(BF16), BF16, 512, 512)
        kvm = kvm.reshape(b, N_MEM, 2 * CROSS_WIDTH)
        wo = w_out[i].astype(BF16)
        if i % 2 == 0:
            w_in = even_w_in[j].astype(BF16)
            proj = even_in_projection(xt, mixer_norm[i], w_in[:, :5 * ATTN_W], b, s, 512)
            qc = norm_matmul(xt, mixer_norm[i], w_in[:, 5 * ATTN_W:], BF16, 512, CROSS_WIDTH)
            outs, lses = [], []
            for grp in range(len(DIL_PAIRS)):
                o, lse = dilated_branch(proj[grp], proj[3 + grp], proj[6 + grp], dil_tables[grp])
                outs.append(o)
                lses.append(lse)
            mix = combine_groups(outs, lses, 256).reshape(t, ATTN_W)
            cross = cross_attention(qc.reshape(b, s, CROSS_WIDTH), 0, kvm, 512)
        else:
            o_q, o_c, o_qi, o_ki, o_wi, o_qc = 0, 1024, 1280, 1792, 1856, 1864
            w = odd_w_in[j]
            pad = jnp.zeros((d, 128 - IDX_DIM - N_IDX_HEADS), w.dtype)
            w_in = jnp.concatenate(
                [w[:, o_q:o_c], w[:, o_qi:o_ki], w[:, o_c:o_qi], w[:, o_qc:],
                 w[:, o_ki:o_wi], w[:, o_wi:o_qc], pad], axis=1).astype(BF16)
            p1 = norm_matmul(xt, mixer_norm[i], w_in, BF16, 512, w_in.shape[1])
            p1 = p1.reshape(b, s, w_in.shape[1])
            c, ct = kv_latent_norm(p1, 6, odd_kv_norm[j])
            mask = dsa_select(p1, 2, 16)
            wuk = jnp.transpose(odd_w_uk[j], (1, 0, 2)).astype(BF16)
            wuvt = jnp.transpose(odd_w_uv[j], (1, 2, 0)).astype(BF16)
            mix = dsa_attention(p1, 0, c, ct, mask, tz, wuk, wuvt).reshape(t, ATTN_W)
            cross = cross_attention(p1, 7, kvm, 512)
        xt = out_projection(mix, cross.reshape(t, CROSS_WIDTH), wo[:ATTN_W], wo[ATTN_W:], xt, 512)
        if i % 2 == 0:
            xt = swiglu_ffn(xt, ffn_norm[i], even_w_gate[j].astype(BF16), even_w_up[j].astype(BF16),
                            even_w_down[j].astype(BF16), 512, 1408)
        else:
            wr = jnp.pad(odd_w_router[j], ((0, 0), (0, 128 - N_EXPERTS)))
            xt = moe_ffn(xt, ffn_norm[i], wr, odd_w_gate[j].astype(BF16), odd_w_up[j].astype(BF16),
                         odd_w_down[j].astype(BF16))
    return rmsnorm(xt, final_norm, 512).reshape(b, s, d)
```

```python
import functools
import math

import jax
import jax.numpy as jnp
from jax import lax
from jax.experimental import pallas as pl
from jax.experimental.pallas import tpu as pltpu

D_MODEL = 1024
N_HEADS = 16
HEAD_DIM = 64
ATTN_W = N_HEADS * HEAD_DIM
DIL_PAIRS = ((128, 1), (512, 4), (2048, 16))
BLOCK = 128
LANES = 128
KV_RANK = 256
N_IDX_HEADS = 8
IDX_DIM = 64
TOPK = 256
N_MEM = 256
N_CROSS_HEADS = 4
CROSS_WIDTH = 256
REL_BUCKETS = 32
REL_MAX_DIST = 2048
N_EXPERTS = 8
RMS_EPS = 1e-6
NEG_INF = -1e30
M_INIT = -5e29
LOG2_E = math.log2(math.e)

F32 = jnp.float32
BF16 = jnp.bfloat16
V7X_VMEM_LIMIT = 56 * 1024 * 1024
DSA_QB = 256
DSA_HEAD_GROUP = 8
ONES_ROWS = 16
INT_MIN = -(2 ** 31)
MOE_TILE = 512
ROUTE_TM = 256

_NT = (((1,), (1,)), ((), ()))


def _params(*sem):
    return pltpu.CompilerParams(dimension_semantics=sem, vmem_limit_bytes=V7X_VMEM_LIMIT)


def _rms(x, g):
    return x * lax.rsqrt(jnp.mean(x * x, axis=-1, keepdims=True) + RMS_EPS) * g


def _norm_mm_kernel(x_ref, g_ref, w_ref, o_ref, xn_ref):
    @pl.when(pl.program_id(1) == 0)
    def _():
        xn_ref[...] = _rms(x_ref[...].astype(F32), g_ref[...]).astype(BF16)

    o_ref[...] = jnp.dot(xn_ref[...], w_ref[...], preferred_element_type=F32).astype(o_ref.dtype)


def norm_matmul(x, g, w, out_dtype, tm, tn):
    t, k = x.shape
    n = w.shape[1]
    return pl.pallas_call(
        _norm_mm_kernel,
        grid=(t // tm, n // tn),
        in_specs=[pl.BlockSpec((tm, k), lambda i, j: (i, 0)),
                  pl.BlockSpec((1, k), lambda i, j: (0, 0)),
                  pl.BlockSpec((k, tn), lambda i, j: (0, j))],
        out_specs=pl.BlockSpec((tm, tn), lambda i, j: (i, j)),
        out_shape=jax.ShapeDtypeStruct((t, n), out_dtype),
        scratch_shapes=[pltpu.VMEM((tm, k), BF16)],
        compiler_params=_params("parallel", "arbitrary"),
        name="norm_matmul",
    )(x, g.reshape(1, k), w)


def _even_proj_kernel(x_ref, g_ref, w_ref, q1_ref, q4_ref, q16_ref, k1_ref, k4_ref, k16_ref,
                      v1_ref, v4_ref, v16_ref, xn_ref, y_ref):
    j = pl.program_id(1)
    tm = x_ref.shape[0]

    @pl.when(j == 0)
    def _():
        xn_ref[...] = _rms(x_ref[...], g_ref[...]).astype(BF16)

    y = jnp.dot(xn_ref[...], w_ref[...], preferred_element_type=F32)
    n_lane_blocks = y.shape[1] // LANES
    for c in range(n_lane_blocks):
        y_ref[c] = y[:, c * LANES:(c + 1) * LANES]

    def put(dst_ref, d):
        if d == 1:
            dst_ref[0, 0] = y.astype(BF16)
            return
        for r in range(d):
            for c in range(n_lane_blocks):
                rows = y_ref[c, pl.ds(r, tm // d, stride=d), :]
                dst_ref[0, r, :, c * LANES:(c + 1) * LANES] = rows.astype(BF16)

    column_dsts = (((q1_ref, 1),), ((q4_ref, 4),), ((q16_ref, 16),),
                   ((k1_ref, 1), (k4_ref, 4), (k16_ref, 16)),
                   ((v1_ref, 1), (v4_ref, 4), (v16_ref, 16)))
    for col, dsts in enumerate(column_dsts):
        @pl.when(j == col)
        def _(dsts=dsts):
            for ref, d in dsts:
                put(ref, d)


def even_in_projection(x, g, w, b, s, tm):
    t, kdim = x.shape
    per_b = s // tm
    dils = [dl for _, dl in DIL_PAIRS]
    layouts = dils + dils + dils
    spec = lambda dl: pl.BlockSpec((1, dl, tm // dl, ATTN_W), lambda i, j: (i // per_b, 0, i % per_b, 0))
    return pl.pallas_call(
        _even_proj_kernel,
        grid=(t // tm, 5),
        in_specs=[pl.BlockSpec((tm, kdim), lambda i, j: (i, 0)),
                  pl.BlockSpec((1, kdim), lambda i, j: (0, 0)),
                  pl.BlockSpec((kdim, ATTN_W), lambda i, j: (0, j))],
        out_specs=[spec(dl) for dl in layouts],
        out_shape=[jax.ShapeDtypeStruct((b, dl, s // dl, ATTN_W), BF16) for dl in layouts],
        scratch_shapes=[pltpu.VMEM((tm, kdim), BF16), pltpu.VMEM((ATTN_W // LANES, tm, LANES), F32)],
        compiler_params=_params("parallel", "arbitrary"),
        name="even_in_projection",
    )(x, g.reshape(1, kdim), w)


def _dil_kernel(q_ref, kp_ref, kc_ref, vp_ref, vc_ref, tab_ref, o_ref, lse_ref, s_ref, p_ref, m_ref):
    first = (pl.program_id(2) == 0).astype(jnp.int32)
    pair = 2 * HEAD_DIM
    lo_q = lax.broadcasted_iota(jnp.int32, (BLOCK, pair), 1) < HEAD_DIM
    lo_k = lax.broadcasted_iota(jnp.int32, (2 * BLOCK, pair), 1) < HEAD_DIM
    ones_bd = jnp.concatenate([jnp.where(lo_k, 1.0, 0.0), jnp.where(lo_k, 0.0, 1.0)], axis=0).astype(BF16)
    scale = jnp.asarray(HEAD_DIM ** -0.5, BF16)
    zero = jnp.zeros((), BF16)
    n_pairs = N_HEADS // 2
    for p in range(n_pairs):
        cols = slice(p * pair, (p + 1) * pair)
        q = q_ref[0, 0, :, cols] * scale
        k = jnp.concatenate([kp_ref[0, 0, :, cols], kc_ref[0, 0, :, cols]], axis=0)
        q_ab = jnp.concatenate([jnp.where(lo_q, q, zero), jnp.where(lo_q, zero, q)], axis=0)
        s_ab = lax.dot_general(q_ab, k, _NT, preferred_element_type=F32)
        s_ref[2 * p] = s_ab[:BLOCK] + tab_ref[first, 2 * p]
        s_ref[2 * p + 1] = s_ab[BLOCK:] + tab_ref[first, 2 * p + 1]
    for p in range(n_pairs):
        sa = s_ref[2 * p]
        sb = s_ref[2 * p + 1]
        ma = jnp.max(sa, axis=-1, keepdims=True)
        mb = jnp.max(sb, axis=-1, keepdims=True)
        p_ref[p, :, :2 * BLOCK] = jnp.exp(sa - ma).astype(BF16)
        p_ref[p, :, 2 * BLOCK:] = jnp.exp(sb - mb).astype(BF16)
        m_ref[p] = jnp.where(lo_q, ma, mb)
    for p in range(n_pairs):
        cols = slice(p * pair, (p + 1) * pair)
        v = jnp.concatenate([vp_ref[0, 0, :, cols], vc_ref[0, 0, :, cols]], axis=0)
        v_bd = jnp.concatenate([jnp.where(lo_k, v, zero), jnp.where(lo_k, zero, v)], axis=0)
        ol = jnp.dot(p_ref[p], jnp.concatenate([v_bd, ones_bd], axis=1), preferred_element_type=F32)
        o, l = ol[:, :pair], ol[:, pair:]
        o_ref[0, 0, :, cols] = o / l
        lse_ref[0, 0, :, cols] = m_ref[p] + jnp.log(l)


def dilated_branch(q, k, v, table):
    b, dilation, ln, _ = q.shape
    nb = ln // BLOCK
    blk = (1, 1, BLOCK, ATTN_W)
    cur = pl.BlockSpec(blk, lambda bi, r, n: (bi, r, n, 0))
    prev = pl.BlockSpec(blk, lambda bi, r, n: (bi, r, jnp.maximum(n - 1, 0), 0))
    out = jax.ShapeDtypeStruct(q.shape, F32)
    return pl.pallas_call(
        _dil_kernel,
        grid=(b, dilation, nb),
        in_specs=[cur, prev, cur, prev, cur,
                  pl.BlockSpec(table.shape, lambda bi, r, n: (0, 0, 0, 0))],
        out_specs=[cur, cur],
        out_shape=[out, out],
        scratch_shapes=[pltpu.VMEM((N_HEADS, BLOCK, 2 * BLOCK), F32),
                        pltpu.VMEM((N_HEADS // 2, BLOCK, 4 * BLOCK), BF16),
                        pltpu.VMEM((N_HEADS // 2, BLOCK, 2 * HEAD_DIM), F32)],
        compiler_params=_params("parallel", "parallel", "arbitrary"),
        name=f"dilated_attn_d{dilation}",
    )(q, k, k, v, v, table)


def _combine_kernel(o1, o4, o16, l1, l4, l16, out_ref, so4, sl4, so16, sl16):
    tm = out_ref.shape[1]
    n_lane_blocks = out_ref.shape[2] // LANES

    def to_token_order(src_ref, dst_ref, d):
        for r in range(d):
            for c in range(n_lane_blocks):
                dst_ref[c, pl.ds(r, tm // d, stride=d), :] = src_ref[0, r, :, c * LANES:(c + 1) * LANES]

    to_token_order(o4, so4, 4)
    to_token_order(l4, sl4, 4)
    to_token_order(o16, so16, 16)
    to_token_order(l16, sl16, 16)
    for c in range(n_lane_blocks):
        cols = slice(c * LANES, (c + 1) * LANES)
        a1, a2, a3 = l1[0, 0, :, cols], sl4[c], sl16[c]
        m = jnp.maximum(jnp.maximum(a1, a2), a3)
        w1, w2, w3 = jnp.exp(a1 - m), jnp.exp(a2 - m), jnp.exp(a3 - m)
        num = w1 * o1[0, 0, :, cols] + w2 * so4[c] + w3 * so16[c]
        out_ref[0, :, cols] = (num / (w1 + w2 + w3)).astype(out_ref.dtype)


def combine_groups(outs, lses, tm):
    b, _, s, w = outs[0].shape
    spec = lambda dl: pl.BlockSpec((1, dl, tm // dl, w), lambda bi, i: (bi, 0, i, 0))
    specs = [spec(o.shape[1]) for o in outs]
    return pl.pallas_call(
        _combine_kernel,
        grid=(b, s // tm),
        in_specs=specs + specs,
        out_specs=pl.BlockSpec((1, tm, w), lambda bi, i: (bi, i, 0)),
        out_shape=jax.ShapeDtypeStruct((b, s, w), BF16),
        scratch_shapes=[pltpu.VMEM((w // LANES, tm, LANES), F32)] * 4,
        compiler_params=_params("parallel", "parallel"),
        name="combine_groups",
    )(*outs, *lses)


def _cross_kernel(q_ref, kv_ref, o_ref):
    for h in range(N_CROSS_HEADS):
        sl = slice(h * HEAD_DIM, (h + 1) * HEAD_DIM)
        vsl = slice(CROSS_WIDTH + h * HEAD_DIM, CROSS_WIDTH + (h + 1) * HEAD_DIM)
        s = lax.dot_general(q_ref[0, :, sl], kv_ref[0, :, sl], _NT,
                            preferred_element_type=F32) * (HEAD_DIM ** -0.5)
        m = jnp.max(s, axis=-1, keepdims=True)
        p = jnp.exp(s - m)
        l = jnp.sum(p, axis=-1, keepdims=True)
        o = jnp.dot(p.astype(BF16), kv_ref[0, :, vsl], preferred_element_type=F32)
        o_ref[0, :, sl] = (o / l).astype(o_ref.dtype)


def cross_attention(qsrc, qcol, kv, tm):
    b, s, _ = qsrc.shape
    return pl.pallas_call(
        _cross_kernel,
        grid=(b, s // tm),
        in_specs=[pl.BlockSpec((1, tm, CROSS_WIDTH), lambda bi, i: (bi, i, qcol)),
                  pl.BlockSpec((1, N_MEM, 2 * CROSS_WIDTH), lambda bi, i: (bi, 0, 0))],
        out_specs=pl.BlockSpec((1, tm, CROSS_WIDTH), lambda bi, i: (bi, i, 0)),
        out_shape=jax.ShapeDtypeStruct((b, s, CROSS_WIDTH), BF16),
        compiler_params=_params("parallel", "parallel"),
        name="cross_attn",
    )(qsrc, kv)


def _outproj_kernel(mix_ref, cr_ref, wa_ref, wb_ref, x_ref, o_ref):
    acc = jnp.dot(mix_ref[...], wa_ref[...], preferred_element_type=F32)
    acc = acc + jnp.dot(cr_ref[...], wb_ref[...], preferred_element_type=F32)
    o_ref[...] = x_ref[...] + acc


def out_projection(mix, cross, wa, wb, x, tm):
    t, d = x.shape
    return pl.pallas_call(
        _outproj_kernel,
        grid=(t // tm,),
        in_specs=[pl.BlockSpec((tm, ATTN_W), lambda i: (i, 0)),
                  pl.BlockSpec((tm, CROSS_WIDTH), lambda i: (i, 0)),
                  pl.BlockSpec(wa.shape, lambda i: (0, 0)),
                  pl.BlockSpec(wb.shape, lambda i: (0, 0)),
                  pl.BlockSpec((tm, d), lambda i: (i, 0))],
        out_specs=pl.BlockSpec((tm, d), lambda i: (i, 0)),
        out_shape=jax.ShapeDtypeStruct((t, d), F32),
        compiler_params=_params("parallel"),
        name="out_projection",
    )(mix, cross, wa, wb, x)


def _swiglu_kernel(x_ref, g_ref, wg_ref, wu_ref, wd_ref, o_ref, hn_ref):
    @pl.when(pl.program_id(1) == 0)
    def _():
        x = x_ref[...]
        hn_ref[...] = _rms(x, g_ref[...]).astype(BF16)
        o_ref[...] = x

    hn = hn_ref[...]
    a = jnp.dot(hn, wg_ref[...], preferred_element_type=F32)
    u = jnp.dot(hn, wu_ref[...], preferred_element_type=F32)
    act = (a * jax.nn.sigmoid(a) * u).astype(BF16)
    o_ref[...] += jnp.dot(act, wd_ref[...], preferred_element_type=F32)


def swiglu_ffn(x, g, wg, wu, wd, tm, tf):
    t, d = x.shape
    f = wg.shape[1]
    return pl.pallas_call(
        _swiglu_kernel,
        grid=(t // tm, f // tf),
        in_specs=[pl.BlockSpec((tm, d), lambda i, j: (i, 0)),
                  pl.BlockSpec((1, d), lambda i, j: (0, 0)),
                  pl.BlockSpec((d, tf), lambda i, j: (0, j)),
                  pl.BlockSpec((d, tf), lambda i, j: (0, j)),
                  pl.BlockSpec((tf, d), lambda i, j: (j, 0))],
        out_specs=pl.BlockSpec((tm, d), lambda i, j: (i, 0)),
        out_shape=jax.ShapeDtypeStruct((t, d), F32),
        scratch_shapes=[pltpu.VMEM((tm, d), BF16)],
        compiler_params=_params("parallel", "arbitrary"),
        name="swiglu_ffn",
    )(x, g.reshape(1, d), wg, wu, wd)


def _router_kernel(x_ref, g_ref, wr_ref, gates_ref, route_ref, counts_ref, run_ref):
    @pl.when(pl.program_id(0) == 0)
    def _():
        run_ref[...] = jnp.zeros(run_ref.shape, F32)

    hn = _rms(x_ref[...], g_ref[...])
    logits = jnp.dot(hn, wr_ref[...], preferred_element_type=F32,
                     precision=lax.Precision.HIGHEST)
    tm = logits.shape[0]
    lane = lax.broadcasted_iota(jnp.int32, logits.shape, 1)
    lg = jnp.where(lane < N_EXPERTS, logits, -jnp.inf)
    m1 = jnp.max(lg, axis=-1, keepdims=True)
    i1 = jnp.min(jnp.where(lg == m1, lane, 128), axis=-1, keepdims=True)
    lg2 = jnp.where(lane == i1, -jnp.inf, lg)
    m2 = jnp.max(lg2, axis=-1, keepdims=True)
    i2 = jnp.min(jnp.where(lg2 == m2, lane, 128), axis=-1, keepdims=True)
    e = jnp.exp(m2 - m1)
    gates_ref[...] = jnp.where(lane == 0, 1.0 / (1.0 + e), jnp.where(lane == 1, e / (1.0 + e), 0.0))

    assign = jnp.where((lane == i1) | (lane == i2), 1.0, 0.0)
    r = lax.broadcasted_iota(jnp.int32, (tm, tm), 0)
    c = lax.broadcasted_iota(jnp.int32, (tm, tm), 1)
    lower = jnp.where(r > c, 1.0, 0.0).astype(BF16)
    before = jnp.dot(lower, assign.astype(BF16), preferred_element_type=F32) + run_ref[...]
    rank1 = jnp.sum(jnp.where(lane == i1, before, 0.0), axis=-1, keepdims=True).astype(jnp.int32)
    rank2 = jnp.sum(jnp.where(lane == i2, before, 0.0), axis=-1, keepdims=True).astype(jnp.int32)
    route_ref[...] = jnp.where(lane == 0, i1, jnp.where(lane == 1, i2, jnp.where(
        lane == 2, rank1, jnp.where(lane == 3, rank2, 0))))
    run = run_ref[...] + jnp.sum(assign, axis=0, keepdims=True)
    run_ref[...] = run
    counts_ref[...] = run.astype(jnp.int32)


def router(x, g, wr_pad, tm):
    t, d = x.shape
    return pl.pallas_call(
        _router_kernel,
        grid=(t // tm,),
        in_specs=[pl.BlockSpec((tm, d), lambda i: (i, 0)),
                  pl.BlockSpec((1, d), lambda i: (0, 0)),
                  pl.BlockSpec((d, 128), lambda i: (0, 0))],
        out_specs=[pl.BlockSpec((tm, 128), lambda i: (i, 0)),
                   pl.BlockSpec((tm, 128), lambda i: (i, 0)),
                   pl.BlockSpec((1, 128), lambda i: (0, 0))],
        out_shape=[jax.ShapeDtypeStruct((t, 128), F32),
                   jax.ShapeDtypeStruct((t, 128), jnp.int32),
                   jax.ShapeDtypeStruct((1, 128), jnp.int32)],
        scratch_shapes=[pltpu.VMEM((1, 128), F32)],
        compiler_params=_params("arbitrary"),
        name="router",
    )(x, g.reshape(1, d), wr_pad)


def _row_copy(src, src_row, dst, dst_row, sem):
    return pltpu.make_async_copy(src.at[pl.ds(src_row, 1)], dst.at[pl.ds(dst_row, 1)], sem)


def _scatter_rows_kernel(d1_ref, d2_ref, x_ref, xs_in_ref, xs_ref, sem):
    del xs_in_ref
    tm = x_ref.shape[0]
    base = pl.program_id(0) * tm

    def issue(r8, carry):
        for k in range(8):
            r = r8 * 8 + k
            _row_copy(x_ref, r, xs_ref, d1_ref[base + r], sem).start(priority=k % 2)
            _row_copy(x_ref, r, xs_ref, d2_ref[base + r], sem).start(priority=(k + 1) % 2)
        return carry

    lax.fori_loop(0, tm // 8, issue, 0)

    for _ in range(2):
        pltpu.make_async_copy(x_ref, xs_ref.at[pl.ds(0, tm)], sem).wait()


def scatter_rows(x, dest1, dest2, n_rows, tm):
    t, d = x.shape
    zeros = jnp.zeros((n_rows, d), x.dtype)
    return pl.pallas_call(
        _scatter_rows_kernel,
        grid_spec=pltpu.PrefetchScalarGridSpec(
            num_scalar_prefetch=2,
            grid=(t // tm,),
            in_specs=[pl.BlockSpec((tm, d), lambda i, d1, d2: (i, 0)),
                      pl.BlockSpec(memory_space=pl.ANY)],
            out_specs=pl.BlockSpec(memory_space=pl.ANY),
            scratch_shapes=[pltpu.SemaphoreType.DMA(())]),
        out_shape=jax.ShapeDtypeStruct((n_rows, d), x.dtype),
        input_output_aliases={3: 0},
        compiler_params=_params("arbitrary"),
        name="moe_scatter_rows",
    )(dest1, dest2, x, zeros)


def _moe_group_kernel(te_ref, nu_ref, x_ref, g_ref, wg_ref, wu_ref, wd_ref, o_ref, hn_ref):
    i = pl.program_id(0)
    j = pl.program_id(1)
    used = i < nu_ref[0]

    @pl.when(j == 0)
    def _():
        o_ref[...] = jnp.zeros(o_ref.shape, F32)

    @pl.when(used & (j == 0))
    def _():
        hn_ref[...] = _rms(x_ref[...], g_ref[...]).astype(BF16)

    @pl.when(used)
    def _():
        hn = hn_ref[...]
        a = jnp.dot(hn, wg_ref[...], preferred_element_type=F32)
        u = jnp.dot(hn, wu_ref[...], preferred_element_type=F32)
        act = (a * jax.nn.sigmoid(a) * u).astype(BF16)
        o_ref[...] += jnp.dot(act, wd_ref[...], preferred_element_type=F32)


def moe_group_ffn(xs, g, tile_expert, n_used, wg, wu, wd, tm, tf):
    p, d = xs.shape
    f = wg.shape[2]
    nj = f // tf
    chunk = lambda i, j, te, nu: jnp.where(i < nu[0], j, nj - 1)
    return pl.pallas_call(
        _moe_group_kernel,
        grid_spec=pltpu.PrefetchScalarGridSpec(
            num_scalar_prefetch=2,
            grid=(p // tm, nj),
            in_specs=[pl.BlockSpec((tm, d), lambda i, j, te, nu: (i, 0)),
                      pl.BlockSpec((1, d), lambda i, j, te, nu: (0, 0)),
                      pl.BlockSpec((None, d, tf), lambda i, j, te, nu: (te[i], 0, chunk(i, j, te, nu))),
                      pl.BlockSpec((None, d, tf), lambda i, j, te, nu: (te[i], 0, chunk(i, j, te, nu))),
                      pl.BlockSpec((None, tf, d), lambda i, j, te, nu: (te[i], chunk(i, j, te, nu), 0))],
            out_specs=pl.BlockSpec((tm, d), lambda i, j, te, nu: (i, 0)),
            scratch_shapes=[pltpu.VMEM((tm, d), BF16)]),
        out_shape=jax.ShapeDtypeStruct((p, d), F32),
        compiler_params=_params("arbitrary", "arbitrary"),
        name="moe_group_ffn",
    )(tile_expert, n_used, xs, g.reshape(1, d), wg, wu, wd)


def _moe_combine_kernel(d1_ref, d2_ref, x_ref, gates_ref, fg_ref, ys_ref, o_ref, y1_ref, y2_ref, sem,
                        *, final_norm):
    tm = x_ref.shape[0]
    i = pl.program_id(0)

    def issue(tile, slot):
        base = tile * tm

        def body(r8, carry):
            for k in range(8):
                r = r8 * 8 + k
                _row_copy(ys_ref, d1_ref[base + r], y1_ref.at[slot], r, sem.at[slot]).start(priority=k % 2)
                _row_copy(ys_ref, d2_ref[base + r], y2_ref.at[slot], r, sem.at[slot]).start(priority=(k + 1) % 2)
            return carry

        lax.fori_loop(0, tm // 8, body, 0)

    @pl.when(i == 0)
    def _():
        issue(0, 0)

    @pl.when(i + 1 < pl.num_programs(0))
    def _():
        issue(i + 1, (i + 1) % 2)

    slot = i % 2
    pltpu.make_async_copy(ys_ref.at[pl.ds(0, tm)], y1_ref.at[slot], sem.at[slot]).wait()
    pltpu.make_async_copy(ys_ref.at[pl.ds(0, tm)], y2_ref.at[slot], sem.at[slot]).wait()
    gates = gates_ref[...]
    out = x_ref[...] + gates[:, 0:1] * y1_ref[slot] + gates[:, 1:2] * y2_ref[slot]
    if final_norm:
        out = _rms(out, fg_ref[...])
    o_ref[...] = out


def moe_combine(x, gates, ys, dest1, dest2, tm, final_gain=None):
    t, d = x.shape
    fg = jnp.ones((1, d), F32) if final_gain is None else final_gain.reshape(1, d)
    return pl.pallas_call(
        functools.partial(_moe_combine_kernel, final_norm=final_gain is not None),
        grid_spec=pltpu.PrefetchScalarGridSpec(
            num_scalar_prefetch=2,
            grid=(t // tm,),
            in_specs=[pl.BlockSpec((tm, d), lambda i, d1, d2: (i, 0)),
                      pl.BlockSpec((tm, 128), lambda i, d1, d2: (i, 0)),
                      pl.BlockSpec((1, d), lambda i, d1, d2: (0, 0)),
                      pl.BlockSpec(memory_space=pl.ANY)],
            out_specs=pl.BlockSpec((tm, d), lambda i, d1, d2: (i, 0)),
            scratch_shapes=[pltpu.VMEM((2, tm, d), F32), pltpu.VMEM((2, tm, d), F32),
                            pltpu.SemaphoreType.DMA((2,))]),
        out_shape=jax.ShapeDtypeStruct((t, d), F32),
        compiler_params=_params("arbitrary"),
        name="moe_combine",
    )(dest1, dest2, x, gates, fg, ys)


def moe_ffn(x, g, wr_pad, wg, wu, wd, final_gain=None):
    t, d = x.shape
    ne = wg.shape[0]
    gates, route, counts = router(x, g, wr_pad, ROUTE_TM)
    counts = counts[0, :ne]
    padded = (counts + MOE_TILE - 1) // MOE_TILE * MOE_TILE
    ends = jnp.cumsum(padded)
    starts = ends - padded
    expert_ids = jnp.arange(ne, dtype=jnp.int32)[None, :]
    start_of = lambda e: jnp.sum(jnp.where(e[:, None] == expert_ids, starts[None, :], 0), axis=1)
    dest1 = (start_of(route[:, 0]) + route[:, 2]).astype(jnp.int32)
    dest2 = (start_of(route[:, 1]) + route[:, 3]).astype(jnp.int32)
    n_tiles = (2 * t) // MOE_TILE + ne
    tile_start = jnp.arange(n_tiles, dtype=jnp.int32) * MOE_TILE
    tile_expert = jnp.minimum(jnp.sum(tile_start[:, None] >= ends[None, :], axis=1), ne - 1).astype(jnp.int32)
    n_used = (ends[-1:] // MOE_TILE).astype(jnp.int32)
    xs = scatter_rows(x, dest1, dest2, n_tiles * MOE_TILE, ROUTE_TM)
    ys = moe_group_ffn(xs, g, tile_expert, n_used, wg, wu, wd, MOE_TILE, 512)
    return moe_combine(x, gates, ys, dest1, dest2, ROUTE_TM, final_gain)


def _kvnorm_kernel(c_ref, g_ref, o_ref, ot_ref):
    y = _rms(c_ref[0].astype(F32), g_ref[...])
    o_ref[0, 0] = y.astype(BF16)
    ot_ref[0, 0, :KV_RANK] = y.T.astype(BF16)
    ot_ref[0, 0, KV_RANK:] = jnp.ones((ONES_ROWS, DSA_QB), BF16)


def kv_latent_norm(p1, ccol, g):
    b, s, _ = p1.shape
    nc = s // DSA_QB
    out = jax.ShapeDtypeStruct((b, nc, DSA_QB, KV_RANK), BF16)
    out_t = jax.ShapeDtypeStruct((b, nc, KV_RANK + ONES_ROWS, DSA_QB), BF16)
    blk = pl.BlockSpec((1, 1, DSA_QB, KV_RANK), lambda bi, j: (bi, j, 0, 0))
    blk_t = pl.BlockSpec((1, 1, KV_RANK + ONES_ROWS, DSA_QB), lambda bi, j: (bi, j, 0, 0))
    return pl.pallas_call(
        _kvnorm_kernel,
        grid=(b, nc),
        in_specs=[pl.BlockSpec((1, DSA_QB, KV_RANK), lambda bi, j: (bi, j, ccol)),
                  pl.BlockSpec((1, KV_RANK), lambda bi, j: (0, 0))],
        out_specs=[blk, blk_t],
        out_shape=[out, out_t],
        compiler_params=_params("parallel", "parallel"),
        name="kv_latent_norm",
    )(p1, g.reshape(1, KV_RANK))


def _select_kernel(qi_ref, kall_ref, kq_ref, mask_ref, keys_ref, hi_ref, lo_ref, jcut_ref, *,
                   nchunks_total):
    qb = DSA_QB
    i = pl.program_id(1)
    nch = i + 1
    shape = (qb, qb)
    row = lax.broadcasted_iota(jnp.int32, shape, 0)
    qpos = i * qb + lax.broadcasted_iota(jnp.int32, shape, 1)

    wt = kq_ref[0].astype(F32).T
    wscale = (N_IDX_HEADS ** -0.5) * (IDX_DIM ** -0.5)

    def score_chunk(j, carry):
        start = pl.multiple_of(j * qb, qb)
        kc = kall_ref[0, pl.ds(start, qb), :][:, :IDX_DIM]
        acc = jnp.zeros(shape, F32)
        for h in range(N_IDX_HEADS):
            qh = qi_ref[0, :, h * IDX_DIM:(h + 1) * IDX_DIM]
            s = lax.dot_general(kc, qh, _NT, preferred_element_type=F32)
            acc = acc + jnp.maximum(s, 0.0) * (wt[IDX_DIM + h:IDX_DIM + h + 1, :] * wscale)
        acc = jnp.where(j * qb + row <= qpos, acc, NEG_INF)
        acc = jnp.where(acc == 0.0, 0.0, acc)
        bits = pltpu.bitcast(acc, jnp.int32)
        key = bits ^ ((bits >> 31) & 0x7FFFFFFF)
        keys_ref[j] = key
        hi_ref[j] = (key >> 16).astype(jnp.int16)
        return carry

    lax.fori_loop(0, nch, score_chunk, 0)

    def count(pred):
        def body(j, c):
            hit = jnp.where(pred(keys_ref[j], j), 1, 0)
            return c + jnp.sum(hit.reshape(qb // 8, 8, qb), axis=0)
        c = lax.fori_loop(0, nch, body, jnp.zeros((8, qb), jnp.int32))
        return jnp.sum(c, axis=0, keepdims=True)

    def count16(ref, pred):
        def body(j, c):
            hit = jnp.where(pred(ref[j]), jnp.int16(1), jnp.int16(0)).reshape(qb // 16, 16, qb)
            for g in range(qb // 16):
                c = c + hit[g]
            return c
        c = lax.fori_loop(0, nch, body, jnp.zeros((16, qb), jnp.int16))
        return jnp.sum(c.astype(jnp.int32), axis=0, keepdims=True)

    def kth_largest16(ref, rank):
        def ge(cand):
            c16 = cand.astype(jnp.int16)
            return count16(ref, lambda k: k >= c16) >= rank
        x = jnp.where(ge(jnp.zeros((1, qb), jnp.int32)), 0, -(1 << 15))

        def bit_body(t, x):
            cand = x + jnp.left_shift(jnp.int32(1), 14 - t)
            return jnp.where(ge(cand), cand, x)
        return lax.fori_loop(0, 15, bit_body, x)

    thr_hi = kth_largest16(hi_ref, TOPK)
    thr_hi16 = thr_hi.astype(jnp.int16)
    rank_lo = TOPK - count16(hi_ref, lambda k: k > thr_hi16)

    def low_half_chunk(j, carry):
        key = keys_ref[j]
        lo = (key & 0xFFFF) - (1 << 15)
        lo_ref[j] = jnp.where((key >> 16) == thr_hi, lo, -(1 << 15)).astype(jnp.int16)
        return carry

    lax.fori_loop(0, nch, low_half_chunk, 0)
    thr = thr_hi * (1 << 16) + (kth_largest16(lo_ref, rank_lo) + (1 << 15))

    n_gt = count(lambda key, j: key > thr)
    n_ge = count(lambda key, j: key >= thr)
    need = TOPK - n_gt

    jcut_ref[...] = jnp.full((1, qb), 1 << 30, jnp.int32)

    @pl.when(jnp.max(n_ge) > TOPK)
    def _():
        def idx_body(t, x):
            cand = x + jnp.left_shift(jnp.int32(1), 10 - t)
            below = count(lambda key, j: (key == thr) & (j * qb + row <= cand - 1))
            return jnp.where(below < need, cand, x)
        jcut_ref[...] = lax.fori_loop(0, 11, idx_body, jnp.zeros((1, qb), jnp.int32))

    jcut = jcut_ref[...]

    def write_chunk(j, carry):
        key = keys_ref[j]
        spos = j * qb + row
        sel = (key > thr) | ((key == thr) & (spos <= jcut))
        sel = sel & (spos <= qpos)
        start = pl.multiple_of(j * qb, qb)
        mask_ref[0, pl.ds(start, qb), :] = jnp.where(sel, 0.0, NEG_INF)
        return carry

    lax.fori_loop(0, nch, write_chunk, 0)

    def fill_chunk(j, carry):
        start = pl.multiple_of(j * qb, qb)
        mask_ref[0, pl.ds(start, qb), :] = jnp.full(shape, NEG_INF, F32)
        return carry

    lax.fori_loop(nch, nchunks_total, fill_chunk, 0)


def dsa_select(p1, qicol, kwcol):
    b, s, _ = p1.shape
    nq = s // DSA_QB
    return pl.pallas_call(
        functools.partial(_select_kernel, nchunks_total=nq),
        grid=(b, nq),
        in_specs=[pl.BlockSpec((1, DSA_QB, N_IDX_HEADS * IDX_DIM), lambda bi, i: (bi, i, qicol)),
                  pl.BlockSpec((1, s, 128), lambda bi, i: (bi, 0, kwcol)),
                  pl.BlockSpec((1, DSA_QB, 128), lambda bi, i: (bi, i, kwcol))],
        out_specs=pl.BlockSpec((1, s, DSA_QB), lambda bi, i: (bi, 0, i)),
        out_shape=jax.ShapeDtypeStruct((b, s, s), F32),
        scratch_shapes=[pltpu.VMEM((nq, DSA_QB, DSA_QB), jnp.int32),
                        pltpu.VMEM((nq, DSA_QB, DSA_QB), jnp.int16),
                        pltpu.VMEM((nq, DSA_QB, DSA_QB), jnp.int16),
                        pltpu.VMEM((1, DSA_QB), jnp.int32)],
        compiler_params=_params("parallel", "arbitrary"),
        name="dsa_select",
    )(p1, p1, p1)


def _dsa_attn_kernel(q_ref, c_ref, ct_ref, mask_ref, tz_ref, wuk_ref, wuvt_ref, o_ref,
                     ql_ref, ot_ref, acc_ref):
    qb = DSA_QB
    hg = DSA_HEAD_GROUP
    width = hg * qb
    i = pl.program_id(1)

    for h in range(N_HEADS):
        qh = q_ref[0, :, h * HEAD_DIM:(h + 1) * HEAD_DIM]
        ql = lax.dot_general(wuk_ref[h], qh, _NT, preferred_element_type=F32)
        ql_ref[h // hg, :, (h % hg) * qb:(h % hg + 1) * qb] = (
            ql * (HEAD_DIM ** -0.5 * LOG2_E)).astype(BF16)

    def group_body(g, carry):
        acc_ref[...] = jnp.zeros((KV_RANK, width), F32)

        def chunk(j, ml):
            m, l = ml
            s = jnp.dot(c_ref[0, j], ql_ref[g], preferred_element_type=F32)
            d = 2 * (i - j)
            start = pl.multiple_of(j * qb, qb)
            mk = mask_ref[0, pl.ds(start, qb), :]
            tiles = []
            for hh in range(hg):
                h = g * hg + hh
                t_diag = tz_ref[d, h]
                tiles.append(mk + jnp.concatenate(
                    [jnp.concatenate([t_diag, tz_ref[d + 1, h]], axis=1),
                     jnp.concatenate([tz_ref[jnp.maximum(d - 1, 0), h], t_diag], axis=1)], axis=0))
            s = s + jnp.concatenate(tiles, axis=1)
            m_new = jnp.maximum(m, jnp.max(s, axis=0, keepdims=True))
            p = jnp.exp2(s - m_new)
            alpha = jnp.exp2(m - m_new)
            pv = jnp.dot(ct_ref[0, j], p.astype(BF16), preferred_element_type=F32)
            l = alpha * l + pv[KV_RANK:KV_RANK + 1]
            acc_ref[...] = alpha * acc_ref[...] + pv[:KV_RANK]
            return m_new, l

        init = (jnp.full((1, width), M_INIT, F32), jnp.zeros((1, width), F32))
        _, l = lax.fori_loop(0, i + 1, chunk, init)
        o_lat = (acc_ref[...] / l).astype(BF16)
        for hh in range(hg):
            h = g * hg + hh
            ot_ref[h] = jnp.dot(wuvt_ref[h], o_lat[:, hh * qb:(hh + 1) * qb],
                                preferred_element_type=F32)
        return carry

    lax.fori_loop(0, N_HEADS // hg, group_body, 0)
    o_ref[0] = ot_ref[...].reshape(ATTN_W, qb).T.astype(o_ref.dtype)


def dsa_attention(p1, qcol, c, ct, mask, tz, wuk, wuvt):
    b, s, _ = p1.shape
    nq = s // DSA_QB
    const = lambda shape: pl.BlockSpec(shape, lambda bi, i: (0,) * len(shape))
    return pl.pallas_call(
        _dsa_attn_kernel,
        grid=(b, nq),
        in_specs=[pl.BlockSpec((1, DSA_QB, ATTN_W), lambda bi, i: (bi, i, qcol)),
                  pl.BlockSpec((1, nq, DSA_QB, KV_RANK), lambda bi, i: (bi, 0, 0, 0)),
                  pl.BlockSpec((1, nq, KV_RANK + ONES_ROWS, DSA_QB), lambda bi, i: (bi, 0, 0, 0)),
                  pl.BlockSpec((1, s, DSA_QB), lambda bi, i: (bi, 0, i)),
                  pl.BlockSpec(tz.shape, lambda bi, i: (0, 0, 0, 0), pipeline_mode=pl.Buffered(1)),
                  const(wuk.shape), const(wuvt.shape)],
        out_specs=pl.BlockSpec((1, DSA_QB, ATTN_W), lambda bi, i: (bi, i, 0)),
        out_shape=jax.ShapeDtypeStruct((b, s, ATTN_W), BF16),
        scratch_shapes=[pltpu.VMEM((N_HEADS // DSA_HEAD_GROUP, KV_RANK, DSA_HEAD_GROUP * DSA_QB), BF16),
                        pltpu.VMEM((N_HEADS, HEAD_DIM, DSA_QB), F32),
                        pltpu.VMEM((KV_RANK, DSA_HEAD_GROUP * DSA_QB), F32)],
        compiler_params=_params("parallel", "arbitrary"),
        name="dsa_attention",
    )(p1, c, ct, mask, tz, wuk, wuvt)


def _rmsnorm_kernel(x_ref, g_ref, o_ref):
    o_ref[...] = _rms(x_ref[...], g_ref[...])


def rmsnorm(x, g, tm):
    t, d = x.shape
    return pl.pallas_call(
        _rmsnorm_kernel,
        grid=(t // tm,),
        in_specs=[pl.BlockSpec((tm, d), lambda i: (i, 0)), pl.BlockSpec((1, d), lambda i: (0, 0))],
        out_specs=pl.BlockSpec((tm, d), lambda i: (i, 0)),
        out_shape=jax.ShapeDtypeStruct((t, d), F32),
        compiler_params=_params("parallel"),
        name="final_rmsnorm",
    )(x, g.reshape(1, d))


def _rel_bucket(dist):
    n = jnp.maximum(dist, 0)
    max_exact = REL_BUCKETS // 2
    nf = jnp.maximum(n, 1).astype(F32)
    large = max_exact + (jnp.log(nf / max_exact) / math.log(REL_MAX_DIST / max_exact)
                         * (REL_BUCKETS - max_exact)).astype(jnp.int32)
    large = jnp.minimum(large, REL_BUCKETS - 1)
    return jnp.where(n < max_exact, n, large)


def _bias_table_kernel(bkt_ref, rb_ref, o_ref):
    bkt = bkt_ref[0]
    for h in range(N_HEADS):
        acc = jnp.full(bkt.shape, NEG_INF, F32)
        for k in range(REL_BUCKETS):
            acc = jnp.where(bkt == k, rb_ref[k, h], acc)
        o_ref[0, h] = acc


def bias_table(bkt, rel_bias):
    n, r, c = bkt.shape
    return pl.pallas_call(
        _bias_table_kernel,
        grid=(n,),
        in_specs=[pl.BlockSpec((1, r, c), lambda i: (i, 0, 0)),
                  pl.BlockSpec(memory_space=pltpu.SMEM)],
        out_specs=pl.BlockSpec((1, N_HEADS, r, c), lambda i: (i, 0, 0, 0)),
        out_shape=jax.ShapeDtypeStruct((n, N_HEADS, r, c), F32),
        compiler_params=_params("parallel"),
        name="bias_table",
    )(bkt, rel_bias.astype(F32))


def _dilated_buckets(window, dilation):
    qi = jnp.arange(BLOCK)[:, None]
    ki = jnp.arange(2 * BLOCK)[None, :]
    rel = qi + BLOCK - ki
    bkt = _rel_bucket(rel * dilation)
    band = (rel >= 0) & (rel <= window // dilation)
    return jnp.stack([jnp.where(band, bkt, -1),
                      jnp.where(band & (ki >= BLOCK), bkt, -1)]).astype(jnp.int32)


def _dsa_buckets(s):
    nd = s // BLOCK
    key = jnp.arange(BLOCK)[:, None]
    qry = jnp.arange(BLOCK)[None, :]
    dist = jnp.arange(nd)[:, None, None] * BLOCK + (qry - key)[None]
    return _rel_bucket(dist).astype(jnp.int32)


def kernel(x, mem, rel_bias, mem_norm, final_norm, mixer_norm, ffn_norm, w_mem_kv, w_out,
           even_w_in, even_w_gate, even_w_up, even_w_down,
           odd_w_in, odd_kv_norm, odd_w_uk, odd_w_uv,
           odd_w_router, odd_w_gate, odd_w_up, odd_w_down):
    b, s, d = x.shape
    t = b * s
    depth = mixer_norm.shape[0]
    xt = x.reshape(t, d)
    mem2 = mem.reshape(b * N_MEM, d)
    dil_tables = [bias_table(_dilated_buckets(w, dl), rel_bias) for w, dl in DIL_PAIRS]
    tz = bias_table(_dsa_buckets(s), rel_bias * LOG2_E)

    for i in range(depth):
        j = i // 2
        kvm = norm_matmul(mem2, mem_norm, w_mem_kv[i].astype(BF16), BF16, 512, 512)
        kvm = kvm.reshape(b, N_MEM, 2 * CROSS_WIDTH)
        wo = w_out[i].astype(BF16)
        if i % 2 == 0:
            w_in = even_w_in[j].astype(BF16)
            proj = even_in_projection(xt, mixer_norm[i], w_in[:, :5 * ATTN_W], b, s, 512)
            qc = norm_matmul(xt, mixer_norm[i], w_in[:, 5 * ATTN_W:], BF16, 512, CROSS_WIDTH)
            outs, lses = [], []
            for grp in range(len(DIL_PAIRS)):
                o, lse = dilated_branch(proj[grp], proj[3 + grp], proj[6 + grp], dil_tables[grp])
                outs.append(o)
                lses.append(lse)
            mix = combine_groups(outs, lses, 256).reshape(t, ATTN_W)
            cross = cross_attention(qc.reshape(b, s, CROSS_WIDTH), 0, kvm, 512)
        else:
            o_q, o_c, o_qi, o_ki, o_wi, o_qc = 0, 1024, 1280, 1792, 1856, 1864
            w = odd_w_in[j]
            pad = jnp.zeros((d, 128 - IDX_DIM - N_IDX_HEADS), w.dtype)
            w_in = jnp.concatenate(
                [w[:, o_q:o_c], w[:, o_qi:o_ki], w[:, o_c:o_qi], w[:, o_qc:],
                 w[:, o_ki:o_wi], w[:, o_wi:o_qc], pad], axis=1).astype(BF16)
            p1 = norm_matmul(xt, mixer_norm[i], w_in, BF16, 512, w_in.shape[1])
            p1 = p1.reshape(b, s, w_in.shape[1])
            c, ct = kv_latent_norm(p1, 6, odd_kv_norm[j])
            mask = dsa_select(p1, 2, 16)
            wuk = jnp.transpose(odd_w_uk[j], (1, 0, 2)).astype(BF16)
            wuvt = jnp.transpose(odd_w_uv[j], (1, 2, 0)).astype(BF16)
            mix = dsa_attention(p1, 0, c, ct, mask, tz, wuk, wuvt).reshape(t, ATTN_W)
            cross = cross_attention(p1, 7, kvm, 512)
        xt = out_projection(mix, cross.reshape(t, CROSS_WIDTH), wo[:ATTN_W], wo[ATTN_W:], xt, 512)
        if i % 2 == 0:
            xt = swiglu_ffn(xt, ffn_norm[i], even_w_gate[j].astype(BF16), even_w_up[j].astype(BF16),
                            even_w_down[j].astype(BF16), 512, 1408)
        else:
            wr = jnp.pad(odd_w_router[j], ((0, 0), (0, 128 - N_EXPERTS)))
            xt = moe_ffn(xt, ffn_norm[i], wr, odd_w_gate[j].astype(BF16), odd_w_up[j].astype(BF16),
                         odd_w_down[j].astype(BF16), final_norm if i == depth - 1 else None)
    if depth % 2 == 1:
        xt = rmsnorm(xt, final_norm, 512)
    return xt.reshape(b, s, d)
```

```python
import functools
import math

import jax
import jax.numpy as jnp
import numpy as np
from jax import lax
from jax.experimental import pallas as pl
from jax.experimental.pallas import tpu as pltpu

D_MODEL = 1024
N_HEADS = 16
HEAD_DIM = 64
ATTN_W = N_HEADS * HEAD_DIM
DIL_PAIRS = ((128, 1), (512, 4), (2048, 16))
BLOCK = 128
LANES = 128
KV_RANK = 256
N_IDX_HEADS = 8
IDX_DIM = 64
TOPK = 256
N_MEM = 256
N_CROSS_HEADS = 4
CROSS_WIDTH = 256
REL_BUCKETS = 32
REL_MAX_DIST = 2048
N_EXPERTS = 8
RMS_EPS = 1e-6
NEG_INF = -1e30
M_INIT = -5e29
LOG2_E = math.log2(math.e)

F32 = jnp.float32
BF16 = jnp.bfloat16
V7X_VMEM_LIMIT = 56 * 1024 * 1024
DSA_QB = 256
DSA_HEAD_GROUP = 8
ONES_ROWS = 16
INT_MIN = -(2 ** 31)
MOE_TILE = 512
ROUTE_TM = 256

_NT = (((1,), (1,)), ((), ()))


def _params(*sem):
    return pltpu.CompilerParams(dimension_semantics=sem, vmem_limit_bytes=V7X_VMEM_LIMIT)


def _rms(x, g):
    return x * lax.rsqrt(jnp.mean(x * x, axis=-1, keepdims=True) + RMS_EPS) * g


def _norm_mm_kernel(x_ref, g_ref, w_ref, o_ref, xn_ref):
    @pl.when(pl.program_id(1) == 0)
    def _():
        xn_ref[...] = _rms(x_ref[...].astype(F32), g_ref[...]).astype(BF16)

    o_ref[...] = jnp.dot(xn_ref[...], w_ref[...], preferred_element_type=F32).astype(o_ref.dtype)


def norm_matmul(x, g, w, out_dtype, tm, tn):
    t, k = x.shape
    n = w.shape[1]
    return pl.pallas_call(
        _norm_mm_kernel,
        grid=(t // tm, n // tn),
        in_specs=[pl.BlockSpec((tm, k), lambda i, j: (i, 0)),
                  pl.BlockSpec((1, k), lambda i, j: (0, 0)),
                  pl.BlockSpec((k, tn), lambda i, j: (0, j))],
        out_specs=pl.BlockSpec((tm, tn), lambda i, j: (i, j)),
        out_shape=jax.ShapeDtypeStruct((t, n), out_dtype),
        scratch_shapes=[pltpu.VMEM((tm, k), BF16)],
        compiler_params=_params("parallel", "arbitrary"),
        name="norm_matmul",
    )(x, g.reshape(1, k), w)


def _even_proj_kernel(x_ref, g_ref, w_ref, q1_ref, q4_ref, q16_ref, k1_ref, k4_ref, k16_ref,
                      v1_ref, v4_ref, v16_ref, xn_ref, y_ref):
    j = pl.program_id(1)
    tm = x_ref.shape[0]

    @pl.when(j == 0)
    def _():
        xn_ref[...] = _rms(x_ref[...], g_ref[...]).astype(BF16)

    y = jnp.dot(xn_ref[...], w_ref[...], preferred_element_type=F32)
    n_lane_blocks = y.shape[1] // LANES
    for c in range(n_lane_blocks):
        y_ref[c] = y[:, c * LANES:(c + 1) * LANES]

    def put(dst_ref, d):
        if d == 1:
            dst_ref[0, 0] = y.astype(BF16)
            return
        for r in range(d):
            for c in range(n_lane_blocks):
                rows = y_ref[c, pl.ds(r, tm // d, stride=d), :]
                dst_ref[0, r, :, c * LANES:(c + 1) * LANES] = rows.astype(BF16)

    column_dsts = (((q1_ref, 1),), ((q4_ref, 4),), ((q16_ref, 16),),
                   ((k1_ref, 1), (k4_ref, 4), (k16_ref, 16)),
                   ((v1_ref, 1), (v4_ref, 4), (v16_ref, 16)))
    for col, dsts in enumerate(column_dsts):
        @pl.when(j == col)
        def _(dsts=dsts):
            for ref, d in dsts:
                put(ref, d)


def even_in_projection(x, g, w, b, s, tm):
    t, kdim = x.shape
    per_b = s // tm
    dils = [dl for _, dl in DIL_PAIRS]
    layouts = dils + dils + dils
    spec = lambda dl: pl.BlockSpec((1, dl, tm // dl, ATTN_W), lambda i, j: (i // per_b, 0, i % per_b, 0))
    return pl.pallas_call(
        _even_proj_kernel,
        grid=(t // tm, 5),
        in_specs=[pl.BlockSpec((tm, kdim), lambda i, j: (i, 0)),
                  pl.BlockSpec((1, kdim), lambda i, j: (0, 0)),
                  pl.BlockSpec((kdim, ATTN_W), lambda i, j: (0, j))],
        out_specs=[spec(dl) for dl in layouts],
        out_shape=[jax.ShapeDtypeStruct((b, dl, s // dl, ATTN_W), BF16) for dl in layouts],
        scratch_shapes=[pltpu.VMEM((tm, kdim), BF16), pltpu.VMEM((ATTN_W // LANES, tm, LANES), F32)],
        compiler_params=_params("parallel", "arbitrary"),
        name="even_in_projection",
    )(x, g.reshape(1, kdim), w)


def _dil_kernel(q_ref, kp_ref, kc_ref, vp_ref, vc_ref, tab_ref, o_ref, lse_ref, s_ref, p_ref, m_ref):
    first = (pl.program_id(2) == 0).astype(jnp.int32)
    pair = 2 * HEAD_DIM
    lo_q = lax.broadcasted_iota(jnp.int32, (BLOCK, pair), 1) < HEAD_DIM
    lo_k = lax.broadcasted_iota(jnp.int32, (2 * BLOCK, pair), 1) < HEAD_DIM
    ones_bd = jnp.concatenate([jnp.where(lo_k, 1.0, 0.0), jnp.where(lo_k, 0.0, 1.0)], axis=0).astype(BF16)
    scale = jnp.asarray(HEAD_DIM ** -0.5, BF16)
    zero = jnp.zeros((), BF16)
    n_pairs = N_HEADS // 2
    for p in range(n_pairs):
        cols = slice(p * pair, (p + 1) * pair)
        q = q_ref[0, 0, :, cols] * scale
        k = jnp.concatenate([kp_ref[0, 0, :, cols], kc_ref[0, 0, :, cols]], axis=0)
        q_ab = jnp.concatenate([jnp.where(lo_q, q, zero), jnp.where(lo_q, zero, q)], axis=0)
        s_ab = lax.dot_general(q_ab, k, _NT, preferred_element_type=F32)
        s_ref[2 * p] = s_ab[:BLOCK] + tab_ref[first, 2 * p]
        s_ref[2 * p + 1] = s_ab[BLOCK:] + tab_ref[first, 2 * p + 1]
    for p in range(n_pairs):
        sa = s_ref[2 * p]
        sb = s_ref[2 * p + 1]
        ma = jnp.max(sa, axis=-1, keepdims=True)
        mb = jnp.max(sb, axis=-1, keepdims=True)
        p_ref[p, :, :2 * BLOCK] = jnp.exp(sa - ma).astype(BF16)
        p_ref[p, :, 2 * BLOCK:] = jnp.exp(sb - mb).astype(BF16)
        m_ref[p] = jnp.where(lo_q, ma, mb)
    for p in range(n_pairs):
        cols = slice(p * pair, (p + 1) * pair)
        v = jnp.concatenate([vp_ref[0, 0, :, cols], vc_ref[0, 0, :, cols]], axis=0)
        v_bd = jnp.concatenate([jnp.where(lo_k, v, zero), jnp.where(lo_k, zero, v)], axis=0)
        ol = jnp.dot(p_ref[p], jnp.concatenate([v_bd, ones_bd], axis=1), preferred_element_type=F32)
        o, l = ol[:, :pair], ol[:, pair:]
        o_ref[0, 0, :, cols] = o / l
        lse_ref[0, 0, :, cols] = m_ref[p] + jnp.log(l)


def dilated_branch(q, k, v, table):
    b, dilation, ln, _ = q.shape
    nb = ln // BLOCK
    blk = (1, 1, BLOCK, ATTN_W)
    cur = pl.BlockSpec(blk, lambda bi, r, n: (bi, r, n, 0))
    prev = pl.BlockSpec(blk, lambda bi, r, n: (bi, r, jnp.maximum(n - 1, 0), 0))
    out = jax.ShapeDtypeStruct(q.shape, F32)
    return pl.pallas_call(
        _dil_kernel,
        grid=(b, dilation, nb),
        in_specs=[cur, prev, cur, prev, cur,
                  pl.BlockSpec(table.shape, lambda bi, r, n: (0, 0, 0, 0))],
        out_specs=[cur, cur],
        out_shape=[out, out],
        scratch_shapes=[pltpu.VMEM((N_HEADS, BLOCK, 2 * BLOCK), F32),
                        pltpu.VMEM((N_HEADS // 2, BLOCK, 4 * BLOCK), BF16),
                        pltpu.VMEM((N_HEADS // 2, BLOCK, 2 * HEAD_DIM), F32)],
        compiler_params=_params("parallel", "parallel", "arbitrary"),
        name=f"dilated_attn_d{dilation}",
    )(q, k, k, v, v, table)


def _combine_kernel(o1, o4, o16, l1, l4, l16, out_ref, so4, sl4, so16, sl16):
    tm = out_ref.shape[1]
    n_lane_blocks = out_ref.shape[2] // LANES

    def to_token_order(src_ref, dst_ref, d):
        for r in range(d):
            for c in range(n_lane_blocks):
                dst_ref[c, pl.ds(r, tm // d, stride=d), :] = src_ref[0, r, :, c * LANES:(c + 1) * LANES]

    to_token_order(o4, so4, 4)
    to_token_order(l4, sl4, 4)
    to_token_order(o16, so16, 16)
    to_token_order(l16, sl16, 16)
    for c in range(n_lane_blocks):
        cols = slice(c * LANES, (c + 1) * LANES)
        a1, a2, a3 = l1[0, 0, :, cols], sl4[c], sl16[c]
        m = jnp.maximum(jnp.maximum(a1, a2), a3)
        w1, w2, w3 = jnp.exp(a1 - m), jnp.exp(a2 - m), jnp.exp(a3 - m)
        num = w1 * o1[0, 0, :, cols] + w2 * so4[c] + w3 * so16[c]
        out_ref[0, :, cols] = (num / (w1 + w2 + w3)).astype(out_ref.dtype)


def combine_groups(outs, lses, tm):
    b, _, s, w = outs[0].shape
    spec = lambda dl: pl.BlockSpec((1, dl, tm // dl, w), lambda bi, i: (bi, 0, i, 0))
    specs = [spec(o.shape[1]) for o in outs]
    return pl.pallas_call(
        _combine_kernel,
        grid=(b, s // tm),
        in_specs=specs + specs,
        out_specs=pl.BlockSpec((1, tm, w), lambda bi, i: (bi, i, 0)),
        out_shape=jax.ShapeDtypeStruct((b, s, w), BF16),
        scratch_shapes=[pltpu.VMEM((w // LANES, tm, LANES), F32)] * 4,
        compiler_params=_params("parallel", "parallel"),
        name="combine_groups",
    )(*outs, *lses)


def _cross_kernel(q_ref, kv_ref, o_ref):
    for h in range(N_CROSS_HEADS):
        sl = slice(h * HEAD_DIM, (h + 1) * HEAD_DIM)
        vsl = slice(CROSS_WIDTH + h * HEAD_DIM, CROSS_WIDTH + (h + 1) * HEAD_DIM)
        s = lax.dot_general(q_ref[0, :, sl], kv_ref[0, :, sl], _NT,
                            preferred_element_type=F32) * (HEAD_DIM ** -0.5)
        m = jnp.max(s, axis=-1, keepdims=True)
        p = jnp.exp(s - m)
        l = jnp.sum(p, axis=-1, keepdims=True)
        o = jnp.dot(p.astype(BF16), kv_ref[0, :, vsl], preferred_element_type=F32)
        o_ref[0, :, sl] = (o / l).astype(o_ref.dtype)


def cross_attention(qsrc, qcol, kv, tm):
    b, s, _ = qsrc.shape
    return pl.pallas_call(
        _cross_kernel,
        grid=(b, s // tm),
        in_specs=[pl.BlockSpec((1, tm, CROSS_WIDTH), lambda bi, i: (bi, i, qcol)),
                  pl.BlockSpec((1, N_MEM, 2 * CROSS_WIDTH), lambda bi, i: (bi, 0, 0))],
        out_specs=pl.BlockSpec((1, tm, CROSS_WIDTH), lambda bi, i: (bi, i, 0)),
        out_shape=jax.ShapeDtypeStruct((b, s, CROSS_WIDTH), BF16),
        compiler_params=_params("parallel", "parallel"),
        name="cross_attn",
    )(qsrc, kv)


def _outproj_kernel(mix_ref, cr_ref, wa_ref, wb_ref, x_ref, o_ref):
    acc = jnp.dot(mix_ref[...], wa_ref[...], preferred_element_type=F32)
    acc = acc + jnp.dot(cr_ref[...], wb_ref[...], preferred_element_type=F32)
    o_ref[...] = x_ref[...] + acc


def out_projection(mix, cross, wa, wb, x, tm):
    t, d = x.shape
    return pl.pallas_call(
        _outproj_kernel,
        grid=(t // tm,),
        in_specs=[pl.BlockSpec((tm, ATTN_W), lambda i: (i, 0)),
                  pl.BlockSpec((tm, CROSS_WIDTH), lambda i: (i, 0)),
                  pl.BlockSpec(wa.shape, lambda i: (0, 0)),
                  pl.BlockSpec(wb.shape, lambda i: (0, 0)),
                  pl.BlockSpec((tm, d), lambda i: (i, 0))],
        out_specs=pl.BlockSpec((tm, d), lambda i: (i, 0)),
        out_shape=jax.ShapeDtypeStruct((t, d), F32),
        compiler_params=_params("parallel"),
        name="out_projection",
    )(mix, cross, wa, wb, x)


def _swiglu_kernel(x_ref, g_ref, wg_ref, wu_ref, wd_ref, o_ref, hn_ref):
    @pl.when(pl.program_id(1) == 0)
    def _():
        x = x_ref[...]
        hn_ref[...] = _rms(x, g_ref[...]).astype(BF16)
        o_ref[...] = x

    hn = hn_ref[...]
    a = jnp.dot(hn, wg_ref[...], preferred_element_type=F32)
    u = jnp.dot(hn, wu_ref[...], preferred_element_type=F32)
    act = (a * jax.nn.sigmoid(a) * u).astype(BF16)
    o_ref[...] += jnp.dot(act, wd_ref[...], preferred_element_type=F32)


def swiglu_ffn(x, g, wg, wu, wd, tm, tf):
    t, d = x.shape
    f = wg.shape[1]
    return pl.pallas_call(
        _swiglu_kernel,
        grid=(t // tm, f // tf),
        in_specs=[pl.BlockSpec((tm, d), lambda i, j: (i, 0)),
                  pl.BlockSpec((1, d), lambda i, j: (0, 0)),
                  pl.BlockSpec((d, tf), lambda i, j: (0, j)),
                  pl.BlockSpec((d, tf), lambda i, j: (0, j)),
                  pl.BlockSpec((tf, d), lambda i, j: (j, 0))],
        out_specs=pl.BlockSpec((tm, d), lambda i, j: (i, 0)),
        out_shape=jax.ShapeDtypeStruct((t, d), F32),
        scratch_shapes=[pltpu.VMEM((tm, d), BF16)],
        compiler_params=_params("parallel", "arbitrary"),
        name="swiglu_ffn",
    )(x, g.reshape(1, d), wg, wu, wd)


def _router_kernel(x_ref, g_ref, wr_ref, gates_ref, route_ref, counts_ref, run_ref):
    @pl.when(pl.program_id(0) == 0)
    def _():
        run_ref[...] = jnp.zeros(run_ref.shape, F32)

    hn = _rms(x_ref[...], g_ref[...])
    logits = jnp.dot(hn, wr_ref[...], preferred_element_type=F32,
                     precision=lax.Precision.HIGHEST)
    tm = logits.shape[0]
    lane = lax.broadcasted_iota(jnp.int32, logits.shape, 1)
    lg = jnp.where(lane < N_EXPERTS, logits, -jnp.inf)
    m1 = jnp.max(lg, axis=-1, keepdims=True)
    i1 = jnp.min(jnp.where(lg == m1, lane, 128), axis=-1, keepdims=True)
    lg2 = jnp.where(lane == i1, -jnp.inf, lg)
    m2 = jnp.max(lg2, axis=-1, keepdims=True)
    i2 = jnp.min(jnp.where(lg2 == m2, lane, 128), axis=-1, keepdims=True)
    e = jnp.exp(m2 - m1)
    gates_ref[...] = jnp.where(lane == 0, 1.0 / (1.0 + e), jnp.where(lane == 1, e / (1.0 + e), 0.0))

    assign = jnp.where((lane == i1) | (lane == i2), 1.0, 0.0)
    r = lax.broadcasted_iota(jnp.int32, (tm, tm), 0)
    c = lax.broadcasted_iota(jnp.int32, (tm, tm), 1)
    lower = jnp.where(r > c, 1.0, 0.0).astype(BF16)
    before = jnp.dot(lower, assign.astype(BF16), preferred_element_type=F32) + run_ref[...]
    rank1 = jnp.sum(jnp.where(lane == i1, before, 0.0), axis=-1, keepdims=True).astype(jnp.int32)
    rank2 = jnp.sum(jnp.where(lane == i2, before, 0.0), axis=-1, keepdims=True).astype(jnp.int32)
    route_ref[...] = jnp.where(lane == 0, i1, jnp.where(lane == 1, i2, jnp.where(
        lane == 2, rank1, jnp.where(lane == 3, rank2, 0))))
    run = run_ref[...] + jnp.sum(assign, axis=0, keepdims=True)
    run_ref[...] = run
    counts_ref[...] = run.astype(jnp.int32)


def router(x, g, wr_pad, tm):
    t, d = x.shape
    return pl.pallas_call(
        _router_kernel,
        grid=(t // tm,),
        in_specs=[pl.BlockSpec((tm, d), lambda i: (i, 0)),
                  pl.BlockSpec((1, d), lambda i: (0, 0)),
                  pl.BlockSpec((d, 128), lambda i: (0, 0))],
        out_specs=[pl.BlockSpec((tm, 128), lambda i: (i, 0)),
                   pl.BlockSpec((tm, 128), lambda i: (i, 0)),
                   pl.BlockSpec((1, 128), lambda i: (0, 0))],
        out_shape=[jax.ShapeDtypeStruct((t, 128), F32),
                   jax.ShapeDtypeStruct((t, 128), jnp.int32),
                   jax.ShapeDtypeStruct((1, 128), jnp.int32)],
        scratch_shapes=[pltpu.VMEM((1, 128), F32)],
        compiler_params=_params("arbitrary"),
        name="router",
    )(x, g.reshape(1, d), wr_pad)


def _row_copy(src, src_row, dst, dst_row, sem):
    return pltpu.make_async_copy(src.at[pl.ds(src_row, 1)], dst.at[pl.ds(dst_row, 1)], sem)


def _scatter_rows_kernel(d1_ref, d2_ref, x_ref, xs_in_ref, xs_ref, sem):
    del xs_in_ref
    tm = x_ref.shape[0]
    base = pl.program_id(0) * tm

    def issue(r, carry):
        _row_copy(x_ref, r, xs_ref, d1_ref[base + r], sem).start()
        _row_copy(x_ref, r, xs_ref, d2_ref[base + r], sem).start()
        return carry

    lax.fori_loop(0, tm, issue, 0, unroll=8)

    for _ in range(2):
        pltpu.make_async_copy(x_ref, xs_ref.at[pl.ds(0, tm)], sem).wait()


def scatter_rows(x, dest1, dest2, n_rows, tm):
    t, d = x.shape
    zeros = jnp.zeros((n_rows, d), x.dtype)
    return pl.pallas_call(
        _scatter_rows_kernel,
        grid_spec=pltpu.PrefetchScalarGridSpec(
            num_scalar_prefetch=2,
            grid=(t // tm,),
            in_specs=[pl.BlockSpec((tm, d), lambda i, d1, d2: (i, 0)),
                      pl.BlockSpec(memory_space=pl.ANY)],
            out_specs=pl.BlockSpec(memory_space=pl.ANY),
            scratch_shapes=[pltpu.SemaphoreType.DMA(())]),
        out_shape=jax.ShapeDtypeStruct((n_rows, d), x.dtype),
        input_output_aliases={3: 0},
        compiler_params=_params("arbitrary"),
        name="moe_scatter_rows",
    )(dest1, dest2, x, zeros)


def _moe_group_kernel(te_ref, nu_ref, x_ref, g_ref, wg_ref, wu_ref, wd_ref, o_ref, hn_ref):
    i = pl.program_id(0)
    j = pl.program_id(1)
    used = i < nu_ref[0]

    @pl.when(j == 0)
    def _():
        o_ref[...] = jnp.zeros(o_ref.shape, F32)

    @pl.when(used & (j == 0))
    def _():
        hn_ref[...] = _rms(x_ref[...], g_ref[...]).astype(BF16)

    @pl.when(used)
    def _():
        hn = hn_ref[...]
        a = jnp.dot(hn, wg_ref[...], preferred_element_type=F32)
        u = jnp.dot(hn, wu_ref[...], preferred_element_type=F32)
        act = (a * jax.nn.sigmoid(a) * u).astype(BF16)
        o_ref[...] += jnp.dot(act, wd_ref[...], preferred_element_type=F32)


def moe_group_ffn(xs, g, tile_expert, n_used, wg, wu, wd, tm, tf):
    p, d = xs.shape
    f = wg.shape[2]
    nj = f // tf
    chunk = lambda i, j, te, nu: jnp.where(i < nu[0], j, nj - 1)
    return pl.pallas_call(
        _moe_group_kernel,
        grid_spec=pltpu.PrefetchScalarGridSpec(
            num_scalar_prefetch=2,
            grid=(p // tm, nj),
            in_specs=[pl.BlockSpec((tm, d), lambda i, j, te, nu: (i, 0)),
                      pl.BlockSpec((1, d), lambda i, j, te, nu: (0, 0)),
                      pl.BlockSpec((None, d, tf), lambda i, j, te, nu: (te[i], 0, chunk(i, j, te, nu))),
                      pl.BlockSpec((None, d, tf), lambda i, j, te, nu: (te[i], 0, chunk(i, j, te, nu))),
                      pl.BlockSpec((None, tf, d), lambda i, j, te, nu: (te[i], chunk(i, j, te, nu), 0))],
            out_specs=pl.BlockSpec((tm, d), lambda i, j, te, nu: (i, 0)),
            scratch_shapes=[pltpu.VMEM((tm, d), BF16)]),
        out_shape=jax.ShapeDtypeStruct((p, d), F32),
        compiler_params=_params("arbitrary", "arbitrary"),
        name="moe_group_ffn",
    )(tile_expert, n_used, xs, g.reshape(1, d), wg, wu, wd)


def _moe_combine_kernel(d1_ref, d2_ref, x_ref, gates_ref, fg_ref, ys_ref, o_ref, y1_ref, y2_ref, sem,
                        *, final_norm):
    tm = x_ref.shape[0]
    i = pl.program_id(0)

    def issue(tile, slot):
        base = tile * tm

        def body(r, carry):
            _row_copy(ys_ref, d1_ref[base + r], y1_ref.at[slot], r, sem.at[slot]).start()
            _row_copy(ys_ref, d2_ref[base + r], y2_ref.at[slot], r, sem.at[slot]).start()
            return carry

        lax.fori_loop(0, tm, body, 0, unroll=8)

    @pl.when(i == 0)
    def _():
        issue(0, 0)

    @pl.when(i + 1 < pl.num_programs(0))
    def _():
        issue(i + 1, (i + 1) % 2)

    slot = i % 2
    pltpu.make_async_copy(ys_ref.at[pl.ds(0, tm)], y1_ref.at[slot], sem.at[slot]).wait()
    pltpu.make_async_copy(ys_ref.at[pl.ds(0, tm)], y2_ref.at[slot], sem.at[slot]).wait()
    gates = gates_ref[...]
    out = x_ref[...] + gates[:, 0:1] * y1_ref[slot] + gates[:, 1:2] * y2_ref[slot]
    if final_norm:
        out = _rms(out, fg_ref[...])
    o_ref[...] = out


def moe_combine(x, gates, ys, dest1, dest2, tm, final_gain=None):
    t, d = x.shape
    fg = jnp.ones((1, d), F32) if final_gain is None else final_gain.reshape(1, d)
    return pl.pallas_call(
        functools.partial(_moe_combine_kernel, final_norm=final_gain is not None),
        grid_spec=pltpu.PrefetchScalarGridSpec(
            num_scalar_prefetch=2,
            grid=(t // tm,),
            in_specs=[pl.BlockSpec((tm, d), lambda i, d1, d2: (i, 0)),
                      pl.BlockSpec((tm, 128), lambda i, d1, d2: (i, 0)),
                      pl.BlockSpec((1, d), lambda i, d1, d2: (0, 0)),
                      pl.BlockSpec(memory_space=pl.ANY)],
            out_specs=pl.BlockSpec((tm, d), lambda i, d1, d2: (i, 0)),
            scratch_shapes=[pltpu.VMEM((2, tm, d), F32), pltpu.VMEM((2, tm, d), F32),
                            pltpu.SemaphoreType.DMA((2,))]),
        out_shape=jax.ShapeDtypeStruct((t, d), F32),
        compiler_params=_params("arbitrary"),
        name="moe_combine",
    )(dest1, dest2, x, gates, fg, ys)


def moe_ffn(x, g, wr_pad, wg, wu, wd, final_gain=None):
    t, d = x.shape
    ne = wg.shape[0]
    gates, route, counts = router(x, g, wr_pad, ROUTE_TM)
    counts = counts[0, :ne]
    padded = (counts + MOE_TILE - 1) // MOE_TILE * MOE_TILE
    ends = jnp.cumsum(padded)
    starts = ends - padded
    expert_ids = jnp.arange(ne, dtype=jnp.int32)[None, :]
    start_of = lambda e: jnp.sum(jnp.where(e[:, None] == expert_ids, starts[None, :], 0), axis=1)
    dest1 = (start_of(route[:, 0]) + route[:, 2]).astype(jnp.int32)
    dest2 = (start_of(route[:, 1]) + route[:, 3]).astype(jnp.int32)
    n_tiles = (2 * t) // MOE_TILE + ne
    tile_start = jnp.arange(n_tiles, dtype=jnp.int32) * MOE_TILE
    tile_expert = jnp.minimum(jnp.sum(tile_start[:, None] >= ends[None, :], axis=1), ne - 1).astype(jnp.int32)
    n_used = (ends[-1:] // MOE_TILE).astype(jnp.int32)
    xs = scatter_rows(x, dest1, dest2, n_tiles * MOE_TILE, ROUTE_TM)
    ys = moe_group_ffn(xs, g, tile_expert, n_used, wg, wu, wd, MOE_TILE, 512)
    return moe_combine(x, gates, ys, dest1, dest2, ROUTE_TM, final_gain)


def _kvnorm_kernel(c_ref, g_ref, o_ref, ot_ref):
    y = _rms(c_ref[0].astype(F32), g_ref[...])
    o_ref[0, 0] = y.astype(BF16)
    ot_ref[0, 0, :KV_RANK] = y.T.astype(BF16)
    ot_ref[0, 0, KV_RANK:] = jnp.ones((ONES_ROWS, DSA_QB), BF16)


def kv_latent_norm(p1, ccol, g):
    b, s, _ = p1.shape
    nc = s // DSA_QB
    out = jax.ShapeDtypeStruct((b, nc, DSA_QB, KV_RANK), BF16)
    out_t = jax.ShapeDtypeStruct((b, nc, KV_RANK + ONES_ROWS, DSA_QB), BF16)
    blk = pl.BlockSpec((1, 1, DSA_QB, KV_RANK), lambda bi, j: (bi, j, 0, 0))
    blk_t = pl.BlockSpec((1, 1, KV_RANK + ONES_ROWS, DSA_QB), lambda bi, j: (bi, j, 0, 0))
    return pl.pallas_call(
        _kvnorm_kernel,
        grid=(b, nc),
        in_specs=[pl.BlockSpec((1, DSA_QB, KV_RANK), lambda bi, j: (bi, j, ccol)),
                  pl.BlockSpec((1, KV_RANK), lambda bi, j: (0, 0))],
        out_specs=[blk, blk_t],
        out_shape=[out, out_t],
        compiler_params=_params("parallel", "parallel"),
        name="kv_latent_norm",
    )(p1, g.reshape(1, KV_RANK))


def _select_kernel(qi_ref, kall_ref, kq_ref, mask_ref, keys_ref, hi_ref, lo_ref, jcut_ref, *,
                   nchunks_total):
    qb = DSA_QB
    i = pl.program_id(1)
    nch = i + 1
    shape = (qb, qb)
    row = lax.broadcasted_iota(jnp.int32, shape, 0)
    qpos = i * qb + lax.broadcasted_iota(jnp.int32, shape, 1)

    wt = kq_ref[0].astype(F32).T
    wscale = (N_IDX_HEADS ** -0.5) * (IDX_DIM ** -0.5)

    def score_chunk(j, carry):
        start = pl.multiple_of(j * qb, qb)
        kc = kall_ref[0, pl.ds(start, qb), :][:, :IDX_DIM]
        acc = jnp.zeros(shape, F32)
        for h in range(N_IDX_HEADS):
            qh = qi_ref[0, :, h * IDX_DIM:(h + 1) * IDX_DIM]
            s = lax.dot_general(kc, qh, _NT, preferred_element_type=F32)
            acc = acc + jnp.maximum(s, 0.0) * (wt[IDX_DIM + h:IDX_DIM + h + 1, :] * wscale)
        acc = jnp.where(j * qb + row <= qpos, acc, NEG_INF)
        acc = jnp.where(acc == 0.0, 0.0, acc)
        bits = pltpu.bitcast(acc, jnp.int32)
        key = bits ^ ((bits >> 31) & 0x7FFFFFFF)
        keys_ref[j] = key
        hi_ref[j] = (key >> 16).astype(jnp.int16)
        return carry

    lax.fori_loop(0, nch, score_chunk, 0)

    def count(pred):
        def body(j, c):
            hit = jnp.where(pred(keys_ref[j], j), 1, 0)
            return c + jnp.sum(hit.reshape(qb // 8, 8, qb), axis=0)
        c = lax.fori_loop(0, nch, body, jnp.zeros((8, qb), jnp.int32))
        return jnp.sum(c, axis=0, keepdims=True)

    def count16(ref, pred):
        def body(j, c):
            hit = jnp.where(pred(ref[j]), jnp.int16(1), jnp.int16(0)).reshape(qb // 16, 16, qb)
            for g in range(qb // 16):
                c = c + hit[g]
            return c
        c = lax.fori_loop(0, nch, body, jnp.zeros((16, qb), jnp.int16))
        return jnp.sum(c.astype(jnp.int32), axis=0, keepdims=True)

    def kth_largest16(ref, rank):
        def ge(cand):
            c16 = cand.astype(jnp.int16)
            return count16(ref, lambda k: k >= c16) >= rank
        x = jnp.where(ge(jnp.zeros((1, qb), jnp.int32)), 0, -(1 << 15))

        def bit_body(t, x):
            cand = x + jnp.left_shift(jnp.int32(1), 14 - t)
            return jnp.where(ge(cand), cand, x)
        return lax.fori_loop(0, 15, bit_body, x)

    thr_hi = kth_largest16(hi_ref, TOPK)
    thr_hi16 = thr_hi.astype(jnp.int16)
    rank_lo = TOPK - count16(hi_ref, lambda k: k > thr_hi16)

    def low_half_chunk(j, carry):
        key = keys_ref[j]
        lo = (key & 0xFFFF) - (1 << 15)
        lo_ref[j] = jnp.where((key >> 16) == thr_hi, lo, -(1 << 15)).astype(jnp.int16)
        return carry

    lax.fori_loop(0, nch, low_half_chunk, 0)
    thr = thr_hi * (1 << 16) + (kth_largest16(lo_ref, rank_lo) + (1 << 15))

    n_gt = count(lambda key, j: key > thr)
    n_ge = count(lambda key, j: key >= thr)
    need = TOPK - n_gt

    jcut_ref[...] = jnp.full((1, qb), 1 << 30, jnp.int32)

    @pl.when(jnp.max(n_ge) > TOPK)
    def _():
        def idx_body(t, x):
            cand = x + jnp.left_shift(jnp.int32(1), 10 - t)
            below = count(lambda key, j: (key == thr) & (j * qb + row <= cand - 1))
            return jnp.where(below < need, cand, x)
        jcut_ref[...] = lax.fori_loop(0, 11, idx_body, jnp.zeros((1, qb), jnp.int32))

    jcut = jcut_ref[...]

    def write_chunk(j, carry):
        key = keys_ref[j]
        spos = j * qb + row
        sel = (key > thr) | ((key == thr) & (spos <= jcut))
        sel = sel & (spos <= qpos)
        start = pl.multiple_of(j * qb, qb)
        mask_ref[0, pl.ds(start, qb), :] = jnp.where(sel, 0.0, NEG_INF)
        return carry

    lax.fori_loop(0, nch, write_chunk, 0)

    def fill_chunk(j, carry):
        start = pl.multiple_of(j * qb, qb)
        mask_ref[0, pl.ds(start, qb), :] = jnp.full(shape, NEG_INF, F32)
        return carry

    lax.fori_loop(nch, nchunks_total, fill_chunk, 0)


def dsa_select(p1, qicol, kwcol):
    b, s, _ = p1.shape
    nq = s // DSA_QB
    return pl.pallas_call(
        functools.partial(_select_kernel, nchunks_total=nq),
        grid=(b, nq),
        in_specs=[pl.BlockSpec((1, DSA_QB, N_IDX_HEADS * IDX_DIM), lambda bi, i: (bi, i, qicol)),
                  pl.BlockSpec((1, s, 128), lambda bi, i: (bi, 0, kwcol)),
                  pl.BlockSpec((1, DSA_QB, 128), lambda bi, i: (bi, i, kwcol))],
        out_specs=pl.BlockSpec((1, s, DSA_QB), lambda bi, i: (bi, 0, i)),
        out_shape=jax.ShapeDtypeStruct((b, s, s), F32),
        scratch_shapes=[pltpu.VMEM((nq, DSA_QB, DSA_QB), jnp.int32),
                        pltpu.VMEM((nq, DSA_QB, DSA_QB), jnp.int16),
                        pltpu.VMEM((nq, DSA_QB, DSA_QB), jnp.int16),
                        pltpu.VMEM((1, DSA_QB), jnp.int32)],
        compiler_params=_params("parallel", "arbitrary"),
        name="dsa_select",
    )(p1, p1, p1)


def _dsa_attn_kernel(q_ref, c_ref, ct_ref, mask_ref, tz_ref, wuk_ref, wuvt_ref, o_ref,
                     ql_ref, ot_ref, acc_ref):
    qb = DSA_QB
    hg = DSA_HEAD_GROUP
    width = hg * qb
    i = pl.program_id(1)

    for h in range(N_HEADS):
        qh = q_ref[0, :, h * HEAD_DIM:(h + 1) * HEAD_DIM]
        ql = lax.dot_general(wuk_ref[h], qh, _NT, preferred_element_type=F32)
        ql_ref[h // hg, :, (h % hg) * qb:(h % hg + 1) * qb] = (
            ql * (HEAD_DIM ** -0.5 * LOG2_E)).astype(BF16)

    def group_body(g, carry):
        acc_ref[...] = jnp.zeros((KV_RANK, width), F32)

        def chunk(j, ml):
            m, l = ml
            s = jnp.dot(c_ref[0, j], ql_ref[g], preferred_element_type=F32)
            d = 2 * (i - j)
            start = pl.multiple_of(j * qb, qb)
            mk = mask_ref[0, pl.ds(start, qb), :]
            tiles = []
            for hh in range(hg):
                h = g * hg + hh
                t_diag = tz_ref[d, h]
                tiles.append(mk + jnp.concatenate(
                    [jnp.concatenate([t_diag, tz_ref[d + 1, h]], axis=1),
                     jnp.concatenate([tz_ref[jnp.maximum(d - 1, 0), h], t_diag], axis=1)], axis=0))
            s = s + jnp.concatenate(tiles, axis=1)
            m_new = jnp.maximum(m, jnp.max(s, axis=0, keepdims=True))
            p = jnp.exp2(s - m_new)
            alpha = jnp.exp2(m - m_new)
            pv = jnp.dot(ct_ref[0, j], p.astype(BF16), preferred_element_type=F32)
            l = alpha * l + pv[KV_RANK:KV_RANK + 1]
            acc_ref[...] = alpha * acc_ref[...] + pv[:KV_RANK]
            return m_new, l

        init = (jnp.full((1, width), M_INIT, F32), jnp.zeros((1, width), F32))
        _, l = lax.fori_loop(0, i + 1, chunk, init)
        o_lat = (acc_ref[...] / l).astype(BF16)
        for hh in range(hg):
            h = g * hg + hh
            ot_ref[h] = jnp.dot(wuvt_ref[h], o_lat[:, hh * qb:(hh + 1) * qb],
                                preferred_element_type=F32)
        return carry

    lax.fori_loop(0, N_HEADS // hg, group_body, 0)
    o_ref[0] = ot_ref[...].reshape(ATTN_W, qb).T.astype(o_ref.dtype)


def dsa_attention(p1, qcol, c, ct, mask, tz, wuk, wuvt):
    b, s, _ = p1.shape
    nq = s // DSA_QB
    const = lambda shape: pl.BlockSpec(shape, lambda bi, i: (0,) * len(shape))
    return pl.pallas_call(
        _dsa_attn_kernel,
        grid=(b, nq),
        in_specs=[pl.BlockSpec((1, DSA_QB, ATTN_W), lambda bi, i: (bi, i, qcol)),
                  pl.BlockSpec((1, nq, DSA_QB, KV_RANK), lambda bi, i: (bi, 0, 0, 0)),
                  pl.BlockSpec((1, nq, KV_RANK + ONES_ROWS, DSA_QB), lambda bi, i: (bi, 0, 0, 0)),
                  pl.BlockSpec((1, s, DSA_QB), lambda bi, i: (bi, 0, i)),
                  pl.BlockSpec(tz.shape, lambda bi, i: (0, 0, 0, 0), pipeline_mode=pl.Buffered(1)),
                  const(wuk.shape), const(wuvt.shape)],
        out_specs=pl.BlockSpec((1, DSA_QB, ATTN_W), lambda bi, i: (bi, i, 0)),
        out_shape=jax.ShapeDtypeStruct((b, s, ATTN_W), BF16),
        scratch_shapes=[pltpu.VMEM((N_HEADS // DSA_HEAD_GROUP, KV_RANK, DSA_HEAD_GROUP * DSA_QB), BF16),
                        pltpu.VMEM((N_HEADS, HEAD_DIM, DSA_QB), F32),
                        pltpu.VMEM((KV_RANK, DSA_HEAD_GROUP * DSA_QB), F32)],
        compiler_params=_params("parallel", "arbitrary"),
        name="dsa_attention",
    )(p1, c, ct, mask, tz, wuk, wuvt)


def _rmsnorm_kernel(x_ref, g_ref, o_ref):
    o_ref[...] = _rms(x_ref[...], g_ref[...])


def rmsnorm(x, g, tm):
    t, d = x.shape
    return pl.pallas_call(
        _rmsnorm_kernel,
        grid=(t // tm,),
        in_specs=[pl.BlockSpec((tm, d), lambda i: (i, 0)), pl.BlockSpec((1, d), lambda i: (0, 0))],
        out_specs=pl.BlockSpec((tm, d), lambda i: (i, 0)),
        out_shape=jax.ShapeDtypeStruct((t, d), F32),
        compiler_params=_params("parallel"),
        name="final_rmsnorm",
    )(x, g.reshape(1, d))


def _rel_bucket(dist):
    n = np.maximum(dist, 0)
    max_exact = REL_BUCKETS // 2
    nf = np.maximum(n, 1).astype(np.float32)
    large = max_exact + (np.log(nf / np.float32(max_exact)) / np.float32(math.log(REL_MAX_DIST / max_exact))
                         * np.float32(REL_BUCKETS - max_exact)).astype(np.int32)
    large = np.minimum(large, REL_BUCKETS - 1)
    return np.where(n < max_exact, n, large)


def _bias_table_kernel(bkt_ref, rb_ref, o_ref):
    bkt = bkt_ref[0]
    for h in range(N_HEADS):
        acc = jnp.full(bkt.shape, NEG_INF, F32)
        for k in range(REL_BUCKETS):
            acc = jnp.where(bkt == k, rb_ref[k, h], acc)
        o_ref[0, h] = acc


def bias_table(bkt, rel_bias):
    n, r, c = bkt.shape
    return pl.pallas_call(
        _bias_table_kernel,
        grid=(n,),
        in_specs=[pl.BlockSpec((1, r, c), lambda i: (i, 0, 0)),
                  pl.BlockSpec(memory_space=pltpu.SMEM)],
        out_specs=pl.BlockSpec((1, N_HEADS, r, c), lambda i: (i, 0, 0, 0)),
        out_shape=jax.ShapeDtypeStruct((n, N_HEADS, r, c), F32),
        compiler_params=_params("parallel"),
        name="bias_table",
    )(bkt, rel_bias.astype(F32))


def _dilated_buckets(window, dilation):
    qi = np.arange(BLOCK)[:, None]
    ki = np.arange(2 * BLOCK)[None, :]
    rel = qi + BLOCK - ki
    bkt = _rel_bucket(rel * dilation)
    band = (rel >= 0) & (rel <= window // dilation)
    return np.stack([np.where(band, bkt, -1),
                     np.where(band & (ki >= BLOCK), bkt, -1)]).astype(np.int32)


def _dsa_buckets(s):
    nd = s // BLOCK
    key = np.arange(BLOCK)[:, None]
    qry = np.arange(BLOCK)[None, :]
    dist = np.arange(nd)[:, None, None] * BLOCK + (qry - key)[None]
    return _rel_bucket(dist).astype(np.int32)


def kernel(x, mem, rel_bias, mem_norm, final_norm, mixer_norm, ffn_norm, w_mem_kv, w_out,
           even_w_in, even_w_gate, even_w_up, even_w_down,
           odd_w_in, odd_kv_norm, odd_w_uk, odd_w_uv,
           odd_w_router, odd_w_gate, odd_w_up, odd_w_down):
    b, s, d = x.shape
    t = b * s
    depth = mixer_norm.shape[0]
    xt = x.reshape(t, d)
    mem2 = mem.reshape(b * N_MEM, d)
    dil_tables = [bias_table(_dilated_buckets(w, dl), rel_bias) for w, dl in DIL_PAIRS]
    tz = bias_table(_dsa_buckets(s), rel_bias * LOG2_E)

    for i in range(depth):
        j = i // 2
        kvm = norm_matmul(mem2, mem_norm, w_mem_kv[i].astype(BF16), BF16, 512, 512)
        kvm = kvm.reshape(b, N_MEM, 2 * CROSS_WIDTH)
        wo = w_out[i].astype(BF16)
        if i % 2 == 0:
            w_in = even_w_in[j].astype(BF16)
            proj = even_in_projection(xt, mixer_norm[i], w_in[:, :5 * ATTN_W], b, s, 512)
            qc = norm_matmul(xt, mixer_norm[i], w_in[:, 5 * ATTN_W:], BF16, 512, CROSS_WIDTH)
            outs, lses = [], []
            for grp in range(len(DIL_PAIRS)):
                o, lse = dilated_branch(proj[grp], proj[3 + grp], proj[6 + grp], dil_tables[grp])
                outs.append(o)
                lses.append(lse)
            mix = combine_groups(outs, lses, 256).reshape(t, ATTN_W)
            cross = cross_attention(qc.reshape(b, s, CROSS_WIDTH), 0, kvm, 512)
        else:
            o_q, o_c, o_qi, o_ki, o_wi, o_qc = 0, 1024, 1280, 1792, 1856, 1864
            w = odd_w_in[j]
            pad = jnp.zeros((d, 128 - IDX_DIM - N_IDX_HEADS), w.dtype)
            w_in = jnp.concatenate(
                [w[:, o_q:o_c], w[:, o_qi:o_ki], w[:, o_c:o_qi], w[:, o_qc:],
                 w[:, o_ki:o_wi], w[:, o_wi:o_qc], pad], axis=1).astype(BF16)
            p1 = norm_matmul(xt, mixer_norm[i], w_in, BF16, 512, w_in.shape[1])
            p1 = p1.reshape(b, s, w_in.shape[1])
            c, ct = kv_latent_norm(p1, 6, odd_kv_norm[j])
            mask = dsa_select(p1, 2, 16)
            wuk = jnp.transpose(odd_w_uk[j], (1, 0, 2)).astype(BF16)
            wuvt = jnp.transpose(odd_w_uv[j], (1, 2, 0)).astype(BF16)
            mix = dsa_attention(p1, 0, c, ct, mask, tz, wuk, wuvt).reshape(t, ATTN_W)
            cross = cross_attention(p1, 7, kvm, 512)
        xt = out_projection(mix, cross.reshape(t, CROSS_WIDTH), wo[:ATTN_W], wo[ATTN_W:], xt, 512)
        if i % 2 == 0:
            xt = swiglu_ffn(xt, ffn_norm[i], even_w_gate[j].astype(BF16), even_w_up[j].astype(BF16),
                            even_w_down[j].astype(BF16), 512, 1408)
        else:
            wr = jnp.pad(odd_w_router[j], ((0, 0), (0, 128 - N_EXPERTS)))
            xt = moe_ffn(xt, ffn_norm[i], wr, odd_w_gate[j].astype(BF16), odd_w_up[j].astype(BF16),
                         odd_w_down[j].astype(BF16), final_norm if i == depth - 1 else None)
    if depth % 2 == 1:
        xt = rmsnorm(xt, final_norm, 512)
    return xt.reshape(b, s, d)
```

```python
import functools
import math

import jax
import jax.numpy as jnp
import numpy as np
from jax import lax
from jax.experimental import pallas as pl
from jax.experimental.pallas import tpu as pltpu

D_MODEL = 1024
N_HEADS = 16
HEAD_DIM = 64
ATTN_W = N_HEADS * HEAD_DIM
DIL_PAIRS = ((128, 1), (512, 4), (2048, 16))
BLOCK = 128
LANES = 128
KV_RANK = 256
N_IDX_HEADS = 8
IDX_DIM = 64
TOPK = 256
N_MEM = 256
N_CROSS_HEADS = 4
CROSS_WIDTH = 256
REL_BUCKETS = 32
REL_MAX_DIST = 2048
N_EXPERTS = 8
RMS_EPS = 1e-6
NEG_INF = -1e30
M_INIT = -5e29
LOG2_E = math.log2(math.e)

F32 = jnp.float32
BF16 = jnp.bfloat16
V7X_VMEM_LIMIT = 56 * 1024 * 1024
DSA_QB = 256
DSA_HEAD_GROUP = 8
ONES_ROWS = 16
INT_MIN = -(2 ** 31)
KEY_OF_NEG_INF = (0xFF800000 ^ 0x7FFFFFFF) - (1 << 32)
MOE_TILE = 512
ROUTE_TM = 256

_NT = (((1,), (1,)), ((), ()))


def _params(*sem):
    return pltpu.CompilerParams(dimension_semantics=sem, vmem_limit_bytes=V7X_VMEM_LIMIT)


def _rms(x, g):
    return x * lax.rsqrt(jnp.mean(x * x, axis=-1, keepdims=True) + RMS_EPS) * g


def _norm_mm_kernel(x_ref, g_ref, w_ref, o_ref, xn_ref):
    @pl.when(pl.program_id(1) == 0)
    def _():
        xn_ref[...] = _rms(x_ref[...].astype(F32), g_ref[...]).astype(BF16)

    o_ref[...] = jnp.dot(xn_ref[...], w_ref[...], preferred_element_type=F32).astype(o_ref.dtype)


def norm_matmul(x, g, w, out_dtype, tm, tn):
    t, k = x.shape
    n = w.shape[1]
    return pl.pallas_call(
        _norm_mm_kernel,
        grid=(t // tm, n // tn),
        in_specs=[pl.BlockSpec((tm, k), lambda i, j: (i, 0)),
                  pl.BlockSpec((1, k), lambda i, j: (0, 0)),
                  pl.BlockSpec((k, tn), lambda i, j: (0, j))],
        out_specs=pl.BlockSpec((tm, tn), lambda i, j: (i, j)),
        out_shape=jax.ShapeDtypeStruct((t, n), out_dtype),
        scratch_shapes=[pltpu.VMEM((tm, k), BF16)],
        compiler_params=_params("parallel", "arbitrary"),
        name="norm_matmul",
    )(x, g.reshape(1, k), w)


def _even_proj_kernel(x_ref, g_ref, w_ref, q1_ref, q4_ref, q16_ref, k1_ref, k4_ref, k16_ref,
                      v1_ref, v4_ref, v16_ref, xn_ref, y_ref):
    j = pl.program_id(1)
    tm = x_ref.shape[0]

    @pl.when(j == 0)
    def _():
        xn_ref[...] = _rms(x_ref[...], g_ref[...]).astype(BF16)

    y = jnp.dot(xn_ref[...], w_ref[...], preferred_element_type=F32)
    n_lane_blocks = y.shape[1] // LANES
    for c in range(n_lane_blocks):
        y_ref[c] = y[:, c * LANES:(c + 1) * LANES]

    def put(dst_ref, d):
        if d == 1:
            dst_ref[0, 0] = y.astype(BF16)
            return
        for r in range(d):
            for c in range(n_lane_blocks):
                rows = y_ref[c, pl.ds(r, tm // d, stride=d), :]
                dst_ref[0, r, :, c * LANES:(c + 1) * LANES] = rows.astype(BF16)

    column_dsts = (((q1_ref, 1),), ((q4_ref, 4),), ((q16_ref, 16),),
                   ((k1_ref, 1), (k4_ref, 4), (k16_ref, 16)),
                   ((v1_ref, 1), (v4_ref, 4), (v16_ref, 16)))
    for col, dsts in enumerate(column_dsts):
        @pl.when(j == col)
        def _(dsts=dsts):
            for ref, d in dsts:
                put(ref, d)


def even_in_projection(x, g, w, b, s, tm):
    t, kdim = x.shape
    per_b = s // tm
    dils = [dl for _, dl in DIL_PAIRS]
    layouts = dils + dils + dils
    spec = lambda dl: pl.BlockSpec((1, dl, tm // dl, ATTN_W), lambda i, j: (i // per_b, 0, i % per_b, 0))
    return pl.pallas_call(
        _even_proj_kernel,
        grid=(t // tm, 5),
        in_specs=[pl.BlockSpec((tm, kdim), lambda i, j: (i, 0)),
                  pl.BlockSpec((1, kdim), lambda i, j: (0, 0)),
                  pl.BlockSpec((kdim, ATTN_W), lambda i, j: (0, j))],
        out_specs=[spec(dl) for dl in layouts],
        out_shape=[jax.ShapeDtypeStruct((b, dl, s // dl, ATTN_W), BF16) for dl in layouts],
        scratch_shapes=[pltpu.VMEM((tm, kdim), BF16), pltpu.VMEM((ATTN_W // LANES, tm, LANES), F32)],
        compiler_params=_params("parallel", "arbitrary"),
        name="even_in_projection",
    )(x, g.reshape(1, kdim), w)


def _dil_kernel(q_ref, kp_ref, kc_ref, vp_ref, vc_ref, tab_ref, o_ref, lse_ref, s_ref, p_ref, m_ref):
    first = (pl.program_id(2) == 0).astype(jnp.int32)
    pair = 2 * HEAD_DIM
    lo_q = lax.broadcasted_iota(jnp.int32, (BLOCK, pair), 1) < HEAD_DIM
    lo_k = lax.broadcasted_iota(jnp.int32, (2 * BLOCK, pair), 1) < HEAD_DIM
    ones_bd = jnp.concatenate([jnp.where(lo_k, 1.0, 0.0), jnp.where(lo_k, 0.0, 1.0)], axis=0).astype(BF16)
    scale = jnp.asarray(HEAD_DIM ** -0.5, BF16)
    zero = jnp.zeros((), BF16)
    n_pairs = N_HEADS // 2
    for p in range(n_pairs):
        cols = slice(p * pair, (p + 1) * pair)
        q = q_ref[0, 0, :, cols] * scale
        k = jnp.concatenate([kp_ref[0, 0, :, cols], kc_ref[0, 0, :, cols]], axis=0)
        q_ab = jnp.concatenate([jnp.where(lo_q, q, zero), jnp.where(lo_q, zero, q)], axis=0)
        s_ab = lax.dot_general(q_ab, k, _NT, preferred_element_type=F32)
        s_ref[2 * p] = s_ab[:BLOCK] + tab_ref[first, 2 * p]
        s_ref[2 * p + 1] = s_ab[BLOCK:] + tab_ref[first, 2 * p + 1]
    for p in range(n_pairs):
        sa = s_ref[2 * p]
        sb = s_ref[2 * p + 1]
        ma = jnp.max(sa, axis=-1, keepdims=True)
        mb = jnp.max(sb, axis=-1, keepdims=True)
        p_ref[p, :, :2 * BLOCK] = jnp.exp(sa - ma).astype(BF16)
        p_ref[p, :, 2 * BLOCK:] = jnp.exp(sb - mb).astype(BF16)
        m_ref[p] = jnp.where(lo_q, ma, mb)
    for p in range(n_pairs):
        cols = slice(p * pair, (p + 1) * pair)
        v = jnp.concatenate([vp_ref[0, 0, :, cols], vc_ref[0, 0, :, cols]], axis=0)
        v_bd = jnp.concatenate([jnp.where(lo_k, v, zero), jnp.where(lo_k, zero, v)], axis=0)
        ol = jnp.dot(p_ref[p], jnp.concatenate([v_bd, ones_bd], axis=1), preferred_element_type=F32)
        o, l = ol[:, :pair], ol[:, pair:]
        o_ref[0, 0, :, cols] = o / l
        lse_ref[0, 0, :, cols] = m_ref[p] + jnp.log(l)


def dilated_branch(q, k, v, table):
    b, dilation, ln, _ = q.shape
    nb = ln // BLOCK
    blk = (1, 1, BLOCK, ATTN_W)
    cur = pl.BlockSpec(blk, lambda bi, r, n: (bi, r, n, 0))
    prev = pl.BlockSpec(blk, lambda bi, r, n: (bi, r, jnp.maximum(n - 1, 0), 0))
    out = jax.ShapeDtypeStruct(q.shape, F32)
    return pl.pallas_call(
        _dil_kernel,
        grid=(b, dilation, nb),
        in_specs=[cur, prev, cur, prev, cur,
                  pl.BlockSpec(table.shape, lambda bi, r, n: (0, 0, 0, 0))],
        out_specs=[cur, cur],
        out_shape=[out, out],
        scratch_shapes=[pltpu.VMEM((N_HEADS, BLOCK, 2 * BLOCK), F32),
                        pltpu.VMEM((N_HEADS // 2, BLOCK, 4 * BLOCK), BF16),
                        pltpu.VMEM((N_HEADS // 2, BLOCK, 2 * HEAD_DIM), F32)],
        compiler_params=_params("parallel", "parallel", "arbitrary"),
        name=f"dilated_attn_d{dilation}",
    )(q, k, k, v, v, table)


def _combine_kernel(o1, o4, o16, l1, l4, l16, out_ref, so4, sl4, so16, sl16):
    tm = out_ref.shape[1]
    n_lane_blocks = out_ref.shape[2] // LANES

    def to_token_order(src_ref, dst_ref, d):
        for r in range(d):
            for c in range(n_lane_blocks):
                dst_ref[c, pl.ds(r, tm // d, stride=d), :] = src_ref[0, r, :, c * LANES:(c + 1) * LANES]

    to_token_order(o4, so4, 4)
    to_token_order(l4, sl4, 4)
    to_token_order(o16, so16, 16)
    to_token_order(l16, sl16, 16)
    for c in range(n_lane_blocks):
        cols = slice(c * LANES, (c + 1) * LANES)
        a1, a2, a3 = l1[0, 0, :, cols], sl4[c], sl16[c]
        m = jnp.maximum(jnp.maximum(a1, a2), a3)
        w1, w2, w3 = jnp.exp(a1 - m), jnp.exp(a2 - m), jnp.exp(a3 - m)
        num = w1 * o1[0, 0, :, cols] + w2 * so4[c] + w3 * so16[c]
        out_ref[0, :, cols] = (num / (w1 + w2 + w3)).astype(out_ref.dtype)


def combine_groups(outs, lses, tm):
    b, _, s, w = outs[0].shape
    spec = lambda dl: pl.BlockSpec((1, dl, tm // dl, w), lambda bi, i: (bi, 0, i, 0))
    specs = [spec(o.shape[1]) for o in outs]
    return pl.pallas_call(
        _combine_kernel,
        grid=(b, s // tm),
        in_specs=specs + specs,
        out_specs=pl.BlockSpec((1, tm, w), lambda bi, i: (bi, i, 0)),
        out_shape=jax.ShapeDtypeStruct((b, s, w), BF16),
        scratch_shapes=[pltpu.VMEM((w // LANES, tm, LANES), F32)] * 4,
        compiler_params=_params("parallel", "parallel"),
        name="combine_groups",
    )(*outs, *lses)


def _cross_kernel(q_ref, kv_ref, o_ref):
    for h in range(N_CROSS_HEADS):
        sl = slice(h * HEAD_DIM, (h + 1) * HEAD_DIM)
        vsl = slice(CROSS_WIDTH + h * HEAD_DIM, CROSS_WIDTH + (h + 1) * HEAD_DIM)
        s = lax.dot_general(q_ref[0, :, sl], kv_ref[0, :, sl], _NT,
                            preferred_element_type=F32) * (HEAD_DIM ** -0.5)
        m = jnp.max(s, axis=-1, keepdims=True)
        p = jnp.exp(s - m)
        l = jnp.sum(p, axis=-1, keepdims=True)
        o = jnp.dot(p.astype(BF16), kv_ref[0, :, vsl], preferred_element_type=F32)
        o_ref[0, :, sl] = (o / l).astype(o_ref.dtype)


def cross_attention(qsrc, qcol, kv, tm):
    b, s, _ = qsrc.shape
    return pl.pallas_call(
        _cross_kernel,
        grid=(b, s // tm),
        in_specs=[pl.BlockSpec((1, tm, CROSS_WIDTH), lambda bi, i: (bi, i, qcol)),
                  pl.BlockSpec((1, N_MEM, 2 * CROSS_WIDTH), lambda bi, i: (bi, 0, 0))],
        out_specs=pl.BlockSpec((1, tm, CROSS_WIDTH), lambda bi, i: (bi, i, 0)),
        out_shape=jax.ShapeDtypeStruct((b, s, CROSS_WIDTH), BF16),
        compiler_params=_params("parallel", "parallel"),
        name="cross_attn",
    )(qsrc, kv)


def _outproj_kernel(mix_ref, cr_ref, wa_ref, wb_ref, x_ref, o_ref):
    acc = jnp.dot(mix_ref[...], wa_ref[...], preferred_element_type=F32)
    acc = acc + jnp.dot(cr_ref[...], wb_ref[...], preferred_element_type=F32)
    o_ref[...] = x_ref[...] + acc


def out_projection(mix, cross, wa, wb, x, tm):
    t, d = x.shape
    return pl.pallas_call(
        _outproj_kernel,
        grid=(t // tm,),
        in_specs=[pl.BlockSpec((tm, ATTN_W), lambda i: (i, 0)),
                  pl.BlockSpec((tm, CROSS_WIDTH), lambda i: (i, 0)),
                  pl.BlockSpec(wa.shape, lambda i: (0, 0)),
                  pl.BlockSpec(wb.shape, lambda i: (0, 0)),
                  pl.BlockSpec((tm, d), lambda i: (i, 0))],
        out_specs=pl.BlockSpec((tm, d), lambda i: (i, 0)),
        out_shape=jax.ShapeDtypeStruct((t, d), F32),
        compiler_params=_params("parallel"),
        name="out_projection",
    )(mix, cross, wa, wb, x)


def _swiglu_kernel(x_ref, g_ref, wg_ref, wu_ref, wd_ref, o_ref, hn_ref):
    @pl.when(pl.program_id(1) == 0)
    def _():
        x = x_ref[...]
        hn_ref[...] = _rms(x, g_ref[...]).astype(BF16)
        o_ref[...] = x

    hn = hn_ref[...]
    a = jnp.dot(hn, wg_ref[...], preferred_element_type=F32)
    u = jnp.dot(hn, wu_ref[...], preferred_element_type=F32)
    act = (a * jax.nn.sigmoid(a) * u).astype(BF16)
    o_ref[...] += jnp.dot(act, wd_ref[...], preferred_element_type=F32)


def swiglu_ffn(x, g, wg, wu, wd, tm, tf):
    t, d = x.shape
    f = wg.shape[1]
    return pl.pallas_call(
        _swiglu_kernel,
        grid=(t // tm, f // tf),
        in_specs=[pl.BlockSpec((tm, d), lambda i, j: (i, 0)),
                  pl.BlockSpec((1, d), lambda i, j: (0, 0)),
                  pl.BlockSpec((d, tf), lambda i, j: (0, j)),
                  pl.BlockSpec((d, tf), lambda i, j: (0, j)),
                  pl.BlockSpec((tf, d), lambda i, j: (j, 0))],
        out_specs=pl.BlockSpec((tm, d), lambda i, j: (i, 0)),
        out_shape=jax.ShapeDtypeStruct((t, d), F32),
        scratch_shapes=[pltpu.VMEM((tm, d), BF16)],
        compiler_params=_params("parallel", "arbitrary"),
        name="swiglu_ffn",
    )(x, g.reshape(1, d), wg, wu, wd)


def _router_kernel(x_ref, g_ref, wr_ref, gates_ref, route_ref, counts_ref, run_ref):
    @pl.when(pl.program_id(0) == 0)
    def _():
        run_ref[...] = jnp.zeros(run_ref.shape, F32)

    hn = _rms(x_ref[...], g_ref[...])
    logits = jnp.dot(hn, wr_ref[...], preferred_element_type=F32,
                     precision=lax.Precision.HIGHEST)
    tm = logits.shape[0]
    lane = lax.broadcasted_iota(jnp.int32, logits.shape, 1)
    lg = jnp.where(lane < N_EXPERTS, logits, -jnp.inf)
    m1 = jnp.max(lg, axis=-1, keepdims=True)
    i1 = jnp.min(jnp.where(lg == m1, lane, 128), axis=-1, keepdims=True)
    lg2 = jnp.where(lane == i1, -jnp.inf, lg)
    m2 = jnp.max(lg2, axis=-1, keepdims=True)
    i2 = jnp.min(jnp.where(lg2 == m2, lane, 128), axis=-1, keepdims=True)
    e = jnp.exp(m2 - m1)
    gates_ref[...] = jnp.where(lane == 0, 1.0 / (1.0 + e), jnp.where(lane == 1, e / (1.0 + e), 0.0))

    assign = jnp.where((lane == i1) | (lane == i2), 1.0, 0.0)
    r = lax.broadcasted_iota(jnp.int32, (tm, tm), 0)
    c = lax.broadcasted_iota(jnp.int32, (tm, tm), 1)
    lower = jnp.where(r > c, 1.0, 0.0).astype(BF16)
    before = jnp.dot(lower, assign.astype(BF16), preferred_element_type=F32) + run_ref[...]
    rank1 = jnp.sum(jnp.where(lane == i1, before, 0.0), axis=-1, keepdims=True).astype(jnp.int32)
    rank2 = jnp.sum(jnp.where(lane == i2, before, 0.0), axis=-1, keepdims=True).astype(jnp.int32)
    route_ref[...] = jnp.where(lane == 0, i1, jnp.where(lane == 1, i2, jnp.where(
        lane == 2, rank1, jnp.where(lane == 3, rank2, 0))))
    run = run_ref[...] + jnp.sum(assign, axis=0, keepdims=True)
    run_ref[...] = run
    counts_ref[...] = run.astype(jnp.int32)


def router(x, g, wr_pad, tm):
    t, d = x.shape
    return pl.pallas_call(
        _router_kernel,
        grid=(t // tm,),
        in_specs=[pl.BlockSpec((tm, d), lambda i: (i, 0)),
                  pl.BlockSpec((1, d), lambda i: (0, 0)),
                  pl.BlockSpec((d, 128), lambda i: (0, 0))],
        out_specs=[pl.BlockSpec((tm, 128), lambda i: (i, 0)),
                   pl.BlockSpec((tm, 128), lambda i: (i, 0)),
                   pl.BlockSpec((1, 128), lambda i: (0, 0))],
        out_shape=[jax.ShapeDtypeStruct((t, 128), F32),
                   jax.ShapeDtypeStruct((t, 128), jnp.int32),
                   jax.ShapeDtypeStruct((1, 128), jnp.int32)],
        scratch_shapes=[pltpu.VMEM((1, 128), F32)],
        compiler_params=_params("arbitrary"),
        name="router",
    )(x, g.reshape(1, d), wr_pad)


def _row_copy(src, src_row, dst, dst_row, sem):
    return pltpu.make_async_copy(src.at[pl.ds(src_row, 1)], dst.at[pl.ds(dst_row, 1)], sem)


def _scatter_rows_kernel(d1_ref, d2_ref, x_ref, xs_in_ref, xs_ref, sem):
    del xs_in_ref
    tm = x_ref.shape[0]
    base = pl.program_id(0) * tm

    def issue(r, carry):
        _row_copy(x_ref, r, xs_ref, d1_ref[base + r], sem).start()
        _row_copy(x_ref, r, xs_ref, d2_ref[base + r], sem).start()
        return carry

    lax.fori_loop(0, tm, issue, 0, unroll=8)

    for _ in range(2):
        pltpu.make_async_copy(x_ref, xs_ref.at[pl.ds(0, tm)], sem).wait()


def scatter_rows(x, dest1, dest2, n_rows, tm):
    t, d = x.shape
    zeros = jnp.zeros((n_rows, d), x.dtype)
    return pl.pallas_call(
        _scatter_rows_kernel,
        grid_spec=pltpu.PrefetchScalarGridSpec(
            num_scalar_prefetch=2,
            grid=(t // tm,),
            in_specs=[pl.BlockSpec((tm, d), lambda i, d1, d2: (i, 0)),
                      pl.BlockSpec(memory_space=pl.ANY)],
            out_specs=pl.BlockSpec(memory_space=pl.ANY),
            scratch_shapes=[pltpu.SemaphoreType.DMA(())]),
        out_shape=jax.ShapeDtypeStruct((n_rows, d), x.dtype),
        input_output_aliases={3: 0},
        compiler_params=_params("arbitrary"),
        name="moe_scatter_rows",
    )(dest1, dest2, x, zeros)


def _moe_group_kernel(te_ref, nu_ref, x_ref, g_ref, wg_ref, wu_ref, wd_ref, o_ref, hn_ref):
    i = pl.program_id(0)
    j = pl.program_id(1)
    used = i < nu_ref[0]

    @pl.when(j == 0)
    def _():
        o_ref[...] = jnp.zeros(o_ref.shape, F32)

    @pl.when(used & (j == 0))
    def _():
        hn_ref[...] = _rms(x_ref[...], g_ref[...]).astype(BF16)

    @pl.when(used)
    def _():
        hn = hn_ref[...]
        a = jnp.dot(hn, wg_ref[...], preferred_element_type=F32)
        u = jnp.dot(hn, wu_ref[...], preferred_element_type=F32)
        act = (a * jax.nn.sigmoid(a) * u).astype(BF16)
        o_ref[...] += jnp.dot(act, wd_ref[...], preferred_element_type=F32)


def moe_group_ffn(xs, g, tile_expert, n_used, wg, wu, wd, tm, tf):
    p, d = xs.shape
    f = wg.shape[2]
    nj = f // tf
    chunk = lambda i, j, te, nu: jnp.where(i < nu[0], j, nj - 1)
    return pl.pallas_call(
        _moe_group_kernel,
        grid_spec=pltpu.PrefetchScalarGridSpec(
            num_scalar_prefetch=2,
            grid=(p // tm, nj),
            in_specs=[pl.BlockSpec((tm, d), lambda i, j, te, nu: (i, 0)),
                      pl.BlockSpec((1, d), lambda i, j, te, nu: (0, 0)),
                      pl.BlockSpec((None, d, tf), lambda i, j, te, nu: (te[i], 0, chunk(i, j, te, nu))),
                      pl.BlockSpec((None, d, tf), lambda i, j, te, nu: (te[i], 0, chunk(i, j, te, nu))),
                      pl.BlockSpec((None, tf, d), lambda i, j, te, nu: (te[i], chunk(i, j, te, nu), 0))],
            out_specs=pl.BlockSpec((tm, d), lambda i, j, te, nu: (i, 0)),
            scratch_shapes=[pltpu.VMEM((tm, d), BF16)]),
        out_shape=jax.ShapeDtypeStruct((p, d), F32),
        compiler_params=_params("arbitrary", "arbitrary"),
        name="moe_group_ffn",
    )(tile_expert, n_used, xs, g.reshape(1, d), wg, wu, wd)


def _moe_combine_kernel(d1_ref, d2_ref, x_ref, gates_ref, fg_ref, ys_ref, o_ref, y1_ref, y2_ref, sem,
                        *, final_norm):
    tm = x_ref.shape[0]
    i = pl.program_id(0)

    def issue(tile, slot):
        base = tile * tm

        def body(r, carry):
            _row_copy(ys_ref, d1_ref[base + r], y1_ref.at[slot], r, sem.at[slot]).start()
            _row_copy(ys_ref, d2_ref[base + r], y2_ref.at[slot], r, sem.at[slot]).start()
            return carry

        lax.fori_loop(0, tm, body, 0, unroll=8)

    @pl.when(i == 0)
    def _():
        issue(0, 0)

    @pl.when(i + 1 < pl.num_programs(0))
    def _():
        issue(i + 1, (i + 1) % 2)

    slot = i % 2
    pltpu.make_async_copy(ys_ref.at[pl.ds(0, tm)], y1_ref.at[slot], sem.at[slot]).wait()
    pltpu.make_async_copy(ys_ref.at[pl.ds(0, tm)], y2_ref.at[slot], sem.at[slot]).wait()
    gates = gates_ref[...]
    out = x_ref[...] + gates[:, 0:1] * y1_ref[slot] + gates[:, 1:2] * y2_ref[slot]
    if final_norm:
        out = _rms(out, fg_ref[...])
    o_ref[...] = out


def moe_combine(x, gates, ys, dest1, dest2, tm, final_gain=None):
    t, d = x.shape
    fg = jnp.ones((1, d), F32) if final_gain is None else final_gain.reshape(1, d)
    return pl.pallas_call(
        functools.partial(_moe_combine_kernel, final_norm=final_gain is not None),
        grid_spec=pltpu.PrefetchScalarGridSpec(
            num_scalar_prefetch=2,
            grid=(t // tm,),
            in_specs=[pl.BlockSpec((tm, d), lambda i, d1, d2: (i, 0)),
                      pl.BlockSpec((tm, 128), lambda i, d1, d2: (i, 0)),
                      pl.BlockSpec((1, d), lambda i, d1, d2: (0, 0)),
                      pl.BlockSpec(memory_space=pl.ANY)],
            out_specs=pl.BlockSpec((tm, d), lambda i, d1, d2: (i, 0)),
            scratch_shapes=[pltpu.VMEM((2, tm, d), F32), pltpu.VMEM((2, tm, d), F32),
                            pltpu.SemaphoreType.DMA((2,))]),
        out_shape=jax.ShapeDtypeStruct((t, d), F32),
        compiler_params=_params("arbitrary"),
        name="moe_combine",
    )(dest1, dest2, x, gates, fg, ys)


def moe_ffn(x, g, wr_pad, wg, wu, wd, final_gain=None):
    t, d = x.shape
    ne = wg.shape[0]
    gates, route, counts = router(x, g, wr_pad, ROUTE_TM)
    counts = counts[0, :ne]
    padded = (counts + MOE_TILE - 1) // MOE_TILE * MOE_TILE
    ends = jnp.cumsum(padded)
    starts = ends - padded
    expert_ids = jnp.arange(ne, dtype=jnp.int32)[None, :]
    start_of = lambda e: jnp.sum(jnp.where(e[:, None] == expert_ids, starts[None, :], 0), axis=1)
    dest1 = (start_of(route[:, 0]) + route[:, 2]).astype(jnp.int32)
    dest2 = (start_of(route[:, 1]) + route[:, 3]).astype(jnp.int32)
    n_tiles = (2 * t) // MOE_TILE + ne
    tile_start = jnp.arange(n_tiles, dtype=jnp.int32) * MOE_TILE
    tile_expert = jnp.minimum(jnp.sum(tile_start[:, None] >= ends[None, :], axis=1), ne - 1).astype(jnp.int32)
    n_used = (ends[-1:] // MOE_TILE).astype(jnp.int32)
    xs = scatter_rows(x, dest1, dest2, n_tiles * MOE_TILE, ROUTE_TM)
    ys = moe_group_ffn(xs, g, tile_expert, n_used, wg, wu, wd, MOE_TILE, 512)
    return moe_combine(x, gates, ys, dest1, dest2, ROUTE_TM, final_gain)


def _kvnorm_kernel(c_ref, g_ref, o_ref, ot_ref):
    y = _rms(c_ref[0].astype(F32), g_ref[...])
    o_ref[0, 0] = y.astype(BF16)
    ot_ref[0, 0, :KV_RANK] = y.T.astype(BF16)
    ot_ref[0, 0, KV_RANK:] = jnp.ones((ONES_ROWS, DSA_QB), BF16)


def kv_latent_norm(p1, ccol, g):
    b, s, _ = p1.shape
    nc = s // DSA_QB
    out = jax.ShapeDtypeStruct((b, nc, DSA_QB, KV_RANK), BF16)
    out_t = jax.ShapeDtypeStruct((b, nc, KV_RANK + ONES_ROWS, DSA_QB), BF16)
    blk = pl.BlockSpec((1, 1, DSA_QB, KV_RANK), lambda bi, j: (bi, j, 0, 0))
    blk_t = pl.BlockSpec((1, 1, KV_RANK + ONES_ROWS, DSA_QB), lambda bi, j: (bi, j, 0, 0))
    return pl.pallas_call(
        _kvnorm_kernel,
        grid=(b, nc),
        in_specs=[pl.BlockSpec((1, DSA_QB, KV_RANK), lambda bi, j: (bi, j, ccol)),
                  pl.BlockSpec((1, KV_RANK), lambda bi, j: (0, 0))],
        out_specs=[blk, blk_t],
        out_shape=[out, out_t],
        compiler_params=_params("parallel", "parallel"),
        name="kv_latent_norm",
    )(p1, g.reshape(1, KV_RANK))


def _select_kernel(qi_ref, kall_ref, kq_ref, mask_ref, sc_ref, jcut_ref, *, nchunks_total):
    qb = DSA_QB
    i = pl.program_id(1)
    nch = i + 1
    shape = (qb, qb)
    row = lax.broadcasted_iota(jnp.int32, shape, 0)
    qpos = i * qb + lax.broadcasted_iota(jnp.int32, shape, 1)

    wt = kq_ref[0].astype(F32).T
    wscale = (N_IDX_HEADS ** -0.5) * (IDX_DIM ** -0.5)

    def score_chunk(j, carry):
        start = pl.multiple_of(j * qb, qb)
        kc = kall_ref[0, pl.ds(start, qb), :][:, :IDX_DIM]
        acc = jnp.zeros(shape, F32)
        for h in range(N_IDX_HEADS):
            qh = qi_ref[0, :, h * IDX_DIM:(h + 1) * IDX_DIM]
            s = lax.dot_general(kc, qh, _NT, preferred_element_type=F32)
            acc = acc + jnp.maximum(s, 0.0) * (wt[IDX_DIM + h:IDX_DIM + h + 1, :] * wscale)
        sc_ref[j] = jnp.where(j * qb + row <= qpos, acc, NEG_INF)
        return carry

    lax.fori_loop(0, nch, score_chunk, 0)

    def count(pred):
        def body(j, c):
            hit = jnp.where(pred(sc_ref[j], j), 1, 0)
            return c + jnp.sum(hit.reshape(qb // 8, 8, qb), axis=0)
        c = lax.fori_loop(0, nch, body, jnp.zeros((8, qb), jnp.int32))
        return jnp.sum(c, axis=0, keepdims=True)

    def key_to_float(x):
        return pltpu.bitcast(x ^ ((x >> 31) & 0x7FFFFFFF), F32)

    def rank_reached(x):
        cand = key_to_float(x)
        return (count(lambda s, j: s >= cand) >= TOPK) | (x < KEY_OF_NEG_INF)

    thr_key = jnp.where(rank_reached(jnp.zeros((1, qb), jnp.int32)), 0, INT_MIN)

    def bit_body(t, thr_key):
        cand = thr_key + jnp.left_shift(jnp.int32(1), 30 - t)
        return jnp.where(rank_reached(cand), cand, thr_key)

    thr = key_to_float(lax.fori_loop(0, 31, bit_body, thr_key))

    n_gt = count(lambda s, j: s > thr)
    n_ge = count(lambda s, j: s >= thr)
    need = TOPK - n_gt

    jcut_ref[...] = jnp.full((1, qb), 1 << 30, jnp.int32)

    @pl.when(jnp.max(n_ge) > TOPK)
    def _():
        def idx_body(t, x):
            cand = x + jnp.left_shift(jnp.int32(1), 10 - t)
            below = count(lambda s, j: (s == thr) & (j * qb + row <= cand - 1))
            return jnp.where(below < need, cand, x)
        jcut_ref[...] = lax.fori_loop(0, 11, idx_body, jnp.zeros((1, qb), jnp.int32))

    jcut = jcut_ref[...]

    def write_chunk(j, carry):
        s = sc_ref[j]
        spos = j * qb + row
        sel = (s > thr) | ((s == thr) & (spos <= jcut))
        sel = sel & (spos <= qpos)
        start = pl.multiple_of(j * qb, qb)
        mask_ref[0, pl.ds(start, qb), :] = jnp.where(sel, 0.0, NEG_INF)
        return carry

    lax.fori_loop(0, nch, write_chunk, 0)

    def fill_chunk(j, carry):
        start = pl.multiple_of(j * qb, qb)
        mask_ref[0, pl.ds(start, qb), :] = jnp.full(shape, NEG_INF, F32)
        return carry

    lax.fori_loop(nch, nchunks_total, fill_chunk, 0)


def dsa_select(p1, qicol, kwcol):
    b, s, _ = p1.shape
    nq = s // DSA_QB
    return pl.pallas_call(
        functools.partial(_select_kernel, nchunks_total=nq),
        grid=(b, nq),
        in_specs=[pl.BlockSpec((1, DSA_QB, N_IDX_HEADS * IDX_DIM), lambda bi, i: (bi, i, qicol)),
                  pl.BlockSpec((1, s, 128), lambda bi, i: (bi, 0, kwcol)),
                  pl.BlockSpec((1, DSA_QB, 128), lambda bi, i: (bi, i, kwcol))],
        out_specs=pl.BlockSpec((1, s, DSA_QB), lambda bi, i: (bi, 0, i)),
        out_shape=jax.ShapeDtypeStruct((b, s, s), F32),
        scratch_shapes=[pltpu.VMEM((nq, DSA_QB, DSA_QB), F32),
                        pltpu.VMEM((1, DSA_QB), jnp.int32)],
        compiler_params=_params("parallel", "arbitrary"),
        name="dsa_select",
    )(p1, p1, p1)


def _dsa_attn_kernel(q_ref, c_ref, ct_ref, mask_ref, tz_ref, wuk_ref, wuvt_ref, o_ref,
                     ql_ref, ot_ref, acc_ref):
    qb = DSA_QB
    hg = DSA_HEAD_GROUP
    width = hg * qb
    i = pl.program_id(1)

    for h in range(N_HEADS):
        qh = q_ref[0, :, h * HEAD_DIM:(h + 1) * HEAD_DIM]
        ql = lax.dot_general(wuk_ref[h], qh, _NT, preferred_element_type=F32)
        ql_ref[h // hg, :, (h % hg) * qb:(h % hg + 1) * qb] = (
            ql * (HEAD_DIM ** -0.5 * LOG2_E)).astype(BF16)

    def group_body(g, carry):
        acc_ref[...] = jnp.zeros((KV_RANK, width), F32)

        def chunk(j, ml):
            m, l = ml
            s = jnp.dot(c_ref[0, j], ql_ref[g], preferred_element_type=F32)
            d = 2 * (i - j)
            start = pl.multiple_of(j * qb, qb)
            mk = mask_ref[0, pl.ds(start, qb), :]
            tiles = []
            for hh in range(hg):
                h = g * hg + hh
                t_diag = tz_ref[d, h]
                tiles.append(mk + jnp.concatenate(
                    [jnp.concatenate([t_diag, tz_ref[d + 1, h]], axis=1),
                     jnp.concatenate([tz_ref[jnp.maximum(d - 1, 0), h], t_diag], axis=1)], axis=0))
            s = s + jnp.concatenate(tiles, axis=1)
            m_new = jnp.maximum(m, jnp.max(s, axis=0, keepdims=True))
            p = jnp.exp2(s - m_new)
            alpha = jnp.exp2(m - m_new)
            pv = jnp.dot(ct_ref[0, j], p.astype(BF16), preferred_element_type=F32)
            l = alpha * l + pv[KV_RANK:KV_RANK + 1]
            acc_ref[...] = alpha * acc_ref[...] + pv[:KV_RANK]
            return m_new, l

        init = (jnp.full((1, width), M_INIT, F32), jnp.zeros((1, width), F32))
        _, l = lax.fori_loop(0, i + 1, chunk, init)
        o_lat = (acc_ref[...] / l).astype(BF16)
        for hh in range(hg):
            h = g * hg + hh
            ot_ref[h] = jnp.dot(wuvt_ref[h], o_lat[:, hh * qb:(hh + 1) * qb],
                                preferred_element_type=F32)
        return carry

    lax.fori_loop(0, N_HEADS // hg, group_body, 0)
    o_ref[0] = ot_ref[...].reshape(ATTN_W, qb).T.astype(o_ref.dtype)


def dsa_attention(p1, qcol, c, ct, mask, tz, wuk, wuvt):
    b, s, _ = p1.shape
    nq = s // DSA_QB
    const = lambda shape: pl.BlockSpec(shape, lambda bi, i: (0,) * len(shape))
    return pl.pallas_call(
        _dsa_attn_kernel,
        grid=(b, nq),
        in_specs=[pl.BlockSpec((1, DSA_QB, ATTN_W), lambda bi, i: (bi, i, qcol)),
                  pl.BlockSpec((1, nq, DSA_QB, KV_RANK), lambda bi, i: (bi, 0, 0, 0)),
                  pl.BlockSpec((1, nq, KV_RANK + ONES_ROWS, DSA_QB), lambda bi, i: (bi, 0, 0, 0)),
                  pl.BlockSpec((1, s, DSA_QB), lambda bi, i: (bi, 0, i)),
                  pl.BlockSpec(tz.shape, lambda bi, i: (0, 0, 0, 0), pipeline_mode=pl.Buffered(1)),
                  const(wuk.shape), const(wuvt.shape)],
        out_specs=pl.BlockSpec((1, DSA_QB, ATTN_W), lambda bi, i: (bi, i, 0)),
        out_shape=jax.ShapeDtypeStruct((b, s, ATTN_W), BF16),
        scratch_shapes=[pltpu.VMEM((N_HEADS // DSA_HEAD_GROUP, KV_RANK, DSA_HEAD_GROUP * DSA_QB), BF16),
                        pltpu.VMEM((N_HEADS, HEAD_DIM, DSA_QB), F32),
                        pltpu.VMEM((KV_RANK, DSA_HEAD_GROUP * DSA_QB), F32)],
        compiler_params=_params("parallel", "arbitrary"),
        name="dsa_attention",
    )(p1, c, ct, mask, tz, wuk, wuvt)


def _rmsnorm_kernel(x_ref, g_ref, o_ref):
    o_ref[...] = _rms(x_ref[...], g_ref[...])


def rmsnorm(x, g, tm):
    t, d = x.shape
    return pl.pallas_call(
        _rmsnorm_kernel,
        grid=(t // tm,),
        in_specs=[pl.BlockSpec((tm, d), lambda i: (i, 0)), pl.BlockSpec((1, d), lambda i: (0, 0))],
        out_specs=pl.BlockSpec((tm, d), lambda i: (i, 0)),
        out_shape=jax.ShapeDtypeStruct((t, d), F32),
        compiler_params=_params("parallel"),
        name="final_rmsnorm",
    )(x, g.reshape(1, d))


def _rel_bucket(dist):
    n = np.maximum(dist, 0)
    max_exact = REL_BUCKETS // 2
    nf = np.maximum(n, 1).astype(np.float32)
    large = max_exact + (np.log(nf / np.float32(max_exact)) / np.float32(math.log(REL_MAX_DIST / max_exact))
                         * np.float32(REL_BUCKETS - max_exact)).astype(np.int32)
    large = np.minimum(large, REL_BUCKETS - 1)
    return np.where(n < max_exact, n, large)


def _bias_table_kernel(bkt_ref, rb_ref, o_ref):
    bkt = bkt_ref[0]
    for h in range(N_HEADS):
        acc = jnp.full(bkt.shape, NEG_INF, F32)
        for k in range(REL_BUCKETS):
            acc = jnp.where(bkt == k, rb_ref[k, h], acc)
        o_ref[0, h] = acc


def bias_table(bkt, rel_bias):
    n, r, c = bkt.shape
    return pl.pallas_call(
        _bias_table_kernel,
        grid=(n,),
        in_specs=[pl.BlockSpec((1, r, c), lambda i: (i, 0, 0)),
                  pl.BlockSpec(memory_space=pltpu.SMEM)],
        out_specs=pl.BlockSpec((1, N_HEADS, r, c), lambda i: (i, 0, 0, 0)),
        out_shape=jax.ShapeDtypeStruct((n, N_HEADS, r, c), F32),
        compiler_params=_params("parallel"),
        name="bias_table",
    )(bkt, rel_bias.astype(F32))


def _dilated_buckets(window, dilation):
    qi = np.arange(BLOCK)[:, None]
    ki = np.arange(2 * BLOCK)[None, :]
    rel = qi + BLOCK - ki
    bkt = _rel_bucket(rel * dilation)
    band = (rel >= 0) & (rel <= window // dilation)
    return np.stack([np.where(band, bkt, -1),
                     np.where(band & (ki >= BLOCK), bkt, -1)]).astype(np.int32)


def _dsa_buckets(s):
    nd = s // BLOCK
    key = np.arange(BLOCK)[:, None]
    qry = np.arange(BLOCK)[None, :]
    dist = np.arange(nd)[:, None, None] * BLOCK + (qry - key)[None]
    return _rel_bucket(dist).astype(np.int32)


def kernel(x, mem, rel_bias, mem_norm, final_norm, mixer_norm, ffn_norm, w_mem_kv, w_out,
           even_w_in, even_w_gate, even_w_up, even_w_down,
           odd_w_in, odd_kv_norm, odd_w_uk, odd_w_uv,
           odd_w_router, odd_w_gate, odd_w_up, odd_w_down):
    b, s, d = x.shape
    t = b * s
    depth = mixer_norm.shape[0]
    xt = x.reshape(t, d)
    mem2 = mem.reshape(b * N_MEM, d)
    dil_tables = [bias_table(_dilated_buckets(w, dl), rel_bias) for w, dl in DIL_PAIRS]
    tz = bias_table(_dsa_buckets(s), rel_bias * LOG2_E)

    for i in range(depth):
        j = i // 2
        kvm = norm_matmul(mem2, mem_norm, w_mem_kv[i].astype(BF16), BF16, 512, 512)
        kvm = kvm.reshape(b, N_MEM, 2 * CROSS_WIDTH)
        wo = w_out[i].astype(BF16)
        if i % 2 == 0:
            w_in = even_w_in[j].astype(BF16)
            proj = even_in_projection(xt, mixer_norm[i], w_in[:, :5 * ATTN_W], b, s, 512)
            qc = norm_matmul(xt, mixer_norm[i], w_in[:, 5 * ATTN_W:], BF16, 512, CROSS_WIDTH)
            outs, lses = [], []
            for grp in range(len(DIL_PAIRS)):
                o, lse = dilated_branch(proj[grp], proj[3 + grp], proj[6 + grp], dil_tables[grp])
                outs.append(o)
                lses.append(lse)
            mix = combine_groups(outs, lses, 256).reshape(t, ATTN_W)
            cross = cross_attention(qc.reshape(b, s, CROSS_WIDTH), 0, kvm, 512)
        else:
            o_q, o_c, o_qi, o_ki, o_wi, o_qc = 0, 1024, 1280, 1792, 1856, 1864
            w = odd_w_in[j]
            pad = jnp.zeros((d, 128 - IDX_DIM - N_IDX_HEADS), w.dtype)
            w_in = jnp.concatenate(
                [w[:, o_q:o_c], w[:, o_qi:o_ki], w[:, o_c:o_qi], w[:, o_qc:],
                 w[:, o_ki:o_wi], w[:, o_wi:o_qc], pad], axis=1).astype(BF16)
            p1 = norm_matmul(xt, mixer_norm[i], w_in, BF16, 512, w_in.shape[1])
            p1 = p1.reshape(b, s, w_in.shape[1])
            c, ct = kv_latent_norm(p1, 6, odd_kv_norm[j])
            mask = dsa_select(p1, 2, 16)
            wuk = jnp.transpose(odd_w_uk[j], (1, 0, 2)).astype(BF16)
            wuvt = jnp.transpose(odd_w_uv[j], (1, 2, 0)).astype(BF16)
            mix = dsa_attention(p1, 0, c, ct, mask, tz, wuk, wuvt).reshape(t, ATTN_W)
            cross = cross_attention(p1, 7, kvm, 512)
        xt = out_projection(mix, cross.reshape(t, CROSS_WIDTH), wo[:ATTN_W], wo[ATTN_W:], xt, 512)
        if i % 2 == 0:
            xt = swiglu_ffn(xt, ffn_norm[i], even_w_gate[j].astype(BF16), even_w_up[j].astype(BF16),
                            even_w_down[j].astype(BF16), 512, 1408)
        else:
            wr = jnp.pad(odd_w_router[j], ((0, 0), (0, 128 - N_EXPERTS)))
            xt = moe_ffn(xt, ffn_norm[i], wr, odd_w_gate[j].astype(BF16), odd_w_up[j].astype(BF16),
                         odd_w_down[j].astype(BF16), final_norm if i == depth - 1 else None)
    if depth % 2 == 1:
        xt = rmsnorm(xt, final_norm, 512)
    return xt.reshape(b, s, d)
```

```python
import functools
import math

import jax
import jax.numpy as jnp
import numpy as np
from jax import lax
from jax.experimental import pallas as pl
from jax.experimental.pallas import tpu as pltpu

D_MODEL = 1024
N_HEADS = 16
HEAD_DIM = 64
ATTN_W = N_HEADS * HEAD_DIM
DIL_PAIRS = ((128, 1), (512, 4), (2048, 16))
BLOCK = 128
LANES = 128
KV_RANK = 256
N_IDX_HEADS = 8
IDX_DIM = 64
TOPK = 256
N_MEM = 256
N_CROSS_HEADS = 4
CROSS_WIDTH = 256
REL_BUCKETS = 32
REL_MAX_DIST = 2048
N_EXPERTS = 8
RMS_EPS = 1e-6
NEG_INF = -1e30
M_INIT = -5e29
LOG2_E = math.log2(math.e)

F32 = jnp.float32
BF16 = jnp.bfloat16
V7X_VMEM_LIMIT = 56 * 1024 * 1024
DSA_QB = 256
DSA_HEAD_GROUP = 8
ONES_ROWS = 16
INT_MIN = -(2 ** 31)
KEY_OF_NEG_INF = (0xFF800000 ^ 0x7FFFFFFF) - (1 << 32)
MOE_TILE = 512
ROUTE_TM = 256
MOE_FF_CHUNK = 1792

_NT = (((1,), (1,)), ((), ()))


def _params(*sem):
    return pltpu.CompilerParams(dimension_semantics=sem, vmem_limit_bytes=V7X_VMEM_LIMIT)


def _rms(x, g):
    return x * lax.rsqrt(jnp.mean(x * x, axis=-1, keepdims=True) + RMS_EPS) * g


def _norm_mm_kernel(x_ref, g_ref, w_ref, o_ref, xn_ref):
    @pl.when(pl.program_id(1) == 0)
    def _():
        xn_ref[...] = _rms(x_ref[...].astype(F32), g_ref[...]).astype(BF16)

    o_ref[...] = jnp.dot(xn_ref[...], w_ref[...], preferred_element_type=F32).astype(o_ref.dtype)


def norm_matmul(x, g, w, out_dtype, tm, tn):
    t, k = x.shape
    n = w.shape[1]
    return pl.pallas_call(
        _norm_mm_kernel,
        grid=(t // tm, n // tn),
        in_specs=[pl.BlockSpec((tm, k), lambda i, j: (i, 0)),
                  pl.BlockSpec((1, k), lambda i, j: (0, 0)),
                  pl.BlockSpec((k, tn), lambda i, j: (0, j))],
        out_specs=pl.BlockSpec((tm, tn), lambda i, j: (i, j)),
        out_shape=jax.ShapeDtypeStruct((t, n), out_dtype),
        scratch_shapes=[pltpu.VMEM((tm, k), BF16)],
        compiler_params=_params("parallel", "arbitrary"),
        name="norm_matmul",
    )(x, g.reshape(1, k), w)


def _even_proj_kernel(x_ref, g_ref, w_ref, q1_ref, q4_ref, q16_ref, k1_ref, k4_ref, k16_ref,
                      v1_ref, v4_ref, v16_ref, xn_ref, y_ref):
    j = pl.program_id(1)
    tm = x_ref.shape[0]

    @pl.when(j == 0)
    def _():
        xn_ref[...] = _rms(x_ref[...], g_ref[...]).astype(BF16)

    y = jnp.dot(xn_ref[...], w_ref[...], preferred_element_type=F32)
    n_lane_blocks = y.shape[1] // LANES
    for c in range(n_lane_blocks):
        y_ref[c] = y[:, c * LANES:(c + 1) * LANES]

    def put(dst_ref, d):
        if d == 1:
            dst_ref[0, 0] = y.astype(BF16)
            return
        for r in range(d):
            for c in range(n_lane_blocks):
                rows = y_ref[c, pl.ds(r, tm // d, stride=d), :]
                dst_ref[0, r, :, c * LANES:(c + 1) * LANES] = rows.astype(BF16)

    column_dsts = (((q1_ref, 1),), ((q4_ref, 4),), ((q16_ref, 16),),
                   ((k1_ref, 1), (k4_ref, 4), (k16_ref, 16)),
                   ((v1_ref, 1), (v4_ref, 4), (v16_ref, 16)))
    for col, dsts in enumerate(column_dsts):
        @pl.when(j == col)
        def _(dsts=dsts):
            for ref, d in dsts:
                put(ref, d)


def even_in_projection(x, g, w, b, s, tm):
    t, kdim = x.shape
    per_b = s // tm
    dils = [dl for _, dl in DIL_PAIRS]
    layouts = dils + dils + dils
    spec = lambda dl: pl.BlockSpec((1, dl, tm // dl, ATTN_W), lambda i, j: (i // per_b, 0, i % per_b, 0))
    return pl.pallas_call(
        _even_proj_kernel,
        grid=(t // tm, 5),
        in_specs=[pl.BlockSpec((tm, kdim), lambda i, j: (i, 0)),
                  pl.BlockSpec((1, kdim), lambda i, j: (0, 0)),
                  pl.BlockSpec((kdim, ATTN_W), lambda i, j: (0, j))],
        out_specs=[spec(dl) for dl in layouts],
        out_shape=[jax.ShapeDtypeStruct((b, dl, s // dl, ATTN_W), BF16) for dl in layouts],
        scratch_shapes=[pltpu.VMEM((tm, kdim), BF16), pltpu.VMEM((ATTN_W // LANES, tm, LANES), F32)],
        compiler_params=_params("parallel", "arbitrary"),
        name="even_in_projection",
    )(x, g.reshape(1, kdim), w)


def _dil_kernel(q_ref, kp_ref, kc_ref, vp_ref, vc_ref, tab_ref, o_ref, lse_ref, s_ref, p_ref, m_ref):
    first = (pl.program_id(2) == 0).astype(jnp.int32)
    pair = 2 * HEAD_DIM
    lo_q = lax.broadcasted_iota(jnp.int32, (BLOCK, pair), 1) < HEAD_DIM
    lo_k = lax.broadcasted_iota(jnp.int32, (2 * BLOCK, pair), 1) < HEAD_DIM
    ones_bd = jnp.concatenate([jnp.where(lo_k, 1.0, 0.0), jnp.where(lo_k, 0.0, 1.0)], axis=0).astype(BF16)
    scale = jnp.asarray(HEAD_DIM ** -0.5, BF16)
    zero = jnp.zeros((), BF16)
    n_pairs = N_HEADS // 2
    for p in range(n_pairs):
        cols = slice(p * pair, (p + 1) * pair)
        q = q_ref[0, 0, :, cols] * scale
        k = jnp.concatenate([kp_ref[0, 0, :, cols], kc_ref[0, 0, :, cols]], axis=0)
        q_ab = jnp.concatenate([jnp.where(lo_q, q, zero), jnp.where(lo_q, zero, q)], axis=0)
        s_ab = lax.dot_general(q_ab, k, _NT, preferred_element_type=F32)
        s_ref[2 * p] = s_ab[:BLOCK] + tab_ref[first, 2 * p]
        s_ref[2 * p + 1] = s_ab[BLOCK:] + tab_ref[first, 2 * p + 1]
    for p in range(n_pairs):
        sa = s_ref[2 * p]
        sb = s_ref[2 * p + 1]
        ma = jnp.max(sa, axis=-1, keepdims=True)
        mb = jnp.max(sb, axis=-1, keepdims=True)
        p_ref[p, :, :2 * BLOCK] = jnp.exp(sa - ma).astype(BF16)
        p_ref[p, :, 2 * BLOCK:] = jnp.exp(sb - mb).astype(BF16)
        m_ref[p] = jnp.where(lo_q, ma, mb)
    for p in range(n_pairs):
        cols = slice(p * pair, (p + 1) * pair)
        v = jnp.concatenate([vp_ref[0, 0, :, cols], vc_ref[0, 0, :, cols]], axis=0)
        v_bd = jnp.concatenate([jnp.where(lo_k, v, zero), jnp.where(lo_k, zero, v)], axis=0)
        ol = jnp.dot(p_ref[p], jnp.concatenate([v_bd, ones_bd], axis=1), preferred_element_type=F32)
        o, l = ol[:, :pair], ol[:, pair:]
        o_ref[0, 0, :, cols] = o / l
        lse_ref[0, 0, :, cols] = m_ref[p] + jnp.log(l)


def dilated_branch(q, k, v, table):
    b, dilation, ln, _ = q.shape
    nb = ln // BLOCK
    blk = (1, 1, BLOCK, ATTN_W)
    cur = pl.BlockSpec(blk, lambda bi, r, n: (bi, r, n, 0))
    prev = pl.BlockSpec(blk, lambda bi, r, n: (bi, r, jnp.maximum(n - 1, 0), 0))
    out = jax.ShapeDtypeStruct(q.shape, F32)
    return pl.pallas_call(
        _dil_kernel,
        grid=(b, dilation, nb),
        in_specs=[cur, prev, cur, prev, cur,
                  pl.BlockSpec(table.shape, lambda bi, r, n: (0, 0, 0, 0))],
        out_specs=[cur, cur],
        out_shape=[out, out],
        scratch_shapes=[pltpu.VMEM((N_HEADS, BLOCK, 2 * BLOCK), F32),
                        pltpu.VMEM((N_HEADS // 2, BLOCK, 4 * BLOCK), BF16),
                        pltpu.VMEM((N_HEADS // 2, BLOCK, 2 * HEAD_DIM), F32)],
        compiler_params=_params("parallel", "parallel", "arbitrary"),
        name=f"dilated_attn_d{dilation}",
    )(q, k, k, v, v, table)


def _combine_kernel(o1, o4, o16, l1, l4, l16, out_ref, so4, sl4, so16, sl16):
    tm = out_ref.shape[1]
    n_lane_blocks = out_ref.shape[2] // LANES

    def to_token_order(src_ref, dst_ref, d):
        for r in range(d):
            for c in range(n_lane_blocks):
                dst_ref[c, pl.ds(r, tm // d, stride=d), :] = src_ref[0, r, :, c * LANES:(c + 1) * LANES]

    to_token_order(o4, so4, 4)
    to_token_order(l4, sl4, 4)
    to_token_order(o16, so16, 16)
    to_token_order(l16, sl16, 16)
    for c in range(n_lane_blocks):
        cols = slice(c * LANES, (c + 1) * LANES)
        a1, a2, a3 = l1[0, 0, :, cols], sl4[c], sl16[c]
        m = jnp.maximum(jnp.maximum(a1, a2), a3)
        w1, w2, w3 = jnp.exp(a1 - m), jnp.exp(a2 - m), jnp.exp(a3 - m)
        num = w1 * o1[0, 0, :, cols] + w2 * so4[c] + w3 * so16[c]
        out_ref[0, :, cols] = (num / (w1 + w2 + w3)).astype(out_ref.dtype)


def combine_groups(outs, lses, tm):
    b, _, s, w = outs[0].shape
    spec = lambda dl: pl.BlockSpec((1, dl, tm // dl, w), lambda bi, i: (bi, 0, i, 0))
    specs = [spec(o.shape[1]) for o in outs]
    return pl.pallas_call(
        _combine_kernel,
        grid=(b, s // tm),
        in_specs=specs + specs,
        out_specs=pl.BlockSpec((1, tm, w), lambda bi, i: (bi, i, 0)),
        out_shape=jax.ShapeDtypeStruct((b, s, w), BF16),
        scratch_shapes=[pltpu.VMEM((w // LANES, tm, LANES), F32)] * 4,
        compiler_params=_params("parallel", "parallel"),
        name="combine_groups",
    )(*outs, *lses)


def _cross_kernel(q_ref, kv_ref, o_ref):
    for h in range(N_CROSS_HEADS):
        sl = slice(h * HEAD_DIM, (h + 1) * HEAD_DIM)
        vsl = slice(CROSS_WIDTH + h * HEAD_DIM, CROSS_WIDTH + (h + 1) * HEAD_DIM)
        s = lax.dot_general(q_ref[0, :, sl], kv_ref[0, :, sl], _NT,
                            preferred_element_type=F32) * (HEAD_DIM ** -0.5)
        m = jnp.max(s, axis=-1, keepdims=True)
        p = jnp.exp(s - m)
        l = jnp.sum(p, axis=-1, keepdims=True)
        o = jnp.dot(p.astype(BF16), kv_ref[0, :, vsl], preferred_element_type=F32)
        o_ref[0, :, sl] = (o / l).astype(o_ref.dtype)


def cross_attention(qsrc, qcol, kv, tm):
    b, s, _ = qsrc.shape
    return pl.pallas_call(
        _cross_kernel,
        grid=(b, s // tm),
        in_specs=[pl.BlockSpec((1, tm, CROSS_WIDTH), lambda bi, i: (bi, i, qcol)),
                  pl.BlockSpec((1, N_MEM, 2 * CROSS_WIDTH), lambda bi, i: (bi, 0, 0))],
        out_specs=pl.BlockSpec((1, tm, CROSS_WIDTH), lambda bi, i: (bi, i, 0)),
        out_shape=jax.ShapeDtypeStruct((b, s, CROSS_WIDTH), BF16),
        compiler_params=_params("parallel", "parallel"),
        name="cross_attn",
    )(qsrc, kv)


def _outproj_kernel(mix_ref, cr_ref, wa_ref, wb_ref, x_ref, o_ref):
    acc = jnp.dot(mix_ref[...], wa_ref[...], preferred_element_type=F32)
    acc = acc + jnp.dot(cr_ref[...], wb_ref[...], preferred_element_type=F32)
    o_ref[...] = x_ref[...] + acc


def out_projection(mix, cross, wa, wb, x, tm):
    t, d = x.shape
    return pl.pallas_call(
        _outproj_kernel,
        grid=(t // tm,),
        in_specs=[pl.BlockSpec((tm, ATTN_W), lambda i: (i, 0)),
                  pl.BlockSpec((tm, CROSS_WIDTH), lambda i: (i, 0)),
                  pl.BlockSpec(wa.shape, lambda i: (0, 0)),
                  pl.BlockSpec(wb.shape, lambda i: (0, 0)),
                  pl.BlockSpec((tm, d), lambda i: (i, 0))],
        out_specs=pl.BlockSpec((tm, d), lambda i: (i, 0)),
        out_shape=jax.ShapeDtypeStruct((t, d), F32),
        compiler_params=_params("parallel"),
        name="out_projection",
    )(mix, cross, wa, wb, x)


def _swiglu_kernel(x_ref, g_ref, wg_ref, wu_ref, wd_ref, o_ref, hn_ref):
    @pl.when(pl.program_id(1) == 0)
    def _():
        x = x_ref[...]
        hn_ref[...] = _rms(x, g_ref[...]).astype(BF16)
        o_ref[...] = x

    hn = hn_ref[...]
    a = jnp.dot(hn, wg_ref[...], preferred_element_type=F32)
    u = jnp.dot(hn, wu_ref[...], preferred_element_type=F32)
    act = (a * jax.nn.sigmoid(a) * u).astype(BF16)
    o_ref[...] += jnp.dot(act, wd_ref[...], preferred_element_type=F32)


def swiglu_ffn(x, g, wg, wu, wd, tm, tf):
    t, d = x.shape
    f = wg.shape[1]
    mode = dict(pipeline_mode=pl.Buffered(1)) if tf == f else {}
    return pl.pallas_call(
        _swiglu_kernel,
        grid=(t // tm, f // tf),
        in_specs=[pl.BlockSpec((tm, d), lambda i, j: (i, 0)),
                  pl.BlockSpec((1, d), lambda i, j: (0, 0)),
                  pl.BlockSpec((d, tf), lambda i, j: (0, j), **mode),
                  pl.BlockSpec((d, tf), lambda i, j: (0, j), **mode),
                  pl.BlockSpec((tf, d), lambda i, j: (j, 0), **mode)],
        out_specs=pl.BlockSpec((tm, d), lambda i, j: (i, 0)),
        out_shape=jax.ShapeDtypeStruct((t, d), F32),
        scratch_shapes=[pltpu.VMEM((tm, d), BF16)],
        compiler_params=_params("parallel", "arbitrary"),
        name="swiglu_ffn",
    )(x, g.reshape(1, d), wg, wu, wd)


def _router_kernel(x_ref, g_ref, wr_ref, gates_ref, route_ref, counts_ref, run_ref):
    @pl.when(pl.program_id(0) == 0)
    def _():
        run_ref[...] = jnp.zeros(run_ref.shape, F32)

    hn = _rms(x_ref[...], g_ref[...])
    logits = jnp.dot(hn, wr_ref[...], preferred_element_type=F32,
                     precision=lax.Precision.HIGHEST)
    tm = logits.shape[0]
    lane = lax.broadcasted_iota(jnp.int32, logits.shape, 1)
    lg = jnp.where(lane < N_EXPERTS, logits, -jnp.inf)
    m1 = jnp.max(lg, axis=-1, keepdims=True)
    i1 = jnp.min(jnp.where(lg == m1, lane, 128), axis=-1, keepdims=True)
    lg2 = jnp.where(lane == i1, -jnp.inf, lg)
    m2 = jnp.max(lg2, axis=-1, keepdims=True)
    i2 = jnp.min(jnp.where(lg2 == m2, lane, 128), axis=-1, keepdims=True)
    e = jnp.exp(m2 - m1)
    gates_ref[...] = jnp.where(lane == 0, 1.0 / (1.0 + e), jnp.where(lane == 1, e / (1.0 + e), 0.0))

    assign = jnp.where((lane == i1) | (lane == i2), 1.0, 0.0)
    r = lax.broadcasted_iota(jnp.int32, (tm, tm), 0)
    c = lax.broadcasted_iota(jnp.int32, (tm, tm), 1)
    lower = jnp.where(r > c, 1.0, 0.0).astype(BF16)
    before = jnp.dot(lower, assign.astype(BF16), preferred_element_type=F32) + run_ref[...]
    rank1 = jnp.sum(jnp.where(lane == i1, before, 0.0), axis=-1, keepdims=True).astype(jnp.int32)
    rank2 = jnp.sum(jnp.where(lane == i2, before, 0.0), axis=-1, keepdims=True).astype(jnp.int32)
    route_ref[...] = jnp.where(lane == 0, i1, jnp.where(lane == 1, i2, jnp.where(
        lane == 2, rank1, jnp.where(lane == 3, rank2, 0))))
    run = run_ref[...] + jnp.sum(assign, axis=0, keepdims=True)
    run_ref[...] = run
    counts_ref[...] = run.astype(jnp.int32)


def router(x, g, wr_pad, tm):
    t, d = x.shape
    return pl.pallas_call(
        _router_kernel,
        grid=(t // tm,),
        in_specs=[pl.BlockSpec((tm, d), lambda i: (i, 0)),
                  pl.BlockSpec((1, d), lambda i: (0, 0)),
                  pl.BlockSpec((d, 128), lambda i: (0, 0))],
        out_specs=[pl.BlockSpec((tm, 128), lambda i: (i, 0)),
                   pl.BlockSpec((tm, 128), lambda i: (i, 0)),
                   pl.BlockSpec((1, 128), lambda i: (0, 0))],
        out_shape=[jax.ShapeDtypeStruct((t, 128), F32),
                   jax.ShapeDtypeStruct((t, 128), jnp.int32),
                   jax.ShapeDtypeStruct((1, 128), jnp.int32)],
        scratch_shapes=[pltpu.VMEM((1, 128), F32)],
        compiler_params=_params("arbitrary"),
        name="router",
    )(x, g.reshape(1, d), wr_pad)


def _row_copy(src, src_row, dst, dst_row, sem):
    return pltpu.make_async_copy(src.at[pl.ds(src_row, 1)], dst.at[pl.ds(dst_row, 1)], sem)


def _scatter_rows_kernel(d1_ref, d2_ref, x_ref, xs_in_ref, xs_ref, sem):
    del xs_in_ref
    tm = x_ref.shape[0]
    base = pl.program_id(0) * tm

    def issue(r, carry):
        _row_copy(x_ref, r, xs_ref, d1_ref[base + r], sem).start()
        _row_copy(x_ref, r, xs_ref, d2_ref[base + r], sem).start()
        return carry

    lax.fori_loop(0, tm, issue, 0, unroll=8)

    for _ in range(2):
        pltpu.make_async_copy(x_ref, xs_ref.at[pl.ds(0, tm)], sem).wait()


def scatter_rows(x, dest1, dest2, n_rows, tm):
    t, d = x.shape
    zeros = jnp.zeros((n_rows, d), x.dtype)
    return pl.pallas_call(
        _scatter_rows_kernel,
        grid_spec=pltpu.PrefetchScalarGridSpec(
            num_scalar_prefetch=2,
            grid=(t // tm,),
            in_specs=[pl.BlockSpec((tm, d), lambda i, d1, d2: (i, 0)),
                      pl.BlockSpec(memory_space=pl.ANY)],
            out_specs=pl.BlockSpec(memory_space=pl.ANY),
            scratch_shapes=[pltpu.SemaphoreType.DMA(())]),
        out_shape=jax.ShapeDtypeStruct((n_rows, d), x.dtype),
        input_output_aliases={3: 0},
        compiler_params=_params("arbitrary"),
        name="moe_scatter_rows",
    )(dest1, dest2, x, zeros)


def _moe_group_kernel(te_ref, nu_ref, x_ref, g_ref, wg_ref, wu_ref, wd_ref, o_ref, hn_ref):
    i = pl.program_id(0)
    j = pl.program_id(1)
    used = i < nu_ref[0]

    @pl.when(j == 0)
    def _():
        o_ref[...] = jnp.zeros(o_ref.shape, F32)

    @pl.when(used & (j == 0))
    def _():
        hn_ref[...] = _rms(x_ref[...], g_ref[...]).astype(BF16)

    @pl.when(used)
    def _():
        hn = hn_ref[...]
        a = jnp.dot(hn, wg_ref[...], preferred_element_type=F32)
        u = jnp.dot(hn, wu_ref[...], preferred_element_type=F32)
        act = (a * jax.nn.sigmoid(a) * u).astype(BF16)
        o_ref[...] += jnp.dot(act, wd_ref[...], preferred_element_type=F32)


def moe_group_ffn(xs, g, tile_expert, n_used, wg, wu, wd, tm, tf):
    p, d = xs.shape
    f = wg.shape[2]
    nj = f // tf
    chunk = lambda i, j, te, nu: jnp.where(i < nu[0], j, nj - 1)
    return pl.pallas_call(
        _moe_group_kernel,
        grid_spec=pltpu.PrefetchScalarGridSpec(
            num_scalar_prefetch=2,
            grid=(p // tm, nj),
            in_specs=[pl.BlockSpec((tm, d), lambda i, j, te, nu: (i, 0)),
                      pl.BlockSpec((1, d), lambda i, j, te, nu: (0, 0)),
                      pl.BlockSpec((None, d, tf), lambda i, j, te, nu: (te[i], 0, chunk(i, j, te, nu))),
                      pl.BlockSpec((None, d, tf), lambda i, j, te, nu: (te[i], 0, chunk(i, j, te, nu))),
                      pl.BlockSpec((None, tf, d), lambda i, j, te, nu: (te[i], chunk(i, j, te, nu), 0))],
            out_specs=pl.BlockSpec((tm, d), lambda i, j, te, nu: (i, 0)),
            scratch_shapes=[pltpu.VMEM((tm, d), BF16)]),
        out_shape=jax.ShapeDtypeStruct((p, d), F32),
        compiler_params=_params("arbitrary", "arbitrary"),
        name="moe_group_ffn",
    )(tile_expert, n_used, xs, g.reshape(1, d), wg, wu, wd)


def _moe_combine_kernel(d1_ref, d2_ref, x_ref, gates_ref, fg_ref, ys_ref, o_ref, y1_ref, y2_ref, sem,
                        *, final_norm):
    tm = x_ref.shape[0]
    i = pl.program_id(0)

    def issue(tile, slot):
        base = tile * tm

        def body(r, carry):
            _row_copy(ys_ref, d1_ref[base + r], y1_ref.at[slot], r, sem.at[slot]).start()
            _row_copy(ys_ref, d2_ref[base + r], y2_ref.at[slot], r, sem.at[slot]).start()
            return carry

        lax.fori_loop(0, tm, body, 0, unroll=8)

    @pl.when(i == 0)
    def _():
        issue(0, 0)

    @pl.when(i + 1 < pl.num_programs(0))
    def _():
        issue(i + 1, (i + 1) % 2)

    slot = i % 2
    pltpu.make_async_copy(ys_ref.at[pl.ds(0, tm)], y1_ref.at[slot], sem.at[slot]).wait()
    pltpu.make_async_copy(ys_ref.at[pl.ds(0, tm)], y2_ref.at[slot], sem.at[slot]).wait()
    gates = gates_ref[...]
    out = x_ref[...] + gates[:, 0:1] * y1_ref[slot] + gates[:, 1:2] * y2_ref[slot]
    if final_norm:
        out = _rms(out, fg_ref[...])
    o_ref[...] = out


def moe_combine(x, gates, ys, dest1, dest2, tm, final_gain=None):
    t, d = x.shape
    fg = jnp.ones((1, d), F32) if final_gain is None else final_gain.reshape(1, d)
    return pl.pallas_call(
        functools.partial(_moe_combine_kernel, final_norm=final_gain is not None),
        grid_spec=pltpu.PrefetchScalarGridSpec(
            num_scalar_prefetch=2,
            grid=(t // tm,),
            in_specs=[pl.BlockSpec((tm, d), lambda i, d1, d2: (i, 0)),
                      pl.BlockSpec((tm, 128), lambda i, d1, d2: (i, 0)),
                      pl.BlockSpec((1, d), lambda i, d1, d2: (0, 0)),
                      pl.BlockSpec(memory_space=pl.ANY)],
            out_specs=pl.BlockSpec((tm, d), lambda i, d1, d2: (i, 0)),
            scratch_shapes=[pltpu.VMEM((2, tm, d), F32), pltpu.VMEM((2, tm, d), F32),
                            pltpu.SemaphoreType.DMA((2,))]),
        out_shape=jax.ShapeDtypeStruct((t, d), F32),
        compiler_params=_params("arbitrary"),
        name="moe_combine",
    )(dest1, dest2, x, gates, fg, ys)


def moe_ffn(x, g, wr_pad, wg, wu, wd, final_gain=None):
    t, d = x.shape
    ne = wg.shape[0]
    gates, route, counts = router(x, g, wr_pad, ROUTE_TM)
    counts = counts[0, :ne]
    padded = (counts + MOE_TILE - 1) // MOE_TILE * MOE_TILE
    ends = jnp.cumsum(padded)
    starts = ends - padded
    expert_ids = jnp.arange(ne, dtype=jnp.int32)[None, :]
    start_of = lambda e: jnp.sum(jnp.where(e[:, None] == expert_ids, starts[None, :], 0), axis=1)
    dest1 = (start_of(route[:, 0]) + route[:, 2]).astype(jnp.int32)
    dest2 = (start_of(route[:, 1]) + route[:, 3]).astype(jnp.int32)
    n_tiles = (2 * t) // MOE_TILE + ne
    tile_start = jnp.arange(n_tiles, dtype=jnp.int32) * MOE_TILE
    tile_expert = jnp.minimum(jnp.sum(tile_start[:, None] >= ends[None, :], axis=1), ne - 1).astype(jnp.int32)
    n_used = (ends[-1:] // MOE_TILE).astype(jnp.int32)
    xs = scatter_rows(x, dest1, dest2, n_tiles * MOE_TILE, ROUTE_TM)
    ys = moe_group_ffn(xs, g, tile_expert, n_used, wg, wu, wd, MOE_TILE, MOE_FF_CHUNK)
    return moe_combine(x, gates, ys, dest1, dest2, ROUTE_TM, final_gain)


def _kvnorm_kernel(c_ref, g_ref, o_ref, ot_ref):
    y = _rms(c_ref[0].astype(F32), g_ref[...])
    o_ref[0, 0] = y.astype(BF16)
    ot_ref[0, 0, :KV_RANK] = y.T.astype(BF16)
    ot_ref[0, 0, KV_RANK:] = jnp.ones((ONES_ROWS, DSA_QB), BF16)


def kv_latent_norm(p1, ccol, g):
    b, s, _ = p1.shape
    nc = s // DSA_QB
    out = jax.ShapeDtypeStruct((b, nc, DSA_QB, KV_RANK), BF16)
    out_t = jax.ShapeDtypeStruct((b, nc, KV_RANK + ONES_ROWS, DSA_QB), BF16)
    blk = pl.BlockSpec((1, 1, DSA_QB, KV_RANK), lambda bi, j: (bi, j, 0, 0))
    blk_t = pl.BlockSpec((1, 1, KV_RANK + ONES_ROWS, DSA_QB), lambda bi, j: (bi, j, 0, 0))
    return pl.pallas_call(
        _kvnorm_kernel,
        grid=(b, nc),
        in_specs=[pl.BlockSpec((1, DSA_QB, KV_RANK), lambda bi, j: (bi, j, ccol)),
                  pl.BlockSpec((1, KV_RANK), lambda bi, j: (0, 0))],
        out_specs=[blk, blk_t],
        out_shape=[out, out_t],
        compiler_params=_params("parallel", "parallel"),
        name="kv_latent_norm",
    )(p1, g.reshape(1, KV_RANK))


def _select_kernel(qi_ref, kall_ref, kq_ref, mask_ref, sc_ref, jcut_ref, *, nchunks_total):
    qb = DSA_QB
    i = pl.program_id(1)
    nch = i + 1
    shape = (qb, qb)
    row = lax.broadcasted_iota(jnp.int32, shape, 0)
    qpos = i * qb + lax.broadcasted_iota(jnp.int32, shape, 1)

    wt = kq_ref[0].astype(F32).T
    wscale = (N_IDX_HEADS ** -0.5) * (IDX_DIM ** -0.5)

    def score_chunk(j, carry):
        start = pl.multiple_of(j * qb, qb)
        kc = kall_ref[0, pl.ds(start, qb), :][:, :IDX_DIM]
        acc = jnp.zeros(shape, F32)
        for h in range(N_IDX_HEADS):
            qh = qi_ref[0, :, h * IDX_DIM:(h + 1) * IDX_DIM]
            s = lax.dot_general(kc, qh, _NT, preferred_element_type=F32)
            acc = acc + jnp.maximum(s, 0.0) * (wt[IDX_DIM + h:IDX_DIM + h + 1, :] * wscale)
        sc_ref[j] = jnp.where(j * qb + row <= qpos, acc, NEG_INF)
        return carry

    lax.fori_loop(0, nch, score_chunk, 0)

    def count(pred):
        def body(j, c):
            hit = jnp.where(pred(sc_ref[j], j), 1, 0)
            return c + jnp.sum(hit.reshape(qb // 8, 8, qb), axis=0)
        c = lax.fori_loop(0, nch, body, jnp.zeros((8, qb), jnp.int32))
        return jnp.sum(c, axis=0, keepdims=True)

    def key_to_float(x):
        return pltpu.bitcast(x ^ ((x >> 31) & 0x7FFFFFFF), F32)

    def rank_reached(x):
        cand = key_to_float(x)
        return (count(lambda s, j: s >= cand) >= TOPK) | (x < KEY_OF_NEG_INF)

    thr_key = jnp.where(rank_reached(jnp.zeros((1, qb), jnp.int32)), 0, INT_MIN)

    def bit_body(t, thr_key):
        cand = thr_key + jnp.left_shift(jnp.int32(1), 30 - t)
        return jnp.where(rank_reached(cand), cand, thr_key)

    thr = key_to_float(lax.fori_loop(0, 31, bit_body, thr_key))

    n_gt = count(lambda s, j: s > thr)
    n_ge = count(lambda s, j: s >= thr)
    need = TOPK - n_gt

    jcut_ref[...] = jnp.full((1, qb), 1 << 30, jnp.int32)

    @pl.when(jnp.max(n_ge) > TOPK)
    def _():
        def idx_body(t, x):
            cand = x + jnp.left_shift(jnp.int32(1), 10 - t)
            below = count(lambda s, j: (s == thr) & (j * qb + row <= cand - 1))
            return jnp.where(below < need, cand, x)
        jcut_ref[...] = lax.fori_loop(0, 11, idx_body, jnp.zeros((1, qb), jnp.int32))

    jcut = jcut_ref[...]

    def write_chunk(j, carry):
        s = sc_ref[j]
        spos = j * qb + row
        sel = (s > thr) | ((s == thr) & (spos <= jcut))
        sel = sel & (spos <= qpos)
        start = pl.multiple_of(j * qb, qb)
        mask_ref[0, pl.ds(start, qb), :] = jnp.where(sel, 0.0, NEG_INF)
        return carry

    lax.fori_loop(0, nch, write_chunk, 0)

    def fill_chunk(j, carry):
        start = pl.multiple_of(j * qb, qb)
        mask_ref[0, pl.ds(start, qb), :] = jnp.full(shape, NEG_INF, F32)
        return carry

    lax.fori_loop(nch, nchunks_total, fill_chunk, 0)


def dsa_select(p1, qicol, kwcol):
    b, s, _ = p1.shape
    nq = s // DSA_QB
    return pl.pallas_call(
        functools.partial(_select_kernel, nchunks_total=nq),
        grid=(b, nq),
        in_specs=[pl.BlockSpec((1, DSA_QB, N_IDX_HEADS * IDX_DIM), lambda bi, i: (bi, i, qicol)),
                  pl.BlockSpec((1, s, 128), lambda bi, i: (bi, 0, kwcol)),
                  pl.BlockSpec((1, DSA_QB, 128), lambda bi, i: (bi, i, kwcol))],
        out_specs=pl.BlockSpec((1, s, DSA_QB), lambda bi, i: (bi, 0, i)),
        out_shape=jax.ShapeDtypeStruct((b, s, s), F32),
        scratch_shapes=[pltpu.VMEM((nq, DSA_QB, DSA_QB), F32),
                        pltpu.VMEM((1, DSA_QB), jnp.int32)],
        compiler_params=_params("parallel", "arbitrary"),
        name="dsa_select",
    )(p1, p1, p1)


def _dsa_attn_kernel(q_ref, c_ref, ct_ref, mask_ref, tz_ref, wuk_ref, wuvt_ref, o_ref,
                     ql_ref, ot_ref, acc_ref):
    qb = DSA_QB
    hg = DSA_HEAD_GROUP
    width = hg * qb
    i = pl.program_id(1)

    for h in range(N_HEADS):
        qh = q_ref[0, :, h * HEAD_DIM:(h + 1) * HEAD_DIM]
        ql = lax.dot_general(wuk_ref[h], qh, _NT, preferred_element_type=F32)
        ql_ref[h // hg, :, (h % hg) * qb:(h % hg + 1) * qb] = (
            ql * (HEAD_DIM ** -0.5 * LOG2_E)).astype(BF16)

    def group_body(g, carry):
        acc_ref[...] = jnp.zeros((KV_RANK, width), F32)

        def chunk(j, ml):
            m, l = ml
            s = jnp.dot(c_ref[0, j], ql_ref[g], preferred_element_type=F32)
            d = 2 * (i - j)
            start = pl.multiple_of(j * qb, qb)
            mk = mask_ref[0, pl.ds(start, qb), :]
            tiles = []
            for hh in range(hg):
                h = g * hg + hh
                t_diag = tz_ref[d, h]
                tiles.append(mk + jnp.concatenate(
                    [jnp.concatenate([t_diag, tz_ref[d + 1, h]], axis=1),
                     jnp.concatenate([tz_ref[jnp.maximum(d - 1, 0), h], t_diag], axis=1)], axis=0))
            s = s + jnp.concatenate(tiles, axis=1)
            m_new = jnp.maximum(m, jnp.max(s, axis=0, keepdims=True))
            p = jnp.exp2(s - m_new)
            alpha = jnp.exp2(m - m_new)
            pv = jnp.dot(ct_ref[0, j], p.astype(BF16), preferred_element_type=F32)
            l = alpha * l + pv[KV_RANK:KV_RANK + 1]
            acc_ref[...] = alpha * acc_ref[...] + pv[:KV_RANK]
            return m_new, l

        init = (jnp.full((1, width), M_INIT, F32), jnp.zeros((1, width), F32))
        _, l = lax.fori_loop(0, i + 1, chunk, init)
        o_lat = (acc_ref[...] / l).astype(BF16)
        for hh in range(hg):
            h = g * hg + hh
            ot_ref[h] = jnp.dot(wuvt_ref[h], o_lat[:, hh * qb:(hh + 1) * qb],
                                preferred_element_type=F32)
        return carry

    lax.fori_loop(0, N_HEADS // hg, group_body, 0)
    o_ref[0] = ot_ref[...].reshape(ATTN_W, qb).T.astype(o_ref.dtype)


def dsa_attention(p1, qcol, c, ct, mask, tz, wuk, wuvt):
    b, s, _ = p1.shape
    nq = s // DSA_QB
    const = lambda shape: pl.BlockSpec(shape, lambda bi, i: (0,) * len(shape))
    return pl.pallas_call(
        _dsa_attn_kernel,
        grid=(b, nq),
        in_specs=[pl.BlockSpec((1, DSA_QB, ATTN_W), lambda bi, i: (bi, i, qcol)),
                  pl.BlockSpec((1, nq, DSA_QB, KV_RANK), lambda bi, i: (bi, 0, 0, 0)),
                  pl.BlockSpec((1, nq, KV_RANK + ONES_ROWS, DSA_QB), lambda bi, i: (bi, 0, 0, 0)),
                  pl.BlockSpec((1, s, DSA_QB), lambda bi, i: (bi, 0, i)),
                  pl.BlockSpec(tz.shape, lambda bi, i: (0, 0, 0, 0), pipeline_mode=pl.Buffered(1)),
                  const(wuk.shape), const(wuvt.shape)],
        out_specs=pl.BlockSpec((1, DSA_QB, ATTN_W), lambda bi, i: (bi, i, 0)),
        out_shape=jax.ShapeDtypeStruct((b, s, ATTN_W), BF16),
        scratch_shapes=[pltpu.VMEM((N_HEADS // DSA_HEAD_GROUP, KV_RANK, DSA_HEAD_GROUP * DSA_QB), BF16),
                        pltpu.VMEM((N_HEADS, HEAD_DIM, DSA_QB), F32),
                        pltpu.VMEM((KV_RANK, DSA_HEAD_GROUP * DSA_QB), F32)],
        compiler_params=_params("parallel", "arbitrary"),
        name="dsa_attention",
    )(p1, c, ct, mask, tz, wuk, wuvt)


def _rmsnorm_kernel(x_ref, g_ref, o_ref):
    o_ref[...] = _rms(x_ref[...], g_ref[...])


def rmsnorm(x, g, tm):
    t, d = x.shape
    return pl.pallas_call(
        _rmsnorm_kernel,
        grid=(t // tm,),
        in_specs=[pl.BlockSpec((tm, d), lambda i: (i, 0)), pl.BlockSpec((1, d), lambda i: (0, 0))],
        out_specs=pl.BlockSpec((tm, d), lambda i: (i, 0)),
        out_shape=jax.ShapeDtypeStruct((t, d), F32),
        compiler_params=_params("parallel"),
        name="final_rmsnorm",
    )(x, g.reshape(1, d))


def _rel_bucket(dist):
    n = np.maximum(dist, 0)
    max_exact = REL_BUCKETS // 2
    nf = np.maximum(n, 1).astype(np.float32)
    large = max_exact + (np.log(nf / np.float32(max_exact)) / np.float32(math.log(REL_MAX_DIST / max_exact))
                         * np.float32(REL_BUCKETS - max_exact)).astype(np.int32)
    large = np.minimum(large, REL_BUCKETS - 1)
    return np.where(n < max_exact, n, large)


def _bias_table_kernel(bkt_ref, rb_ref, o_ref):
    bkt = bkt_ref[0]
    for h in range(N_HEADS):
        acc = jnp.full(bkt.shape, NEG_INF, F32)
        for k in range(REL_BUCKETS):
            acc = jnp.where(bkt == k, rb_ref[k, h], acc)
        o_ref[0, h] = acc


def bias_table(bkt, rel_bias):
    n, r, c = bkt.shape
    return pl.pallas_call(
        _bias_table_kernel,
        grid=(n,),
        in_specs=[pl.BlockSpec((1, r, c), lambda i: (i, 0, 0)),
                  pl.BlockSpec(memory_space=pltpu.SMEM)],
        out_specs=pl.BlockSpec((1, N_HEADS, r, c), lambda i: (i, 0, 0, 0)),
        out_shape=jax.ShapeDtypeStruct((n, N_HEADS, r, c), F32),
        compiler_params=_params("parallel"),
        name="bias_table",
    )(bkt, rel_bias.astype(F32))


def _dilated_buckets(window, dilation):
    qi = np.arange(BLOCK)[:, None]
    ki = np.arange(2 * BLOCK)[None, :]
    rel = qi + BLOCK - ki
    bkt = _rel_bucket(rel * dilation)
    band = (rel >= 0) & (rel <= window // dilation)
    return np.stack([np.where(band, bkt, -1),
                     np.where(band & (ki >= BLOCK), bkt, -1)]).astype(np.int32)


def _dsa_buckets(s):
    nd = s // BLOCK
    key = np.arange(BLOCK)[:, None]
    qry = np.arange(BLOCK)[None, :]
    dist = np.arange(nd)[:, None, None] * BLOCK + (qry - key)[None]
    return _rel_bucket(dist).astype(np.int32)


def kernel(x, mem, rel_bias, mem_norm, final_norm, mixer_norm, ffn_norm, w_mem_kv, w_out,
           even_w_in, even_w_gate, even_w_up, even_w_down,
           odd_w_in, odd_kv_norm, odd_w_uk, odd_w_uv,
           odd_w_router, odd_w_gate, odd_w_up, odd_w_down):
    b, s, d = x.shape
    t = b * s
    depth = mixer_norm.shape[0]
    xt = x.reshape(t, d)
    mem2 = mem.reshape(b * N_MEM, d)
    dil_tables = [bias_table(_dilated_buckets(w, dl), rel_bias) for w, dl in DIL_PAIRS]
    tz = bias_table(_dsa_buckets(s), rel_bias * LOG2_E)

    for i in range(depth):
        j = i // 2
        kvm = norm_matmul(mem2, mem_norm, w_mem_kv[i].astype(BF16), BF16, 512, 512)
        kvm = kvm.reshape(b, N_MEM, 2 * CROSS_WIDTH)
        wo = w_out[i].astype(BF16)
        if i % 2 == 0:
            w_in = even_w_in[j].astype(BF16)
            proj = even_in_projection(xt, mixer_norm[i], w_in[:, :5 * ATTN_W], b, s, 512)
            qc = norm_matmul(xt, mixer_norm[i], w_in[:, 5 * ATTN_W:], BF16, 512, CROSS_WIDTH)
            outs, lses = [], []
            for grp in range(len(DIL_PAIRS)):
                o, lse = dilated_branch(proj[grp], proj[3 + grp], proj[6 + grp], dil_tables[grp])
                outs.append(o)
                lses.append(lse)
            mix = combine_groups(outs, lses, 256).reshape(t, ATTN_W)
            cross = cross_attention(qc.reshape(b, s, CROSS_WIDTH), 0, kvm, 512)
        else:
            o_q, o_c, o_qi, o_ki, o_wi, o_qc = 0, 1024, 1280, 1792, 1856, 1864
            w = odd_w_in[j]
            pad = jnp.zeros((d, 128 - IDX_DIM - N_IDX_HEADS), w.dtype)
            w_in = jnp.concatenate(
                [w[:, o_q:o_c], w[:, o_qi:o_ki], w[:, o_c:o_qi], w[:, o_qc:],
                 w[:, o_ki:o_wi], w[:, o_wi:o_qc], pad], axis=1).astype(BF16)
            p1 = norm_matmul(xt, mixer_norm[i], w_in, BF16, 512, w_in.shape[1])
            p1 = p1.reshape(b, s, w_in.shape[1])
            c, ct = kv_latent_norm(p1, 6, odd_kv_norm[j])
            mask = dsa_select(p1, 2, 16)
            wuk = jnp.transpose(odd_w_uk[j], (1, 0, 2)).astype(BF16)
            wuvt = jnp.transpose(odd_w_uv[j], (1, 2, 0)).astype(BF16)
            mix = dsa_attention(p1, 0, c, ct, mask, tz, wuk, wuvt).reshape(t, ATTN_W)
            cross = cross_attention(p1, 7, kvm, 512)
        xt = out_projection(mix, cross.reshape(t, CROSS_WIDTH), wo[:ATTN_W], wo[ATTN_W:], xt, 512)
        if i % 2 == 0:
            xt = swiglu_ffn(xt, ffn_norm[i], even_w_gate[j].astype(BF16), even_w_up[j].astype(BF16),
                            even_w_down[j].astype(BF16), 512, even_w_gate.shape[2])
        else:
            wr = jnp.pad(odd_w_router[j], ((0, 0), (0, 128 - N_EXPERTS)))
            xt = moe_ffn(xt, ffn_norm[i], wr, odd_w_gate[j].astype(BF16), odd_w_up[j].astype(BF16),
                         odd_w_down[j].astype(BF16), final_norm if i == depth - 1 else None)
    if depth % 2 == 1:
        xt = rmsnorm(xt, final_norm, 512)
    return xt.reshape(b, s, d)
```

```python
import functools
import math

import jax
import jax.numpy as jnp
import numpy as np
from jax import lax
from jax.experimental import pallas as pl
from jax.experimental.pallas import tpu as pltpu

D_MODEL = 1024
N_HEADS = 16
HEAD_DIM = 64
ATTN_W = N_HEADS * HEAD_DIM
DIL_PAIRS = ((128, 1), (512, 4), (2048, 16))
BLOCK = 128
LANES = 128
KV_RANK = 256
N_IDX_HEADS = 8
IDX_DIM = 64
TOPK = 256
N_MEM = 256
N_CROSS_HEADS = 4
CROSS_WIDTH = 256
REL_BUCKETS = 32
REL_MAX_DIST = 2048
N_EXPERTS = 8
RMS_EPS = 1e-6
NEG_INF = -1e30
M_INIT = -5e29
LOG2_E = math.log2(math.e)

F32 = jnp.float32
BF16 = jnp.bfloat16
V7X_VMEM_LIMIT = 56 * 1024 * 1024
DSA_QB = 256
DSA_HEAD_GROUP = 8
ONES_ROWS = 16
INT_MIN = -(2 ** 31)
KEY_OF_NEG_INF = (0xFF800000 ^ 0x7FFFFFFF) - (1 << 32)
MOE_TILE = 512
ROUTE_TM = 256
MOE_FF_CHUNK = 1792

_NT = (((1,), (1,)), ((), ()))


def _params(*sem):
    return pltpu.CompilerParams(dimension_semantics=sem, vmem_limit_bytes=V7X_VMEM_LIMIT)


def _rms(x, g):
    return x * lax.rsqrt(jnp.mean(x * x, axis=-1, keepdims=True) + RMS_EPS) * g


def _norm_mm_kernel(x_ref, g_ref, w_ref, o_ref, xn_ref):
    @pl.when(pl.program_id(1) == 0)
    def _():
        xn_ref[...] = _rms(x_ref[...].astype(F32), g_ref[...]).astype(BF16)

    o_ref[...] = jnp.dot(xn_ref[...], w_ref[...], preferred_element_type=F32).astype(o_ref.dtype)


def norm_matmul(x, g, w, out_dtype, tm, tn):
    t, k = x.shape
    n = w.shape[1]
    return pl.pallas_call(
        _norm_mm_kernel,
        grid=(t // tm, n // tn),
        in_specs=[pl.BlockSpec((tm, k), lambda i, j: (i, 0)),
                  pl.BlockSpec((1, k), lambda i, j: (0, 0)),
                  pl.BlockSpec((k, tn), lambda i, j: (0, j))],
        out_specs=pl.BlockSpec((tm, tn), lambda i, j: (i, j)),
        out_shape=jax.ShapeDtypeStruct((t, n), out_dtype),
        scratch_shapes=[pltpu.VMEM((tm, k), BF16)],
        compiler_params=_params("parallel", "arbitrary"),
        name="norm_matmul",
    )(x, g.reshape(1, k), w)


def _even_proj_kernel(x_ref, g_ref, w_ref, q1_ref, q4_ref, q16_ref, k1_ref, k4_ref, k16_ref,
                      v1_ref, v4_ref, v16_ref, xn_ref, y_ref):
    j = pl.program_id(1)
    tm = x_ref.shape[0]

    @pl.when(j == 0)
    def _():
        xn_ref[...] = _rms(x_ref[...], g_ref[...]).astype(BF16)

    y = jnp.dot(xn_ref[...], w_ref[...], preferred_element_type=F32)
    n_lane_blocks = y.shape[1] // LANES
    for c in range(n_lane_blocks):
        y_ref[c] = y[:, c * LANES:(c + 1) * LANES]

    def put(dst_ref, d):
        if d == 1:
            dst_ref[0, 0] = y.astype(BF16)
            return
        for r in range(d):
            for c in range(n_lane_blocks):
                rows = y_ref[c, pl.ds(r, tm // d, stride=d), :]
                dst_ref[0, r, :, c * LANES:(c + 1) * LANES] = rows.astype(BF16)

    column_dsts = (((q1_ref, 1),), ((q4_ref, 4),), ((q16_ref, 16),),
                   ((k1_ref, 1), (k4_ref, 4), (k16_ref, 16)),
                   ((v1_ref, 1), (v4_ref, 4), (v16_ref, 16)))
    for col, dsts in enumerate(column_dsts):
        @pl.when(j == col)
        def _(dsts=dsts):
            for ref, d in dsts:
                put(ref, d)


def even_in_projection(x, g, w, b, s, tm):
    t, kdim = x.shape
    per_b = s // tm
    dils = [dl for _, dl in DIL_PAIRS]
    layouts = dils + dils + dils
    spec = lambda dl: pl.BlockSpec((1, dl, tm // dl, ATTN_W), lambda i, j: (i // per_b, 0, i % per_b, 0))
    return pl.pallas_call(
        _even_proj_kernel,
        grid=(t // tm, 5),
        in_specs=[pl.BlockSpec((tm, kdim), lambda i, j: (i, 0)),
                  pl.BlockSpec((1, kdim), lambda i, j: (0, 0)),
                  pl.BlockSpec((kdim, ATTN_W), lambda i, j: (0, j))],
        out_specs=[spec(dl) for dl in layouts],
        out_shape=[jax.ShapeDtypeStruct((b, dl, s // dl, ATTN_W), BF16) for dl in layouts],
        scratch_shapes=[pltpu.VMEM((tm, kdim), BF16), pltpu.VMEM((ATTN_W // LANES, tm, LANES), F32)],
        compiler_params=_params("parallel", "arbitrary"),
        name="even_in_projection",
    )(x, g.reshape(1, kdim), w)


def _dil_kernel(q_ref, kp_ref, kc_ref, vp_ref, vc_ref, tab_ref, o_ref, lse_ref, s_ref, p_ref, m_ref):
    first = (pl.program_id(2) == 0).astype(jnp.int32)
    pair = 2 * HEAD_DIM
    lo_q = lax.broadcasted_iota(jnp.int32, (BLOCK, pair), 1) < HEAD_DIM
    lo_k = lax.broadcasted_iota(jnp.int32, (2 * BLOCK, pair), 1) < HEAD_DIM
    ones_bd = jnp.concatenate([jnp.where(lo_k, 1.0, 0.0), jnp.where(lo_k, 0.0, 1.0)], axis=0).astype(BF16)
    scale = jnp.asarray(HEAD_DIM ** -0.5, BF16)
    zero = jnp.zeros((), BF16)
    n_pairs = N_HEADS // 2
    for p in range(n_pairs):
        cols = slice(p * pair, (p + 1) * pair)
        q = q_ref[0, 0, :, cols] * scale
        k = jnp.concatenate([kp_ref[0, 0, :, cols], kc_ref[0, 0, :, cols]], axis=0)
        q_ab = jnp.concatenate([jnp.where(lo_q, q, zero), jnp.where(lo_q, zero, q)], axis=0)
        s_ab = lax.dot_general(q_ab, k, _NT, preferred_element_type=F32)
        s_ref[2 * p] = s_ab[:BLOCK] + tab_ref[first, 2 * p]
        s_ref[2 * p + 1] = s_ab[BLOCK:] + tab_ref[first, 2 * p + 1]
    for p in range(n_pairs):
        sa = s_ref[2 * p]
        sb = s_ref[2 * p + 1]
        ma = jnp.max(sa, axis=-1, keepdims=True)
        mb = jnp.max(sb, axis=-1, keepdims=True)
        p_ref[p, :, :2 * BLOCK] = jnp.exp(sa - ma).astype(BF16)
        p_ref[p, :, 2 * BLOCK:] = jnp.exp(sb - mb).astype(BF16)
        m_ref[p] = jnp.where(lo_q, ma, mb)
    for p in range(n_pairs):
        cols = slice(p * pair, (p + 1) * pair)
        v = jnp.concatenate([vp_ref[0, 0, :, cols], vc_ref[0, 0, :, cols]], axis=0)
        v_bd = jnp.concatenate([jnp.where(lo_k, v, zero), jnp.where(lo_k, zero, v)], axis=0)
        ol = jnp.dot(p_ref[p], jnp.concatenate([v_bd, ones_bd], axis=1), preferred_element_type=F32)
        o, l = ol[:, :pair], ol[:, pair:]
        o_ref[0, 0, :, cols] = (o / l).astype(o_ref.dtype)
        lse_ref[0, 0, :, cols] = m_ref[p] + jnp.log(l)


def dilated_branch(q, k, v, table):
    b, dilation, ln, _ = q.shape
    nb = ln // BLOCK
    blk = (1, 1, BLOCK, ATTN_W)
    cur = pl.BlockSpec(blk, lambda bi, r, n: (bi, r, n, 0))
    prev = pl.BlockSpec(blk, lambda bi, r, n: (bi, r, jnp.maximum(n - 1, 0), 0))
    return pl.pallas_call(
        _dil_kernel,
        grid=(b, dilation, nb),
        in_specs=[cur, prev, cur, prev, cur,
                  pl.BlockSpec(table.shape, lambda bi, r, n: (0, 0, 0, 0))],
        out_specs=[cur, cur],
        out_shape=[jax.ShapeDtypeStruct(q.shape, BF16), jax.ShapeDtypeStruct(q.shape, F32)],
        scratch_shapes=[pltpu.VMEM((N_HEADS, BLOCK, 2 * BLOCK), F32),
                        pltpu.VMEM((N_HEADS // 2, BLOCK, 4 * BLOCK), BF16),
                        pltpu.VMEM((N_HEADS // 2, BLOCK, 2 * HEAD_DIM), F32)],
        compiler_params=_params("parallel", "parallel", "arbitrary"),
        name=f"dilated_attn_d{dilation}",
    )(q, k, k, v, v, table)


def _combine_kernel(o1, o4, o16, l1, l4, l16, out_ref, so4, sl4, so16, sl16):
    tm = out_ref.shape[1]
    n_lane_blocks = out_ref.shape[2] // LANES

    def to_token_order(src_ref, dst_ref, d):
        for r in range(d):
            for c in range(n_lane_blocks):
                rows = src_ref[0, r, :, c * LANES:(c + 1) * LANES]
                dst_ref[c, pl.ds(r, tm // d, stride=d), :] = rows.astype(F32)

    to_token_order(o4, so4, 4)
    to_token_order(l4, sl4, 4)
    to_token_order(o16, so16, 16)
    to_token_order(l16, sl16, 16)
    for c in range(n_lane_blocks):
        cols = slice(c * LANES, (c + 1) * LANES)
        a1, a2, a3 = l1[0, 0, :, cols], sl4[c], sl16[c]
        m = jnp.maximum(jnp.maximum(a1, a2), a3)
        w1, w2, w3 = jnp.exp(a1 - m), jnp.exp(a2 - m), jnp.exp(a3 - m)
        num = w1 * o1[0, 0, :, cols] + w2 * so4[c] + w3 * so16[c]
        out_ref[0, :, cols] = (num / (w1 + w2 + w3)).astype(out_ref.dtype)


def combine_groups(outs, lses, tm):
    b, _, s, w = outs[0].shape
    spec = lambda dl: pl.BlockSpec((1, dl, tm // dl, w), lambda bi, i: (bi, 0, i, 0))
    specs = [spec(o.shape[1]) for o in outs]
    return pl.pallas_call(
        _combine_kernel,
        grid=(b, s // tm),
        in_specs=specs + specs,
        out_specs=pl.BlockSpec((1, tm, w), lambda bi, i: (bi, i, 0)),
        out_shape=jax.ShapeDtypeStruct((b, s, w), BF16),
        scratch_shapes=[pltpu.VMEM((w // LANES, tm, LANES), F32)] * 4,
        compiler_params=_params("parallel", "parallel"),
        name="combine_groups",
    )(*outs, *lses)


def _cross_kernel(q_ref, kv_ref, o_ref):
    for h in range(N_CROSS_HEADS):
        sl = slice(h * HEAD_DIM, (h + 1) * HEAD_DIM)
        vsl = slice(CROSS_WIDTH + h * HEAD_DIM, CROSS_WIDTH + (h + 1) * HEAD_DIM)
        s = lax.dot_general(q_ref[0, :, sl], kv_ref[0, :, sl], _NT,
                            preferred_element_type=F32) * (HEAD_DIM ** -0.5)
        m = jnp.max(s, axis=-1, keepdims=True)
        p = jnp.exp(s - m)
        l = jnp.sum(p, axis=-1, keepdims=True)
        o = jnp.dot(p.astype(BF16), kv_ref[0, :, vsl], preferred_element_type=F32)
        o_ref[0, :, sl] = (o / l).astype(o_ref.dtype)


def cross_attention(qsrc, qcol, kv, tm):
    b, s, _ = qsrc.shape
    return pl.pallas_call(
        _cross_kernel,
        grid=(b, s // tm),
        in_specs=[pl.BlockSpec((1, tm, CROSS_WIDTH), lambda bi, i: (bi, i, qcol)),
                  pl.BlockSpec((1, N_MEM, 2 * CROSS_WIDTH), lambda bi, i: (bi, 0, 0))],
        out_specs=pl.BlockSpec((1, tm, CROSS_WIDTH), lambda bi, i: (bi, i, 0)),
        out_shape=jax.ShapeDtypeStruct((b, s, CROSS_WIDTH), BF16),
        compiler_params=_params("parallel", "parallel"),
        name="cross_attn",
    )(qsrc, kv)


def _outproj_kernel(mix_ref, cr_ref, wa_ref, wb_ref, x_ref, o_ref):
    acc = jnp.dot(mix_ref[...], wa_ref[...], preferred_element_type=F32)
    acc = acc + jnp.dot(cr_ref[...], wb_ref[...], preferred_element_type=F32)
    o_ref[...] = x_ref[...] + acc


def out_projection(mix, cross, wa, wb, x, tm):
    t, d = x.shape
    return pl.pallas_call(
        _outproj_kernel,
        grid=(t // tm,),
        in_specs=[pl.BlockSpec((tm, ATTN_W), lambda i: (i, 0)),
                  pl.BlockSpec((tm, CROSS_WIDTH), lambda i: (i, 0)),
                  pl.BlockSpec(wa.shape, lambda i: (0, 0)),
                  pl.BlockSpec(wb.shape, lambda i: (0, 0)),
                  pl.BlockSpec((tm, d), lambda i: (i, 0))],
        out_specs=pl.BlockSpec((tm, d), lambda i: (i, 0)),
        out_shape=jax.ShapeDtypeStruct((t, d), F32),
        compiler_params=_params("parallel"),
        name="out_projection",
    )(mix, cross, wa, wb, x)


def _swiglu_kernel(x_ref, g_ref, wg_ref, wu_ref, wd_ref, o_ref, hn_ref):
    @pl.when(pl.program_id(1) == 0)
    def _():
        x = x_ref[...]
        hn_ref[...] = _rms(x, g_ref[...]).astype(BF16)
        o_ref[...] = x

    hn = hn_ref[...]
    a = jnp.dot(hn, wg_ref[...], preferred_element_type=F32)
    u = jnp.dot(hn, wu_ref[...], preferred_element_type=F32)
    act = (a * jax.nn.sigmoid(a) * u).astype(BF16)
    o_ref[...] += jnp.dot(act, wd_ref[...], preferred_element_type=F32)


def swiglu_ffn(x, g, wg, wu, wd, tm, tf):
    t, d = x.shape
    f = wg.shape[1]
    mode = dict(pipeline_mode=pl.Buffered(1)) if tf == f else {}
    return pl.pallas_call(
        _swiglu_kernel,
        grid=(t // tm, f // tf),
        in_specs=[pl.BlockSpec((tm, d), lambda i, j: (i, 0)),
                  pl.BlockSpec((1, d), lambda i, j: (0, 0)),
                  pl.BlockSpec((d, tf), lambda i, j: (0, j), **mode),
                  pl.BlockSpec((d, tf), lambda i, j: (0, j), **mode),
                  pl.BlockSpec((tf, d), lambda i, j: (j, 0), **mode)],
        out_specs=pl.BlockSpec((tm, d), lambda i, j: (i, 0)),
        out_shape=jax.ShapeDtypeStruct((t, d), F32),
        scratch_shapes=[pltpu.VMEM((tm, d), BF16)],
        compiler_params=_params("parallel", "arbitrary"),
        name="swiglu_ffn",
    )(x, g.reshape(1, d), wg, wu, wd)


def _router_kernel(x_ref, g_ref, wr_ref, gates_ref, route_ref, counts_ref, run_ref):
    @pl.when(pl.program_id(0) == 0)
    def _():
        run_ref[...] = jnp.zeros(run_ref.shape, F32)

    hn = _rms(x_ref[...], g_ref[...])
    logits = jnp.dot(hn, wr_ref[...], preferred_element_type=F32,
                     precision=lax.Precision.HIGHEST)
    tm = logits.shape[0]
    lane = lax.broadcasted_iota(jnp.int32, logits.shape, 1)
    lg = jnp.where(lane < N_EXPERTS, logits, -jnp.inf)
    m1 = jnp.max(lg, axis=-1, keepdims=True)
    i1 = jnp.min(jnp.where(lg == m1, lane, 128), axis=-1, keepdims=True)
    lg2 = jnp.where(lane == i1, -jnp.inf, lg)
    m2 = jnp.max(lg2, axis=-1, keepdims=True)
    i2 = jnp.min(jnp.where(lg2 == m2, lane, 128), axis=-1, keepdims=True)
    e = jnp.exp(m2 - m1)
    gates_ref[...] = jnp.where(lane == 0, 1.0 / (1.0 + e), jnp.where(lane == 1, e / (1.0 + e), 0.0))

    assign = jnp.where((lane == i1) | (lane == i2), 1.0, 0.0)
    r = lax.broadcasted_iota(jnp.int32, (tm, tm), 0)
    c = lax.broadcasted_iota(jnp.int32, (tm, tm), 1)
    lower = jnp.where(r > c, 1.0, 0.0).astype(BF16)
    before = jnp.dot(lower, assign.astype(BF16), preferred_element_type=F32) + run_ref[...]
    rank1 = jnp.sum(jnp.where(lane == i1, before, 0.0), axis=-1, keepdims=True).astype(jnp.int32)
    rank2 = jnp.sum(jnp.where(lane == i2, before, 0.0), axis=-1, keepdims=True).astype(jnp.int32)
    route_ref[...] = jnp.where(lane == 0, i1, jnp.where(lane == 1, i2, jnp.where(
        lane == 2, rank1, jnp.where(lane == 3, rank2, 0))))
    run = run_ref[...] + jnp.sum(assign, axis=0, keepdims=True)
    run_ref[...] = run
    counts_ref[...] = run.astype(jnp.int32)


def router(x, g, wr_pad, tm):
    t, d = x.shape
    return pl.pallas_call(
        _router_kernel,
        grid=(t // tm,),
        in_specs=[pl.BlockSpec((tm, d), lambda i: (i, 0)),
                  pl.BlockSpec((1, d), lambda i: (0, 0)),
                  pl.BlockSpec((d, 128), lambda i: (0, 0))],
        out_specs=[pl.BlockSpec((tm, 128), lambda i: (i, 0)),
                   pl.BlockSpec((tm, 128), lambda i: (i, 0)),
                   pl.BlockSpec((1, 128), lambda i: (0, 0))],
        out_shape=[jax.ShapeDtypeStruct((t, 128), F32),
                   jax.ShapeDtypeStruct((t, 128), jnp.int32),
                   jax.ShapeDtypeStruct((1, 128), jnp.int32)],
        scratch_shapes=[pltpu.VMEM((1, 128), F32)],
        compiler_params=_params("arbitrary"),
        name="router",
    )(x, g.reshape(1, d), wr_pad)


def _row_copy(src, src_row, dst, dst_row, sem):
    return pltpu.make_async_copy(src.at[pl.ds(src_row, 1)], dst.at[pl.ds(dst_row, 1)], sem)


def _scatter_rows_kernel(d1_ref, d2_ref, x_ref, xs_in_ref, xs_ref, sem):
    del xs_in_ref
    tm = x_ref.shape[0]
    base = pl.program_id(0) * tm

    def issue(r, carry):
        _row_copy(x_ref, r, xs_ref, d1_ref[base + r], sem).start()
        _row_copy(x_ref, r, xs_ref, d2_ref[base + r], sem).start()
        return carry

    lax.fori_loop(0, tm, issue, 0, unroll=8)

    for _ in range(2):
        pltpu.make_async_copy(x_ref, xs_ref.at[pl.ds(0, tm)], sem).wait()


def scatter_rows(x, dest1, dest2, n_rows, tm):
    t, d = x.shape
    zeros = jnp.zeros((n_rows, d), x.dtype)
    return pl.pallas_call(
        _scatter_rows_kernel,
        grid_spec=pltpu.PrefetchScalarGridSpec(
            num_scalar_prefetch=2,
            grid=(t // tm,),
            in_specs=[pl.BlockSpec((tm, d), lambda i, d1, d2: (i, 0)),
                      pl.BlockSpec(memory_space=pl.ANY)],
            out_specs=pl.BlockSpec(memory_space=pl.ANY),
            scratch_shapes=[pltpu.SemaphoreType.DMA(())]),
        out_shape=jax.ShapeDtypeStruct((n_rows, d), x.dtype),
        input_output_aliases={3: 0},
        compiler_params=_params("arbitrary"),
        name="moe_scatter_rows",
    )(dest1, dest2, x, zeros)


def _moe_group_kernel(te_ref, nu_ref, x_ref, g_ref, wg_ref, wu_ref, wd_ref, o_ref, hn_ref):
    i = pl.program_id(0)
    j = pl.program_id(1)
    used = i < nu_ref[0]

    @pl.when(j == 0)
    def _():
        o_ref[...] = jnp.zeros(o_ref.shape, F32)

    @pl.when(used & (j == 0))
    def _():
        hn_ref[...] = _rms(x_ref[...], g_ref[...]).astype(BF16)

    @pl.when(used)
    def _():
        hn = hn_ref[...]
        a = jnp.dot(hn, wg_ref[...], preferred_element_type=F32)
        u = jnp.dot(hn, wu_ref[...], preferred_element_type=F32)
        act = (a * jax.nn.sigmoid(a) * u).astype(BF16)
        o_ref[...] += jnp.dot(act, wd_ref[...], preferred_element_type=F32)


def moe_group_ffn(xs, g, tile_expert, n_used, wg, wu, wd, tm, tf):
    p, d = xs.shape
    f = wg.shape[2]
    nj = f // tf
    chunk = lambda i, j, te, nu: jnp.where(i < nu[0], j, nj - 1)
    return pl.pallas_call(
        _moe_group_kernel,
        grid_spec=pltpu.PrefetchScalarGridSpec(
            num_scalar_prefetch=2,
            grid=(p // tm, nj),
            in_specs=[pl.BlockSpec((tm, d), lambda i, j, te, nu: (i, 0)),
                      pl.BlockSpec((1, d), lambda i, j, te, nu: (0, 0)),
                      pl.BlockSpec((None, d, tf), lambda i, j, te, nu: (te[i], 0, chunk(i, j, te, nu))),
                      pl.BlockSpec((None, d, tf), lambda i, j, te, nu: (te[i], 0, chunk(i, j, te, nu))),
                      pl.BlockSpec((None, tf, d), lambda i, j, te, nu: (te[i], chunk(i, j, te, nu), 0))],
            out_specs=pl.BlockSpec((tm, d), lambda i, j, te, nu: (i, 0)),
            scratch_shapes=[pltpu.VMEM((tm, d), BF16)]),
        out_shape=jax.ShapeDtypeStruct((p, d), F32),
        compiler_params=_params("arbitrary", "arbitrary"),
        name="moe_group_ffn",
    )(tile_expert, n_used, xs, g.reshape(1, d), wg, wu, wd)


def _moe_combine_kernel(d1_ref, d2_ref, x_ref, gates_ref, fg_ref, ys_ref, o_ref, y1_ref, y2_ref, sem,
                        *, final_norm):
    tm = x_ref.shape[0]
    i = pl.program_id(0)

    def issue(tile, slot):
        base = tile * tm

        def body(r, carry):
            _row_copy(ys_ref, d1_ref[base + r], y1_ref.at[slot], r, sem.at[slot]).start()
            _row_copy(ys_ref, d2_ref[base + r], y2_ref.at[slot], r, sem.at[slot]).start()
            return carry

        lax.fori_loop(0, tm, body, 0, unroll=8)

    @pl.when(i == 0)
    def _():
        issue(0, 0)

    @pl.when(i + 1 < pl.num_programs(0))
    def _():
        issue(i + 1, (i + 1) % 2)

    slot = i % 2
    pltpu.make_async_copy(ys_ref.at[pl.ds(0, tm)], y1_ref.at[slot], sem.at[slot]).wait()
    pltpu.make_async_copy(ys_ref.at[pl.ds(0, tm)], y2_ref.at[slot], sem.at[slot]).wait()
    gates = gates_ref[...]
    out = x_ref[...] + gates[:, 0:1] * y1_ref[slot] + gates[:, 1:2] * y2_ref[slot]
    if final_norm:
        out = _rms(out, fg_ref[...])
    o_ref[...] = out


def moe_combine(x, gates, ys, dest1, dest2, tm, final_gain=None):
    t, d = x.shape
    fg = jnp.ones((1, d), F32) if final_gain is None else final_gain.reshape(1, d)
    return pl.pallas_call(
        functools.partial(_moe_combine_kernel, final_norm=final_gain is not None),
        grid_spec=pltpu.PrefetchScalarGridSpec(
            num_scalar_prefetch=2,
            grid=(t // tm,),
            in_specs=[pl.BlockSpec((tm, d), lambda i, d1, d2: (i, 0)),
                      pl.BlockSpec((tm, 128), lambda i, d1, d2: (i, 0)),
                      pl.BlockSpec((1, d), lambda i, d1, d2: (0, 0)),
                      pl.BlockSpec(memory_space=pl.ANY)],
            out_specs=pl.BlockSpec((tm, d), lambda i, d1, d2: (i, 0)),
            scratch_shapes=[pltpu.VMEM((2, tm, d), F32), pltpu.VMEM((2, tm, d), F32),
                            pltpu.SemaphoreType.DMA((2,))]),
        out_shape=jax.ShapeDtypeStruct((t, d), F32),
        compiler_params=_params("arbitrary"),
        name="moe_combine",
    )(dest1, dest2, x, gates, fg, ys)


def moe_ffn(x, g, wr_pad, wg, wu, wd, final_gain=None):
    t, d = x.shape
    ne = wg.shape[0]
    gates, route, counts = router(x, g, wr_pad, ROUTE_TM)
    counts = counts[0, :ne]
    padded = (counts + MOE_TILE - 1) // MOE_TILE * MOE_TILE
    ends = jnp.cumsum(padded)
    starts = ends - padded
    expert_ids = jnp.arange(ne, dtype=jnp.int32)[None, :]
    start_of = lambda e: jnp.sum(jnp.where(e[:, None] == expert_ids, starts[None, :], 0), axis=1)
    dest1 = (start_of(route[:, 0]) + route[:, 2]).astype(jnp.int32)
    dest2 = (start_of(route[:, 1]) + route[:, 3]).astype(jnp.int32)
    n_tiles = (2 * t) // MOE_TILE + ne
    tile_start = jnp.arange(n_tiles, dtype=jnp.int32) * MOE_TILE
    tile_expert = jnp.minimum(jnp.sum(tile_start[:, None] >= ends[None, :], axis=1), ne - 1).astype(jnp.int32)
    n_used = (ends[-1:] // MOE_TILE).astype(jnp.int32)
    xs = scatter_rows(x, dest1, dest2, n_tiles * MOE_TILE, ROUTE_TM)
    ys = moe_group_ffn(xs, g, tile_expert, n_used, wg, wu, wd, MOE_TILE, MOE_FF_CHUNK)
    return moe_combine(x, gates, ys, dest1, dest2, ROUTE_TM, final_gain)


def _kvnorm_kernel(c_ref, g_ref, o_ref, ot_ref):
    y = _rms(c_ref[0].astype(F32), g_ref[...])
    o_ref[0, 0] = y.astype(BF16)
    ot_ref[0, 0, :KV_RANK] = y.T.astype(BF16)
    ot_ref[0, 0, KV_RANK:] = jnp.ones((ONES_ROWS, DSA_QB), BF16)


def kv_latent_norm(p1, ccol, g):
    b, s, _ = p1.shape
    nc = s // DSA_QB
    out = jax.ShapeDtypeStruct((b, nc, DSA_QB, KV_RANK), BF16)
    out_t = jax.ShapeDtypeStruct((b, nc, KV_RANK + ONES_ROWS, DSA_QB), BF16)
    blk = pl.BlockSpec((1, 1, DSA_QB, KV_RANK), lambda bi, j: (bi, j, 0, 0))
    blk_t = pl.BlockSpec((1, 1, KV_RANK + ONES_ROWS, DSA_QB), lambda bi, j: (bi, j, 0, 0))
    return pl.pallas_call(
        _kvnorm_kernel,
        grid=(b, nc),
        in_specs=[pl.BlockSpec((1, DSA_QB, KV_RANK), lambda bi, j: (bi, j, ccol)),
                  pl.BlockSpec((1, KV_RANK), lambda bi, j: (0, 0))],
        out_specs=[blk, blk_t],
        out_shape=[out, out_t],
        compiler_params=_params("parallel", "parallel"),
        name="kv_latent_norm",
    )(p1, g.reshape(1, KV_RANK))


def _select_kernel(qi_ref, kall_ref, kq_ref, mask_ref, sc_ref, jcut_ref, *, nchunks_total):
    qb = DSA_QB
    i = pl.program_id(1)
    nch = i + 1
    shape = (qb, qb)
    row = lax.broadcasted_iota(jnp.int32, shape, 0)
    qpos = i * qb + lax.broadcasted_iota(jnp.int32, shape, 1)

    wt = kq_ref[0].astype(F32).T
    wscale = (N_IDX_HEADS ** -0.5) * (IDX_DIM ** -0.5)

    def score_chunk(j, carry):
        start = pl.multiple_of(j * qb, qb)
        kc = kall_ref[0, pl.ds(start, qb), :][:, :IDX_DIM]
        acc = jnp.zeros(shape, F32)
        for h in range(N_IDX_HEADS):
            qh = qi_ref[0, :, h * IDX_DIM:(h + 1) * IDX_DIM]
            s = lax.dot_general(kc, qh, _NT, preferred_element_type=F32)
            acc = acc + jnp.maximum(s, 0.0) * (wt[IDX_DIM + h:IDX_DIM + h + 1, :] * wscale)
        sc_ref[j] = jnp.where(j * qb + row <= qpos, acc, NEG_INF)
        return carry

    lax.fori_loop(0, nch, score_chunk, 0)

    def count(pred):
        def body(j, c):
            hit = jnp.where(pred(sc_ref[j], j), 1, 0)
            return c + jnp.sum(hit.reshape(qb // 8, 8, qb), axis=0)
        c = lax.fori_loop(0, nch, body, jnp.zeros((8, qb), jnp.int32))
        return jnp.sum(c, axis=0, keepdims=True)

    def key_to_float(x):
        return pltpu.bitcast(x ^ ((x >> 31) & 0x7FFFFFFF), F32)

    def rank_reached(x):
        cand = key_to_float(x)
        return (count(lambda s, j: s >= cand) >= TOPK) | (x < KEY_OF_NEG_INF)

    thr_key = jnp.where(rank_reached(jnp.zeros((1, qb), jnp.int32)), 0, INT_MIN)

    def bit_body(t, thr_key):
        cand = thr_key + jnp.left_shift(jnp.int32(1), 30 - t)
        return jnp.where(rank_reached(cand), cand, thr_key)

    thr = key_to_float(lax.fori_loop(0, 31, bit_body, thr_key))

    n_gt = count(lambda s, j: s > thr)
    n_ge = count(lambda s, j: s >= thr)
    need = TOPK - n_gt

    jcut_ref[...] = jnp.full((1, qb), 1 << 30, jnp.int32)

    @pl.when(jnp.max(n_ge) > TOPK)
    def _():
        def idx_body(t, x):
            cand = x + jnp.left_shift(jnp.int32(1), 10 - t)
            below = count(lambda s, j: (s == thr) & (j * qb + row <= cand - 1))
            return jnp.where(below < need, cand, x)
        jcut_ref[...] = lax.fori_loop(0, 11, idx_body, jnp.zeros((1, qb), jnp.int32))

    jcut = jcut_ref[...]

    def write_chunk(j, carry):
        s = sc_ref[j]
        spos = j * qb + row
        sel = (s > thr) | ((s == thr) & (spos <= jcut))
        sel = sel & (spos <= qpos)
        start = pl.multiple_of(j * qb, qb)
        mask_ref[0, pl.ds(start, qb), :] = jnp.where(sel, 0.0, NEG_INF)
        return carry

    lax.fori_loop(0, nch, write_chunk, 0)

    def fill_chunk(j, carry):
        start = pl.multiple_of(j * qb, qb)
        mask_ref[0, pl.ds(start, qb), :] = jnp.full(shape, NEG_INF, F32)
        return carry

    lax.fori_loop(nch, nchunks_total, fill_chunk, 0)


def dsa_select(p1, qicol, kwcol):
    b, s, _ = p1.shape
    nq = s // DSA_QB
    return pl.pallas_call(
        functools.partial(_select_kernel, nchunks_total=nq),
        grid=(b, nq),
        in_specs=[pl.BlockSpec((1, DSA_QB, N_IDX_HEADS * IDX_DIM), lambda bi, i: (bi, i, qicol)),
                  pl.BlockSpec((1, s, 128), lambda bi, i: (bi, 0, kwcol)),
                  pl.BlockSpec((1, DSA_QB, 128), lambda bi, i: (bi, i, kwcol))],
        out_specs=pl.BlockSpec((1, s, DSA_QB), lambda bi, i: (bi, 0, i)),
        out_shape=jax.ShapeDtypeStruct((b, s, s), F32),
        scratch_shapes=[pltpu.VMEM((nq, DSA_QB, DSA_QB), F32),
                        pltpu.VMEM((1, DSA_QB), jnp.int32)],
        compiler_params=_params("parallel", "arbitrary"),
        name="dsa_select",
    )(p1, p1, p1)


def _dsa_attn_kernel(q_ref, c_ref, ct_ref, mask_ref, tz_ref, wuk_ref, wuvt_ref, o_ref,
                     ql_ref, ot_ref, acc_ref):
    qb = DSA_QB
    hg = DSA_HEAD_GROUP
    width = hg * qb
    i = pl.program_id(1)

    for h in range(N_HEADS):
        qh = q_ref[0, :, h * HEAD_DIM:(h + 1) * HEAD_DIM]
        ql = lax.dot_general(wuk_ref[h], qh, _NT, preferred_element_type=F32)
        ql_ref[h // hg, :, (h % hg) * qb:(h % hg + 1) * qb] = (
            ql * (HEAD_DIM ** -0.5 * LOG2_E)).astype(BF16)

    n_groups = N_HEADS // hg
    acc_ref[...] = jnp.zeros((n_groups, KV_RANK, width), F32)

    def chunk(j, carry):
        ms, ls = carry
        d = 2 * (i - j)
        start = pl.multiple_of(j * qb, qb)
        mk = mask_ref[0, pl.ds(start, qb), :]
        c_j = c_ref[0, j]
        ct_j = ct_ref[0, j]
        scores = []
        for g in range(n_groups):
            tiles = []
            for hh in range(hg):
                h = g * hg + hh
                t_diag = tz_ref[d, h]
                tiles.append(mk + jnp.concatenate(
                    [jnp.concatenate([t_diag, tz_ref[d + 1, h]], axis=1),
                     jnp.concatenate([tz_ref[jnp.maximum(d - 1, 0), h], t_diag], axis=1)], axis=0))
            s = jnp.dot(c_j, ql_ref[g], preferred_element_type=F32)
            scores.append(s + jnp.concatenate(tiles, axis=1))
        new_ms, alphas, probs = [], [], []
        for g in range(n_groups):
            m_new = jnp.maximum(ms[g], jnp.max(scores[g], axis=0, keepdims=True))
            probs.append(jnp.exp2(scores[g] - m_new).astype(BF16))
            alphas.append(jnp.exp2(ms[g] - m_new))
            new_ms.append(m_new)
        new_ls = []
        for g in range(n_groups):
            pv = jnp.dot(ct_j, probs[g], preferred_element_type=F32)
            new_ls.append(alphas[g] * ls[g] + pv[KV_RANK:KV_RANK + 1])
            acc_ref[g] = alphas[g] * acc_ref[g] + pv[:KV_RANK]
        return tuple(new_ms), tuple(new_ls)

    init = (tuple(jnp.full((1, width), M_INIT, F32) for _ in range(n_groups)),
            tuple(jnp.zeros((1, width), F32) for _ in range(n_groups)))
    _, ls = lax.fori_loop(0, i + 1, chunk, init)
    for g in range(n_groups):
        o_lat = (acc_ref[g] / ls[g]).astype(BF16)
        for hh in range(hg):
            h = g * hg + hh
            ot_ref[h] = jnp.dot(wuvt_ref[h], o_lat[:, hh * qb:(hh + 1) * qb],
                                preferred_element_type=F32)
    o_ref[0] = ot_ref[...].reshape(ATTN_W, qb).T.astype(o_ref.dtype)


def dsa_attention(p1, qcol, c, ct, mask, tz, wuk, wuvt):
    b, s, _ = p1.shape
    nq = s // DSA_QB
    const = lambda shape: pl.BlockSpec(shape, lambda bi, i: (0,) * len(shape))
    return pl.pallas_call(
        _dsa_attn_kernel,
        grid=(b, nq),
        in_specs=[pl.BlockSpec((1, DSA_QB, ATTN_W), lambda bi, i: (bi, i, qcol)),
                  pl.BlockSpec((1, nq, DSA_QB, KV_RANK), lambda bi, i: (bi, 0, 0, 0)),
                  pl.BlockSpec((1, nq, KV_RANK + ONES_ROWS, DSA_QB), lambda bi, i: (bi, 0, 0, 0)),
                  pl.BlockSpec((1, s, DSA_QB), lambda bi, i: (bi, 0, i)),
                  pl.BlockSpec(tz.shape, lambda bi, i: (0, 0, 0, 0), pipeline_mode=pl.Buffered(1)),
                  const(wuk.shape), const(wuvt.shape)],
        out_specs=pl.BlockSpec((1, DSA_QB, ATTN_W), lambda bi, i: (bi, i, 0)),
        out_shape=jax.ShapeDtypeStruct((b, s, ATTN_W), BF16),
        scratch_shapes=[pltpu.VMEM((N_HEADS // DSA_HEAD_GROUP, KV_RANK, DSA_HEAD_GROUP * DSA_QB), BF16),
                        pltpu.VMEM((N_HEADS, HEAD_DIM, DSA_QB), F32),
                        pltpu.VMEM((N_HEADS // DSA_HEAD_GROUP, KV_RANK, DSA_HEAD_GROUP * DSA_QB), F32)],
        compiler_params=_params("parallel", "arbitrary"),
        name="dsa_attention",
    )(p1, c, ct, mask, tz, wuk, wuvt)


def _rmsnorm_kernel(x_ref, g_ref, o_ref):
    o_ref[...] = _rms(x_ref[...], g_ref[...])


def rmsnorm(x, g, tm):
    t, d = x.shape
    return pl.pallas_call(
        _rmsnorm_kernel,
        grid=(t // tm,),
        in_specs=[pl.BlockSpec((tm, d), lambda i: (i, 0)), pl.BlockSpec((1, d), lambda i: (0, 0))],
        out_specs=pl.BlockSpec((tm, d), lambda i: (i, 0)),
        out_shape=jax.ShapeDtypeStruct((t, d), F32),
        compiler_params=_params("parallel"),
        name="final_rmsnorm",
    )(x, g.reshape(1, d))


def _rel_bucket(dist):
    n = np.maximum(dist, 0)
    max_exact = REL_BUCKETS // 2
    nf = np.maximum(n, 1).astype(np.float32)
    large = max_exact + (np.log(nf / np.float32(max_exact)) / np.float32(math.log(REL_MAX_DIST / max_exact))
                         * np.float32(REL_BUCKETS - max_exact)).astype(np.int32)
    large = np.minimum(large, REL_BUCKETS - 1)
    return np.where(n < max_exact, n, large)


def _bias_table_kernel(bkt_ref, rb_ref, o_ref):
    bkt = bkt_ref[0]
    for h in range(N_HEADS):
        acc = jnp.full(bkt.shape, NEG_INF, F32)
        for k in range(REL_BUCKETS):
            acc = jnp.where(bkt == k, rb_ref[k, h], acc)
        o_ref[0, h] = acc


def bias_table(bkt, rel_bias):
    n, r, c = bkt.shape
    return pl.pallas_call(
        _bias_table_kernel,
        grid=(n,),
        in_specs=[pl.BlockSpec((1, r, c), lambda i: (i, 0, 0)),
                  pl.BlockSpec(memory_space=pltpu.SMEM)],
        out_specs=pl.BlockSpec((1, N_HEADS, r, c), lambda i: (i, 0, 0, 0)),
        out_shape=jax.ShapeDtypeStruct((n, N_HEADS, r, c), F32),
        compiler_params=_params("parallel"),
        name="bias_table",
    )(bkt, rel_bias.astype(F32))


def _dilated_buckets(window, dilation):
    qi = np.arange(BLOCK)[:, None]
    ki = np.arange(2 * BLOCK)[None, :]
    rel = qi + BLOCK - ki
    bkt = _rel_bucket(rel * dilation)
    band = (rel >= 0) & (rel <= window // dilation)
    return np.stack([np.where(band, bkt, -1),
                     np.where(band & (ki >= BLOCK), bkt, -1)]).astype(np.int32)


def _dsa_buckets(s):
    nd = s // BLOCK
    key = np.arange(BLOCK)[:, None]
    qry = np.arange(BLOCK)[None, :]
    dist = np.arange(nd)[:, None, None] * BLOCK + (qry - key)[None]
    return _rel_bucket(dist).astype(np.int32)


def kernel(x, mem, rel_bias, mem_norm, final_norm, mixer_norm, ffn_norm, w_mem_kv, w_out,
           even_w_in, even_w_gate, even_w_up, even_w_down,
           odd_w_in, odd_kv_norm, odd_w_uk, odd_w_uv,
           odd_w_router, odd_w_gate, odd_w_up, odd_w_down):
    b, s, d = x.shape
    t = b * s
    depth = mixer_norm.shape[0]
    xt = x.reshape(t, d)
    mem2 = mem.reshape(b * N_MEM, d)
    dil_tables = [bias_table(_dilated_buckets(w, dl), rel_bias) for w, dl in DIL_PAIRS]
    tz = bias_table(_dsa_buckets(s), rel_bias * LOG2_E)

    for i in range(depth):
        j = i // 2
        kvm = norm_matmul(mem2, mem_norm, w_mem_kv[i].astype(BF16), BF16, 512, 512)
        kvm = kvm.reshape(b, N_MEM, 2 * CROSS_WIDTH)
        wo = w_out[i].astype(BF16)
        if i % 2 == 0:
            w_in = even_w_in[j].astype(BF16)
            proj = even_in_projection(xt, mixer_norm[i], w_in[:, :5 * ATTN_W], b, s, 512)
            qc = norm_matmul(xt, mixer_norm[i], w_in[:, 5 * ATTN_W:], BF16, 512, CROSS_WIDTH)
            outs, lses = [], []
            for grp in range(len(DIL_PAIRS)):
                o, lse = dilated_branch(proj[grp], proj[3 + grp], proj[6 + grp], dil_tables[grp])
                outs.append(o)
                lses.append(lse)
            mix = combine_groups(outs, lses, 256).reshape(t, ATTN_W)
            cross = cross_attention(qc.reshape(b, s, CROSS_WIDTH), 0, kvm, 512)
        else:
            o_q, o_c, o_qi, o_ki, o_wi, o_qc = 0, 1024, 1280, 1792, 1856, 1864
            w = odd_w_in[j]
            pad = jnp.zeros((d, 128 - IDX_DIM - N_IDX_HEADS), w.dtype)
            w_in = jnp.concatenate(
                [w[:, o_q:o_c], w[:, o_qi:o_ki], w[:, o_c:o_qi], w[:, o_qc:],
                 w[:, o_ki:o_wi], w[:, o_wi:o_qc], pad], axis=1).astype(BF16)
            p1 = norm_matmul(xt, mixer_norm[i], w_in, BF16, 512, w_in.shape[1])
            p1 = p1.reshape(b, s, w_in.shape[1])
            c, ct = kv_latent_norm(p1, 6, odd_kv_norm[j])
            mask = dsa_select(p1, 2, 16)
            wuk = jnp.transpose(odd_w_uk[j], (1, 0, 2)).astype(BF16)
            wuvt = jnp.transpose(odd_w_uv[j], (1, 2, 0)).astype(BF16)
            mix = dsa_attention(p1, 0, c, ct, mask, tz, wuk, wuvt).reshape(t, ATTN_W)
            cross = cross_attention(p1, 7, kvm, 512)
        xt = out_projection(mix, cross.reshape(t, CROSS_WIDTH), wo[:ATTN_W], wo[ATTN_W:], xt, 512)
        if i % 2 == 0:
            xt = swiglu_ffn(xt, ffn_norm[i], even_w_gate[j].astype(BF16), even_w_up[j].astype(BF16),
                            even_w_down[j].astype(BF16), 512, even_w_gate.shape[2])
        else:
            wr = jnp.pad(odd_w_router[j], ((0, 0), (0, 128 - N_EXPERTS)))
            xt = moe_ffn(xt, ffn_norm[i], wr, odd_w_gate[j].astype(BF16), odd_w_up[j].astype(BF16),
                         odd_w_down[j].astype(BF16), final_norm if i == depth - 1 else None)
    if depth % 2 == 1:
        xt = rmsnorm(xt, final_norm, 512)
    return xt.reshape(b, s, d)
```

```python
import functools
import math

import jax
import jax.numpy as jnp
import numpy as np
from jax import lax
from jax.experimental import pallas as pl
from jax.experimental.pallas import tpu as pltpu

D_MODEL = 1024
N_HEADS = 16
HEAD_DIM = 64
ATTN_W = N_HEADS * HEAD_DIM
DIL_PAIRS = ((128, 1), (512, 4), (2048, 16))
BLOCK = 128
LANES = 128
KV_RANK = 256
N_IDX_HEADS = 8
IDX_DIM = 64
TOPK = 256
N_MEM = 256
N_CROSS_HEADS = 4
CROSS_WIDTH = 256
REL_BUCKETS = 32
REL_MAX_DIST = 2048
N_EXPERTS = 8
RMS_EPS = 1e-6
NEG_INF = -1e30
M_INIT = -5e29
LOG2_E = math.log2(math.e)

F32 = jnp.float32
BF16 = jnp.bfloat16
V7X_VMEM_LIMIT = 56 * 1024 * 1024
DSA_QB = 256
DSA_HEAD_GROUP = 8
ONES_ROWS = 16
INT_MIN = -(2 ** 31)
KEY_OF_NEG_INF = (0xFF800000 ^ 0x7FFFFFFF) - (1 << 32)
MOE_TILE = 512
ROUTE_TM = 256
MOE_FF_CHUNK = 1792

_NT = (((1,), (1,)), ((), ()))


def _params(*sem):
    return pltpu.CompilerParams(dimension_semantics=sem, vmem_limit_bytes=V7X_VMEM_LIMIT)


def _rms(x, g):
    return x * lax.rsqrt(jnp.mean(x * x, axis=-1, keepdims=True) + RMS_EPS) * g


def _norm_mm_kernel(x_ref, g_ref, w_ref, o_ref, xn_ref):
    @pl.when(pl.program_id(1) == 0)
    def _():
        xn_ref[...] = _rms(x_ref[...].astype(F32), g_ref[...]).astype(BF16)

    o_ref[...] = jnp.dot(xn_ref[...], w_ref[...], preferred_element_type=F32).astype(o_ref.dtype)


def norm_matmul(x, g, w, out_dtype, tm, tn):
    t, k = x.shape
    n = w.shape[1]
    return pl.pallas_call(
        _norm_mm_kernel,
        grid=(t // tm, n // tn),
        in_specs=[pl.BlockSpec((tm, k), lambda i, j: (i, 0)),
                  pl.BlockSpec((1, k), lambda i, j: (0, 0)),
                  pl.BlockSpec((k, tn), lambda i, j: (0, j))],
        out_specs=pl.BlockSpec((tm, tn), lambda i, j: (i, j)),
        out_shape=jax.ShapeDtypeStruct((t, n), out_dtype),
        scratch_shapes=[pltpu.VMEM((tm, k), BF16)],
        compiler_params=_params("parallel", "arbitrary"),
        name="norm_matmul",
    )(x, g.reshape(1, k), w)


def _even_proj_kernel(x_ref, g_ref, w_ref, q1_ref, q4_ref, q16_ref, k1_ref, k4_ref, k16_ref,
                      v1_ref, v4_ref, v16_ref, xn_ref, y_ref):
    j = pl.program_id(1)
    tm = x_ref.shape[0]

    @pl.when(j == 0)
    def _():
        xn_ref[...] = _rms(x_ref[...], g_ref[...]).astype(BF16)

    y = jnp.dot(xn_ref[...], w_ref[...], preferred_element_type=F32)
    n_lane_blocks = y.shape[1] // LANES
    for c in range(n_lane_blocks):
        y_ref[c] = y[:, c * LANES:(c + 1) * LANES]

    def put(dst_ref, d):
        if d == 1:
            dst_ref[0, 0] = y.astype(BF16)
            return
        for r in range(d):
            for c in range(n_lane_blocks):
                rows = y_ref[c, pl.ds(r, tm // d, stride=d), :]
                dst_ref[0, r, :, c * LANES:(c + 1) * LANES] = rows.astype(BF16)

    column_dsts = (((q1_ref, 1),), ((q4_ref, 4),), ((q16_ref, 16),),
                   ((k1_ref, 1), (k4_ref, 4), (k16_ref, 16)),
                   ((v1_ref, 1), (v4_ref, 4), (v16_ref, 16)))
    for col, dsts in enumerate(column_dsts):
        @pl.when(j == col)
        def _(dsts=dsts):
            for ref, d in dsts:
                put(ref, d)


def even_in_projection(x, g, w, b, s, tm):
    t, kdim = x.shape
    per_b = s // tm
    dils = [dl for _, dl in DIL_PAIRS]
    layouts = dils + dils + dils
    spec = lambda dl: pl.BlockSpec((1, dl, tm // dl, ATTN_W), lambda i, j: (i // per_b, 0, i % per_b, 0))
    return pl.pallas_call(
        _even_proj_kernel,
        grid=(t // tm, 5),
        in_specs=[pl.BlockSpec((tm, kdim), lambda i, j: (i, 0)),
                  pl.BlockSpec((1, kdim), lambda i, j: (0, 0)),
                  pl.BlockSpec((kdim, ATTN_W), lambda i, j: (0, j))],
        out_specs=[spec(dl) for dl in layouts],
        out_shape=[jax.ShapeDtypeStruct((b, dl, s // dl, ATTN_W), BF16) for dl in layouts],
        scratch_shapes=[pltpu.VMEM((tm, kdim), BF16), pltpu.VMEM((ATTN_W // LANES, tm, LANES), F32)],
        compiler_params=_params("parallel", "arbitrary"),
        name="even_in_projection",
    )(x, g.reshape(1, kdim), w)


def _dil_kernel(q_ref, kp_ref, kc_ref, vp_ref, vc_ref, tab_ref, o_ref, lse_ref, s_ref, p_ref, m_ref):
    first = (pl.program_id(2) == 0).astype(jnp.int32)
    pair = 2 * HEAD_DIM
    lo_q = lax.broadcasted_iota(jnp.int32, (BLOCK, pair), 1) < HEAD_DIM
    lo_k = lax.broadcasted_iota(jnp.int32, (2 * BLOCK, pair), 1) < HEAD_DIM
    ones_bd = jnp.concatenate([jnp.where(lo_k, 1.0, 0.0), jnp.where(lo_k, 0.0, 1.0)], axis=0).astype(BF16)
    scale = jnp.asarray(HEAD_DIM ** -0.5, BF16)
    zero = jnp.zeros((), BF16)
    n_pairs = N_HEADS // 2
    for p in range(n_pairs):
        cols = slice(p * pair, (p + 1) * pair)
        q = q_ref[0, 0, :, cols] * scale
        k = jnp.concatenate([kp_ref[0, 0, :, cols], kc_ref[0, 0, :, cols]], axis=0)
        q_ab = jnp.concatenate([jnp.where(lo_q, q, zero), jnp.where(lo_q, zero, q)], axis=0)
        s_ab = lax.dot_general(q_ab, k, _NT, preferred_element_type=F32)
        s_ref[2 * p] = s_ab[:BLOCK] + tab_ref[first, 2 * p]
        s_ref[2 * p + 1] = s_ab[BLOCK:] + tab_ref[first, 2 * p + 1]
    for p in range(n_pairs):
        sa = s_ref[2 * p]
        sb = s_ref[2 * p + 1]
        ma = jnp.max(sa, axis=-1, keepdims=True)
        mb = jnp.max(sb, axis=-1, keepdims=True)
        p_ref[p, :, :2 * BLOCK] = jnp.exp(sa - ma).astype(BF16)
        p_ref[p, :, 2 * BLOCK:] = jnp.exp(sb - mb).astype(BF16)
        m_ref[p] = jnp.where(lo_q, ma, mb)
    for p in range(n_pairs):
        cols = slice(p * pair, (p + 1) * pair)
        v = jnp.concatenate([vp_ref[0, 0, :, cols], vc_ref[0, 0, :, cols]], axis=0)
        v_bd = jnp.concatenate([jnp.where(lo_k, v, zero), jnp.where(lo_k, zero, v)], axis=0)
        ol = jnp.dot(p_ref[p], jnp.concatenate([v_bd, ones_bd], axis=1), preferred_element_type=F32)
        o, l = ol[:, :pair], ol[:, pair:]
        o_ref[0, 0, :, cols] = (o / l).astype(o_ref.dtype)
        lse_ref[0, 0, :, cols] = m_ref[p] + jnp.log(l)


def dilated_branch(q, k, v, table):
    b, dilation, ln, _ = q.shape
    nb = ln // BLOCK
    blk = (1, 1, BLOCK, ATTN_W)
    cur = pl.BlockSpec(blk, lambda bi, r, n: (bi, r, n, 0))
    prev = pl.BlockSpec(blk, lambda bi, r, n: (bi, r, jnp.maximum(n - 1, 0), 0))
    return pl.pallas_call(
        _dil_kernel,
        grid=(b, dilation, nb),
        in_specs=[cur, prev, cur, prev, cur,
                  pl.BlockSpec(table.shape, lambda bi, r, n: (0, 0, 0, 0))],
        out_specs=[cur, cur],
        out_shape=[jax.ShapeDtypeStruct(q.shape, BF16), jax.ShapeDtypeStruct(q.shape, F32)],
        scratch_shapes=[pltpu.VMEM((N_HEADS, BLOCK, 2 * BLOCK), F32),
                        pltpu.VMEM((N_HEADS // 2, BLOCK, 4 * BLOCK), BF16),
                        pltpu.VMEM((N_HEADS // 2, BLOCK, 2 * HEAD_DIM), F32)],
        compiler_params=_params("parallel", "parallel", "arbitrary"),
        name=f"dilated_attn_d{dilation}",
    )(q, k, k, v, v, table)


def _combine_kernel(o1, o4, o16, l1, l4, l16, out_ref, so4, sl4, so16, sl16):
    tm = out_ref.shape[1]
    n_lane_blocks = out_ref.shape[2] // LANES

    def to_token_order(src_ref, dst_ref, d):
        for r in range(d):
            for c in range(n_lane_blocks):
                rows = src_ref[0, r, :, c * LANES:(c + 1) * LANES]
                dst_ref[c, pl.ds(r, tm // d, stride=d), :] = rows.astype(F32)

    to_token_order(o4, so4, 4)
    to_token_order(l4, sl4, 4)
    to_token_order(o16, so16, 16)
    to_token_order(l16, sl16, 16)
    for c in range(n_lane_blocks):
        cols = slice(c * LANES, (c + 1) * LANES)
        a1, a2, a3 = l1[0, 0, :, cols], sl4[c], sl16[c]
        m = jnp.maximum(jnp.maximum(a1, a2), a3)
        w1, w2, w3 = jnp.exp(a1 - m), jnp.exp(a2 - m), jnp.exp(a3 - m)
        num = w1 * o1[0, 0, :, cols] + w2 * so4[c] + w3 * so16[c]
        out_ref[0, :, cols] = (num / (w1 + w2 + w3)).astype(out_ref.dtype)


def combine_groups(outs, lses, tm):
    b, _, s, w = outs[0].shape
    spec = lambda dl: pl.BlockSpec((1, dl, tm // dl, w), lambda bi, i: (bi, 0, i, 0))
    specs = [spec(o.shape[1]) for o in outs]
    return pl.pallas_call(
        _combine_kernel,
        grid=(b, s // tm),
        in_specs=specs + specs,
        out_specs=pl.BlockSpec((1, tm, w), lambda bi, i: (bi, i, 0)),
        out_shape=jax.ShapeDtypeStruct((b, s, w), BF16),
        scratch_shapes=[pltpu.VMEM((w // LANES, tm, LANES), F32)] * 4,
        compiler_params=_params("parallel", "parallel"),
        name="combine_groups",
    )(*outs, *lses)


def _cross_kernel(q_ref, kv_ref, o_ref):
    for h in range(N_CROSS_HEADS):
        sl = slice(h * HEAD_DIM, (h + 1) * HEAD_DIM)
        vsl = slice(CROSS_WIDTH + h * HEAD_DIM, CROSS_WIDTH + (h + 1) * HEAD_DIM)
        s = lax.dot_general(q_ref[0, :, sl], kv_ref[0, :, sl], _NT,
                            preferred_element_type=F32) * (HEAD_DIM ** -0.5)
        m = jnp.max(s, axis=-1, keepdims=True)
        p = jnp.exp(s - m)
        l = jnp.sum(p, axis=-1, keepdims=True)
        o = jnp.dot(p.astype(BF16), kv_ref[0, :, vsl], preferred_element_type=F32)
        o_ref[0, :, sl] = (o / l).astype(o_ref.dtype)


def cross_attention(qsrc, qcol, kv, tm):
    b, s, _ = qsrc.shape
    return pl.pallas_call(
        _cross_kernel,
        grid=(b, s // tm),
        in_specs=[pl.BlockSpec((1, tm, CROSS_WIDTH), lambda bi, i: (bi, i, qcol)),
                  pl.BlockSpec((1, N_MEM, 2 * CROSS_WIDTH), lambda bi, i: (bi, 0, 0))],
        out_specs=pl.BlockSpec((1, tm, CROSS_WIDTH), lambda bi, i: (bi, i, 0)),
        out_shape=jax.ShapeDtypeStruct((b, s, CROSS_WIDTH), BF16),
        compiler_params=_params("parallel", "parallel"),
        name="cross_attn",
    )(qsrc, kv)


def _outproj_kernel(mix_ref, cr_ref, wa_ref, wb_ref, x_ref, o_ref):
    acc = jnp.dot(mix_ref[...], wa_ref[...], preferred_element_type=F32)
    acc = acc + jnp.dot(cr_ref[...], wb_ref[...], preferred_element_type=F32)
    o_ref[...] = x_ref[...] + acc


def out_projection(mix, cross, wa, wb, x, tm):
    t, d = x.shape
    return pl.pallas_call(
        _outproj_kernel,
        grid=(t // tm,),
        in_specs=[pl.BlockSpec((tm, ATTN_W), lambda i: (i, 0)),
                  pl.BlockSpec((tm, CROSS_WIDTH), lambda i: (i, 0)),
                  pl.BlockSpec(wa.shape, lambda i: (0, 0)),
                  pl.BlockSpec(wb.shape, lambda i: (0, 0)),
                  pl.BlockSpec((tm, d), lambda i: (i, 0))],
        out_specs=pl.BlockSpec((tm, d), lambda i: (i, 0)),
        out_shape=jax.ShapeDtypeStruct((t, d), F32),
        compiler_params=_params("parallel"),
        name="out_projection",
    )(mix, cross, wa, wb, x)


def _outproj_swiglu_kernel(mix_ref, cr_ref, wa_ref, wb_ref, x_ref, g_ref, wg_ref, wu_ref, wd_ref, o_ref):
    x1 = x_ref[...] + jnp.dot(mix_ref[...], wa_ref[...], preferred_element_type=F32)
    x1 = x1 + jnp.dot(cr_ref[...], wb_ref[...], preferred_element_type=F32)
    hn = _rms(x1, g_ref[...]).astype(BF16)
    a = jnp.dot(hn, wg_ref[...], preferred_element_type=F32)
    u = jnp.dot(hn, wu_ref[...], preferred_element_type=F32)
    act = (a * jax.nn.sigmoid(a) * u).astype(BF16)
    o_ref[...] = x1 + jnp.dot(act, wd_ref[...], preferred_element_type=F32)


def out_projection_swiglu(mix, cross, wa, wb, x, g, wg, wu, wd, tm):
    t, d = x.shape
    resident = lambda w: pl.BlockSpec(w.shape, lambda i: (0, 0), pipeline_mode=pl.Buffered(1))
    return pl.pallas_call(
        _outproj_swiglu_kernel,
        grid=(t // tm,),
        in_specs=[pl.BlockSpec((tm, ATTN_W), lambda i: (i, 0)),
                  pl.BlockSpec((tm, CROSS_WIDTH), lambda i: (i, 0)),
                  resident(wa), resident(wb),
                  pl.BlockSpec((tm, d), lambda i: (i, 0)),
                  pl.BlockSpec((1, d), lambda i: (0, 0)),
                  resident(wg), resident(wu), resident(wd)],
        out_specs=pl.BlockSpec((tm, d), lambda i: (i, 0)),
        out_shape=jax.ShapeDtypeStruct((t, d), F32),
        compiler_params=_params("parallel"),
        name="out_projection_swiglu",
    )(mix, cross, wa, wb, x, g.reshape(1, d), wg, wu, wd)


def _router_kernel(x_ref, g_ref, wr_ref, gates_ref, route_ref, counts_ref, run_ref):
    @pl.when(pl.program_id(0) == 0)
    def _():
        run_ref[...] = jnp.zeros(run_ref.shape, F32)

    hn = _rms(x_ref[...], g_ref[...])
    logits = jnp.dot(hn, wr_ref[...], preferred_element_type=F32,
                     precision=lax.Precision.HIGHEST)
    tm = logits.shape[0]
    lane = lax.broadcasted_iota(jnp.int32, logits.shape, 1)
    lg = jnp.where(lane < N_EXPERTS, logits, -jnp.inf)
    m1 = jnp.max(lg, axis=-1, keepdims=True)
    i1 = jnp.min(jnp.where(lg == m1, lane, 128), axis=-1, keepdims=True)
    lg2 = jnp.where(lane == i1, -jnp.inf, lg)
    m2 = jnp.max(lg2, axis=-1, keepdims=True)
    i2 = jnp.min(jnp.where(lg2 == m2, lane, 128), axis=-1, keepdims=True)
    e = jnp.exp(m2 - m1)
    gates_ref[...] = jnp.where(lane == 0, 1.0 / (1.0 + e), jnp.where(lane == 1, e / (1.0 + e), 0.0))

    assign = jnp.where((lane == i1) | (lane == i2), 1.0, 0.0)
    r = lax.broadcasted_iota(jnp.int32, (tm, tm), 0)
    c = lax.broadcasted_iota(jnp.int32, (tm, tm), 1)
    lower = jnp.where(r > c, 1.0, 0.0).astype(BF16)
    before = jnp.dot(lower, assign.astype(BF16), preferred_element_type=F32) + run_ref[...]
    rank1 = jnp.sum(jnp.where(lane == i1, before, 0.0), axis=-1, keepdims=True).astype(jnp.int32)
    rank2 = jnp.sum(jnp.where(lane == i2, before, 0.0), axis=-1, keepdims=True).astype(jnp.int32)
    route_ref[...] = jnp.where(lane == 0, i1, jnp.where(lane == 1, i2, jnp.where(
        lane == 2, rank1, jnp.where(lane == 3, rank2, 0))))
    run = run_ref[...] + jnp.sum(assign, axis=0, keepdims=True)
    run_ref[...] = run
    counts_ref[...] = run.astype(jnp.int32)


def router(x, g, wr_pad, tm):
    t, d = x.shape
    return pl.pallas_call(
        _router_kernel,
        grid=(t // tm,),
        in_specs=[pl.BlockSpec((tm, d), lambda i: (i, 0)),
                  pl.BlockSpec((1, d), lambda i: (0, 0)),
                  pl.BlockSpec((d, 128), lambda i: (0, 0))],
        out_specs=[pl.BlockSpec((tm, 128), lambda i: (i, 0)),
                   pl.BlockSpec((tm, 128), lambda i: (i, 0)),
                   pl.BlockSpec((1, 128), lambda i: (0, 0))],
        out_shape=[jax.ShapeDtypeStruct((t, 128), F32),
                   jax.ShapeDtypeStruct((t, 128), jnp.int32),
                   jax.ShapeDtypeStruct((1, 128), jnp.int32)],
        scratch_shapes=[pltpu.VMEM((1, 128), F32)],
        compiler_params=_params("arbitrary"),
        name="router",
    )(x, g.reshape(1, d), wr_pad)


def _row_copy(src, src_row, dst, dst_row, sem):
    return pltpu.make_async_copy(src.at[pl.ds(src_row, 1)], dst.at[pl.ds(dst_row, 1)], sem)


def _scatter_rows_kernel(d1_ref, d2_ref, x_ref, xs_in_ref, xs_ref, sem):
    del xs_in_ref
    tm = x_ref.shape[0]
    base = pl.program_id(0) * tm

    def issue(r, carry):
        _row_copy(x_ref, r, xs_ref, d1_ref[base + r], sem).start()
        _row_copy(x_ref, r, xs_ref, d2_ref[base + r], sem).start()
        return carry

    lax.fori_loop(0, tm, issue, 0, unroll=8)

    for _ in range(2):
        pltpu.make_async_copy(x_ref, xs_ref.at[pl.ds(0, tm)], sem).wait()


def scatter_rows(x, dest1, dest2, n_rows, tm):
    t, d = x.shape
    zeros = jnp.zeros((n_rows, d), x.dtype)
    return pl.pallas_call(
        _scatter_rows_kernel,
        grid_spec=pltpu.PrefetchScalarGridSpec(
            num_scalar_prefetch=2,
            grid=(t // tm,),
            in_specs=[pl.BlockSpec((tm, d), lambda i, d1, d2: (i, 0)),
                      pl.BlockSpec(memory_space=pl.ANY)],
            out_specs=pl.BlockSpec(memory_space=pl.ANY),
            scratch_shapes=[pltpu.SemaphoreType.DMA(())]),
        out_shape=jax.ShapeDtypeStruct((n_rows, d), x.dtype),
        input_output_aliases={3: 0},
        compiler_params=_params("arbitrary"),
        name="moe_scatter_rows",
    )(dest1, dest2, x, zeros)


def _moe_group_kernel(te_ref, nu_ref, x_ref, g_ref, wg_ref, wu_ref, wd_ref, o_ref, hn_ref):
    i = pl.program_id(0)
    j = pl.program_id(1)
    used = i < nu_ref[0]

    @pl.when(j == 0)
    def _():
        o_ref[...] = jnp.zeros(o_ref.shape, F32)

    @pl.when(used & (j == 0))
    def _():
        hn_ref[...] = _rms(x_ref[...], g_ref[...]).astype(BF16)

    @pl.when(used)
    def _():
        hn = hn_ref[...]
        a = jnp.dot(hn, wg_ref[...], preferred_element_type=F32)
        u = jnp.dot(hn, wu_ref[...], preferred_element_type=F32)
        act = (a * jax.nn.sigmoid(a) * u).astype(BF16)
        o_ref[...] += jnp.dot(act, wd_ref[...], preferred_element_type=F32)


def moe_group_ffn(xs, g, tile_expert, n_used, wg, wu, wd, tm, tf):
    p, d = xs.shape
    f = wg.shape[2]
    nj = f // tf
    chunk = lambda i, j, te, nu: jnp.where(i < nu[0], j, nj - 1)
    return pl.pallas_call(
        _moe_group_kernel,
        grid_spec=pltpu.PrefetchScalarGridSpec(
            num_scalar_prefetch=2,
            grid=(p // tm, nj),
            in_specs=[pl.BlockSpec((tm, d), lambda i, j, te, nu: (i, 0)),
                      pl.BlockSpec((1, d), lambda i, j, te, nu: (0, 0)),
                      pl.BlockSpec((None, d, tf), lambda i, j, te, nu: (te[i], 0, chunk(i, j, te, nu))),
                      pl.BlockSpec((None, d, tf), lambda i, j, te, nu: (te[i], 0, chunk(i, j, te, nu))),
                      pl.BlockSpec((None, tf, d), lambda i, j, te, nu: (te[i], chunk(i, j, te, nu), 0))],
            out_specs=pl.BlockSpec((tm, d), lambda i, j, te, nu: (i, 0)),
            scratch_shapes=[pltpu.VMEM((tm, d), BF16)]),
        out_shape=jax.ShapeDtypeStruct((p, d), F32),
        compiler_params=_params("arbitrary", "arbitrary"),
        name="moe_group_ffn",
    )(tile_expert, n_used, xs, g.reshape(1, d), wg, wu, wd)


def _moe_combine_kernel(d1_ref, d2_ref, x_ref, gates_ref, fg_ref, ys_ref, o_ref, y1_ref, y2_ref, sem,
                        *, final_norm):
    tm = x_ref.shape[0]
    i = pl.program_id(0)

    def issue(tile, slot):
        base = tile * tm

        def body(r, carry):
            _row_copy(ys_ref, d1_ref[base + r], y1_ref.at[slot], r, sem.at[slot]).start()
            _row_copy(ys_ref, d2_ref[base + r], y2_ref.at[slot], r, sem.at[slot]).start()
            return carry

        lax.fori_loop(0, tm, body, 0, unroll=8)

    @pl.when(i == 0)
    def _():
        issue(0, 0)

    @pl.when(i + 1 < pl.num_programs(0))
    def _():
        issue(i + 1, (i + 1) % 2)

    slot = i % 2
    pltpu.make_async_copy(ys_ref.at[pl.ds(0, tm)], y1_ref.at[slot], sem.at[slot]).wait()
    pltpu.make_async_copy(ys_ref.at[pl.ds(0, tm)], y2_ref.at[slot], sem.at[slot]).wait()
    gates = gates_ref[...]
    out = x_ref[...] + gates[:, 0:1] * y1_ref[slot] + gates[:, 1:2] * y2_ref[slot]
    if final_norm:
        out = _rms(out, fg_ref[...])
    o_ref[...] = out


def moe_combine(x, gates, ys, dest1, dest2, tm, final_gain=None):
    t, d = x.shape
    fg = jnp.ones((1, d), F32) if final_gain is None else final_gain.reshape(1, d)
    return pl.pallas_call(
        functools.partial(_moe_combine_kernel, final_norm=final_gain is not None),
        grid_spec=pltpu.PrefetchScalarGridSpec(
            num_scalar_prefetch=2,
            grid=(t // tm,),
            in_specs=[pl.BlockSpec((tm, d), lambda i, d1, d2: (i, 0)),
                      pl.BlockSpec((tm, 128), lambda i, d1, d2: (i, 0)),
                      pl.BlockSpec((1, d), lambda i, d1, d2: (0, 0)),
                      pl.BlockSpec(memory_space=pl.ANY)],
            out_specs=pl.BlockSpec((tm, d), lambda i, d1, d2: (i, 0)),
            scratch_shapes=[pltpu.VMEM((2, tm, d), F32), pltpu.VMEM((2, tm, d), F32),
                            pltpu.SemaphoreType.DMA((2,))]),
        out_shape=jax.ShapeDtypeStruct((t, d), F32),
        compiler_params=_params("arbitrary"),
        name="moe_combine",
    )(dest1, dest2, x, gates, fg, ys)


def moe_ffn(x, g, wr_pad, wg, wu, wd, final_gain=None):
    t, d = x.shape
    ne = wg.shape[0]
    gates, route, counts = router(x, g, wr_pad, ROUTE_TM)
    counts = counts[0, :ne]
    padded = (counts + MOE_TILE - 1) // MOE_TILE * MOE_TILE
    ends = jnp.cumsum(padded)
    starts = ends - padded
    expert_ids = jnp.arange(ne, dtype=jnp.int32)[None, :]
    start_of = lambda e: jnp.sum(jnp.where(e[:, None] == expert_ids, starts[None, :], 0), axis=1)
    dest1 = (start_of(route[:, 0]) + route[:, 2]).astype(jnp.int32)
    dest2 = (start_of(route[:, 1]) + route[:, 3]).astype(jnp.int32)
    n_tiles = (2 * t) // MOE_TILE + ne
    tile_start = jnp.arange(n_tiles, dtype=jnp.int32) * MOE_TILE
    tile_expert = jnp.minimum(jnp.sum(tile_start[:, None] >= ends[None, :], axis=1), ne - 1).astype(jnp.int32)
    n_used = (ends[-1:] // MOE_TILE).astype(jnp.int32)
    xs = scatter_rows(x, dest1, dest2, n_tiles * MOE_TILE, ROUTE_TM)
    ys = moe_group_ffn(xs, g, tile_expert, n_used, wg, wu, wd, MOE_TILE, MOE_FF_CHUNK)
    return moe_combine(x, gates, ys, dest1, dest2, ROUTE_TM, final_gain)


def _kvnorm_kernel(c_ref, g_ref, o_ref, ot_ref):
    y = _rms(c_ref[0].astype(F32), g_ref[...])
    o_ref[0, 0] = y.astype(BF16)
    ot_ref[0, 0, :KV_RANK] = y.T.astype(BF16)
    ot_ref[0, 0, KV_RANK:] = jnp.ones((ONES_ROWS, DSA_QB), BF16)


def kv_latent_norm(p1, ccol, g):
    b, s, _ = p1.shape
    nc = s // DSA_QB
    out = jax.ShapeDtypeStruct((b, nc, DSA_QB, KV_RANK), BF16)
    out_t = jax.ShapeDtypeStruct((b, nc, KV_RANK + ONES_ROWS, DSA_QB), BF16)
    blk = pl.BlockSpec((1, 1, DSA_QB, KV_RANK), lambda bi, j: (bi, j, 0, 0))
    blk_t = pl.BlockSpec((1, 1, KV_RANK + ONES_ROWS, DSA_QB), lambda bi, j: (bi, j, 0, 0))
    return pl.pallas_call(
        _kvnorm_kernel,
        grid=(b, nc),
        in_specs=[pl.BlockSpec((1, DSA_QB, KV_RANK), lambda bi, j: (bi, j, ccol)),
                  pl.BlockSpec((1, KV_RANK), lambda bi, j: (0, 0))],
        out_specs=[blk, blk_t],
        out_shape=[out, out_t],
        compiler_params=_params("parallel", "parallel"),
        name="kv_latent_norm",
    )(p1, g.reshape(1, KV_RANK))


def _select_kernel(qi_ref, kall_ref, kq_ref, mask_ref, sc_ref, jcut_ref, *, nchunks_total):
    qb = DSA_QB
    i = pl.program_id(1)
    nch = i + 1
    shape = (qb, qb)
    row = lax.broadcasted_iota(jnp.int32, shape, 0)
    qpos = i * qb + lax.broadcasted_iota(jnp.int32, shape, 1)

    wt = kq_ref[0].astype(F32).T
    wscale = (N_IDX_HEADS ** -0.5) * (IDX_DIM ** -0.5)

    def score_chunk(j, carry):
        start = pl.multiple_of(j * qb, qb)
        kc = kall_ref[0, pl.ds(start, qb), :][:, :IDX_DIM]
        acc = jnp.zeros(shape, F32)
        for h in range(N_IDX_HEADS):
            qh = qi_ref[0, :, h * IDX_DIM:(h + 1) * IDX_DIM]
            s = lax.dot_general(kc, qh, _NT, preferred_element_type=F32)
            acc = acc + jnp.maximum(s, 0.0) * (wt[IDX_DIM + h:IDX_DIM + h + 1, :] * wscale)
        sc_ref[j] = jnp.where(j * qb + row <= qpos, acc, NEG_INF)
        return carry

    lax.fori_loop(0, nch, score_chunk, 0)

    def count(pred):
        def body(j, c):
            hit = jnp.where(pred(sc_ref[j], j), 1, 0)
            return c + jnp.sum(hit.reshape(qb // 8, 8, qb), axis=0)
        c = lax.fori_loop(0, nch, body, jnp.zeros((8, qb), jnp.int32))
        return jnp.sum(c, axis=0, keepdims=True)

    def key_to_float(x):
        return pltpu.bitcast(x ^ ((x >> 31) & 0x7FFFFFFF), F32)

    def rank_reached(x):
        cand = key_to_float(x)
        return (count(lambda s, j: s >= cand) >= TOPK) | (x < KEY_OF_NEG_INF)

    thr_key = jnp.where(rank_reached(jnp.zeros((1, qb), jnp.int32)), 0, INT_MIN)

    def bit_body(t, thr_key):
        cand = thr_key + jnp.left_shift(jnp.int32(1), 30 - t)
        return jnp.where(rank_reached(cand), cand, thr_key)

    thr = key_to_float(lax.fori_loop(0, 31, bit_body, thr_key))

    n_gt = count(lambda s, j: s > thr)
    n_ge = count(lambda s, j: s >= thr)
    need = TOPK - n_gt

    jcut_ref[...] = jnp.full((1, qb), 1 << 30, jnp.int32)

    @pl.when(jnp.max(n_ge) > TOPK)
    def _():
        def idx_body(t, x):
            cand = x + jnp.left_shift(jnp.int32(1), 10 - t)
            below = count(lambda s, j: (s == thr) & (j * qb + row <= cand - 1))
            return jnp.where(below < need, cand, x)
        jcut_ref[...] = lax.fori_loop(0, 11, idx_body, jnp.zeros((1, qb), jnp.int32))

    jcut = jcut_ref[...]

    def write_chunk(j, carry):
        s = sc_ref[j]
        spos = j * qb + row
        sel = (s > thr) | ((s == thr) & (spos <= jcut))
        sel = sel & (spos <= qpos)
        start = pl.multiple_of(j * qb, qb)
        mask_ref[0, pl.ds(start, qb), :] = jnp.where(sel, 0.0, NEG_INF)
        return carry

    lax.fori_loop(0, nch, write_chunk, 0)

    def fill_chunk(j, carry):
        start = pl.multiple_of(j * qb, qb)
        mask_ref[0, pl.ds(start, qb), :] = jnp.full(shape, NEG_INF, F32)
        return carry

    lax.fori_loop(nch, nchunks_total, fill_chunk, 0)


def dsa_select(p1, qicol, kwcol):
    b, s, _ = p1.shape
    nq = s // DSA_QB
    return pl.pallas_call(
        functools.partial(_select_kernel, nchunks_total=nq),
        grid=(b, nq),
        in_specs=[pl.BlockSpec((1, DSA_QB, N_IDX_HEADS * IDX_DIM), lambda bi, i: (bi, i, qicol)),
                  pl.BlockSpec((1, s, 128), lambda bi, i: (bi, 0, kwcol)),
                  pl.BlockSpec((1, DSA_QB, 128), lambda bi, i: (bi, i, kwcol))],
        out_specs=pl.BlockSpec((1, s, DSA_QB), lambda bi, i: (bi, 0, i)),
        out_shape=jax.ShapeDtypeStruct((b, s, s), F32),
        scratch_shapes=[pltpu.VMEM((nq, DSA_QB, DSA_QB), F32),
                        pltpu.VMEM((1, DSA_QB), jnp.int32)],
        compiler_params=_params("parallel", "arbitrary"),
        name="dsa_select",
    )(p1, p1, p1)


def _dsa_attn_kernel(q_ref, c_ref, ct_ref, mask_ref, tz_ref, wuk_ref, wuvt_ref, o_ref,
                     ql_ref, ot_ref, acc_ref):
    qb = DSA_QB
    hg = DSA_HEAD_GROUP
    width = hg * qb
    i = pl.program_id(1)

    for h in range(N_HEADS):
        qh = q_ref[0, :, h * HEAD_DIM:(h + 1) * HEAD_DIM]
        ql = lax.dot_general(wuk_ref[h], qh, _NT, preferred_element_type=F32)
        ql_ref[h // hg, :, (h % hg) * qb:(h % hg + 1) * qb] = (
            ql * (HEAD_DIM ** -0.5 * LOG2_E)).astype(BF16)

    n_groups = N_HEADS // hg
    acc_ref[...] = jnp.zeros((n_groups, KV_RANK, width), F32)

    def chunk(j, carry):
        ms, ls = carry
        d = 2 * (i - j)
        start = pl.multiple_of(j * qb, qb)
        mk = mask_ref[0, pl.ds(start, qb), :]
        c_j = c_ref[0, j]
        ct_j = ct_ref[0, j]
        scores = []
        for g in range(n_groups):
            tiles = []
            for hh in range(hg):
                h = g * hg + hh
                t_diag = tz_ref[d, h]
                tiles.append(mk + jnp.concatenate(
                    [jnp.concatenate([t_diag, tz_ref[d + 1, h]], axis=1),
                     jnp.concatenate([tz_ref[jnp.maximum(d - 1, 0), h], t_diag], axis=1)], axis=0))
            s = jnp.dot(c_j, ql_ref[g], preferred_element_type=F32)
            scores.append(s + jnp.concatenate(tiles, axis=1))
        new_ms, alphas, probs = [], [], []
        for g in range(n_groups):
            m_new = jnp.maximum(ms[g], jnp.max(scores[g], axis=0, keepdims=True))
            probs.append(jnp.exp2(scores[g] - m_new).astype(BF16))
            alphas.append(jnp.exp2(ms[g] - m_new))
            new_ms.append(m_new)
        new_ls = []
        for g in range(n_groups):
            pv = jnp.dot(ct_j, probs[g], preferred_element_type=F32)
            new_ls.append(alphas[g] * ls[g] + pv[KV_RANK:KV_RANK + 1])
            acc_ref[g] = alphas[g] * acc_ref[g] + pv[:KV_RANK]
        return tuple(new_ms), tuple(new_ls)

    init = (tuple(jnp.full((1, width), M_INIT, F32) for _ in range(n_groups)),
            tuple(jnp.zeros((1, width), F32) for _ in range(n_groups)))
    _, ls = lax.fori_loop(0, i + 1, chunk, init)
    for g in range(n_groups):
        o_lat = (acc_ref[g] / ls[g]).astype(BF16)
        for hh in range(hg):
            h = g * hg + hh
            ot_ref[h] = jnp.dot(wuvt_ref[h], o_lat[:, hh * qb:(hh + 1) * qb],
                                preferred_element_type=F32)
    o_ref[0] = ot_ref[...].reshape(ATTN_W, qb).T.astype(o_ref.dtype)


def dsa_attention(p1, qcol, c, ct, mask, tz, wuk, wuvt):
    b, s, _ = p1.shape
    nq = s // DSA_QB
    const = lambda shape: pl.BlockSpec(shape, lambda bi, i: (0,) * len(shape))
    return pl.pallas_call(
        _dsa_attn_kernel,
        grid=(b, nq),
        in_specs=[pl.BlockSpec((1, DSA_QB, ATTN_W), lambda bi, i: (bi, i, qcol)),
                  pl.BlockSpec((1, nq, DSA_QB, KV_RANK), lambda bi, i: (bi, 0, 0, 0)),
                  pl.BlockSpec((1, nq, KV_RANK + ONES_ROWS, DSA_QB), lambda bi, i: (bi, 0, 0, 0)),
                  pl.BlockSpec((1, s, DSA_QB), lambda bi, i: (bi, 0, i)),
                  pl.BlockSpec(tz.shape, lambda bi, i: (0, 0, 0, 0), pipeline_mode=pl.Buffered(1)),
                  const(wuk.shape), const(wuvt.shape)],
        out_specs=pl.BlockSpec((1, DSA_QB, ATTN_W), lambda bi, i: (bi, i, 0)),
        out_shape=jax.ShapeDtypeStruct((b, s, ATTN_W), BF16),
        scratch_shapes=[pltpu.VMEM((N_HEADS // DSA_HEAD_GROUP, KV_RANK, DSA_HEAD_GROUP * DSA_QB), BF16),
                        pltpu.VMEM((N_HEADS, HEAD_DIM, DSA_QB), F32),
                        pltpu.VMEM((N_HEADS // DSA_HEAD_GROUP, KV_RANK, DSA_HEAD_GROUP * DSA_QB), F32)],
        compiler_params=_params("parallel", "arbitrary"),
        name="dsa_attention",
    )(p1, c, ct, mask, tz, wuk, wuvt)


def _rmsnorm_kernel(x_ref, g_ref, o_ref):
    o_ref[...] = _rms(x_ref[...], g_ref[...])


def rmsnorm(x, g, tm):
    t, d = x.shape
    return pl.pallas_call(
        _rmsnorm_kernel,
        grid=(t // tm,),
        in_specs=[pl.BlockSpec((tm, d), lambda i: (i, 0)), pl.BlockSpec((1, d), lambda i: (0, 0))],
        out_specs=pl.BlockSpec((tm, d), lambda i: (i, 0)),
        out_shape=jax.ShapeDtypeStruct((t, d), F32),
        compiler_params=_params("parallel"),
        name="final_rmsnorm",
    )(x, g.reshape(1, d))


def _rel_bucket(dist):
    n = np.maximum(dist, 0)
    max_exact = REL_BUCKETS // 2
    nf = np.maximum(n, 1).astype(np.float32)
    large = max_exact + (np.log(nf / np.float32(max_exact)) / np.float32(math.log(REL_MAX_DIST / max_exact))
                         * np.float32(REL_BUCKETS - max_exact)).astype(np.int32)
    large = np.minimum(large, REL_BUCKETS - 1)
    return np.where(n < max_exact, n, large)


def _bias_table_kernel(bkt_ref, rb_ref, o_ref):
    bkt = bkt_ref[0]
    for h in range(N_HEADS):
        acc = jnp.full(bkt.shape, NEG_INF, F32)
        for k in range(REL_BUCKETS):
            acc = jnp.where(bkt == k, rb_ref[k, h], acc)
        o_ref[0, h] = acc


def bias_table(bkt, rel_bias):
    n, r, c = bkt.shape
    return pl.pallas_call(
        _bias_table_kernel,
        grid=(n,),
        in_specs=[pl.BlockSpec((1, r, c), lambda i: (i, 0, 0)),
                  pl.BlockSpec(memory_space=pltpu.SMEM)],
        out_specs=pl.BlockSpec((1, N_HEADS, r, c), lambda i: (i, 0, 0, 0)),
        out_shape=jax.ShapeDtypeStruct((n, N_HEADS, r, c), F32),
        compiler_params=_params("parallel"),
        name="bias_table",
    )(bkt, rel_bias.astype(F32))


def _dilated_buckets(window, dilation):
    qi = np.arange(BLOCK)[:, None]
    ki = np.arange(2 * BLOCK)[None, :]
    rel = qi + BLOCK - ki
    bkt = _rel_bucket(rel * dilation)
    band = (rel >= 0) & (rel <= window // dilation)
    return np.stack([np.where(band, bkt, -1),
                     np.where(band & (ki >= BLOCK), bkt, -1)]).astype(np.int32)


def _dsa_buckets(s):
    nd = s // BLOCK
    key = np.arange(BLOCK)[:, None]
    qry = np.arange(BLOCK)[None, :]
    dist = np.arange(nd)[:, None, None] * BLOCK + (qry - key)[None]
    return _rel_bucket(dist).astype(np.int32)


def kernel(x, mem, rel_bias, mem_norm, final_norm, mixer_norm, ffn_norm, w_mem_kv, w_out,
           even_w_in, even_w_gate, even_w_up, even_w_down,
           odd_w_in, odd_kv_norm, odd_w_uk, odd_w_uv,
           odd_w_router, odd_w_gate, odd_w_up, odd_w_down):
    b, s, d = x.shape
    t = b * s
    depth = mixer_norm.shape[0]
    xt = x.reshape(t, d)
    mem2 = mem.reshape(b * N_MEM, d)
    dil_tables = [bias_table(_dilated_buckets(w, dl), rel_bias) for w, dl in DIL_PAIRS]
    tz = bias_table(_dsa_buckets(s), rel_bias * LOG2_E)

    for i in range(depth):
        j = i // 2
        kvm = norm_matmul(mem2, mem_norm, w_mem_kv[i].astype(BF16), BF16, 512, 512)
        kvm = kvm.reshape(b, N_MEM, 2 * CROSS_WIDTH)
        wo = w_out[i].astype(BF16)
        if i % 2 == 0:
            w_in = even_w_in[j].astype(BF16)
            proj = even_in_projection(xt, mixer_norm[i], w_in[:, :5 * ATTN_W], b, s, 512)
            qc = norm_matmul(xt, mixer_norm[i], w_in[:, 5 * ATTN_W:], BF16, 512, CROSS_WIDTH)
            outs, lses = [], []
            for grp in range(len(DIL_PAIRS)):
                o, lse = dilated_branch(proj[grp], proj[3 + grp], proj[6 + grp], dil_tables[grp])
                outs.append(o)
                lses.append(lse)
            mix = combine_groups(outs, lses, 256).reshape(t, ATTN_W)
            cross = cross_attention(qc.reshape(b, s, CROSS_WIDTH), 0, kvm, 512)
        else:
            o_q, o_c, o_qi, o_ki, o_wi, o_qc = 0, 1024, 1280, 1792, 1856, 1864
            w = odd_w_in[j]
            pad = jnp.zeros((d, 128 - IDX_DIM - N_IDX_HEADS), w.dtype)
            w_in = jnp.concatenate(
                [w[:, o_q:o_c], w[:, o_qi:o_ki], w[:, o_c:o_qi], w[:, o_qc:],
                 w[:, o_ki:o_wi], w[:, o_wi:o_qc], pad], axis=1).astype(BF16)
            p1 = norm_matmul(xt, mixer_norm[i], w_in, BF16, 512, w_in.shape[1])
            p1 = p1.reshape(b, s, w_in.shape[1])
            c, ct = kv_latent_norm(p1, 6, odd_kv_norm[j])
            mask = dsa_select(p1, 2, 16)
            wuk = jnp.transpose(odd_w_uk[j], (1, 0, 2)).astype(BF16)
            wuvt = jnp.transpose(odd_w_uv[j], (1, 2, 0)).astype(BF16)
            mix = dsa_attention(p1, 0, c, ct, mask, tz, wuk, wuvt).reshape(t, ATTN_W)
            cross = cross_attention(p1, 7, kvm, 512)
        cross = cross.reshape(t, CROSS_WIDTH)
        if i % 2 == 0:
            xt = out_projection_swiglu(mix, cross, wo[:ATTN_W], wo[ATTN_W:], xt, ffn_norm[i],
                                       even_w_gate[j].astype(BF16), even_w_up[j].astype(BF16),
                                       even_w_down[j].astype(BF16), 512)
        else:
            xt = out_projection(mix, cross, wo[:ATTN_W], wo[ATTN_W:], xt, 512)
            wr = jnp.pad(odd_w_router[j], ((0, 0), (0, 128 - N_EXPERTS)))
            xt = moe_ffn(xt, ffn_norm[i], wr, odd_w_gate[j].astype(BF16), odd_w_up[j].astype(BF16),
                         odd_w_down[j].astype(BF16), final_norm if i == depth - 1 else None)
    if depth % 2 == 1:
        xt = rmsnorm(xt, final_norm, 512)
    return xt.reshape(b, s, d)
```

```python
import functools
import math

import jax
import jax.numpy as jnp
import numpy as np
from jax import lax
from jax.experimental import pallas as pl
from jax.experimental.pallas import tpu as pltpu

D_MODEL = 1024
N_HEADS = 16
HEAD_DIM = 64
ATTN_W = N_HEADS * HEAD_DIM
DIL_PAIRS = ((128, 1), (512, 4), (2048, 16))
BLOCK = 128
LANES = 128
KV_RANK = 256
N_IDX_HEADS = 8
IDX_DIM = 64
TOPK = 256
N_MEM = 256
N_CROSS_HEADS = 4
CROSS_WIDTH = 256
REL_BUCKETS = 32
REL_MAX_DIST = 2048
N_EXPERTS = 8
RMS_EPS = 1e-6
NEG_INF = -1e30
M_INIT = -5e29
LOG2_E = math.log2(math.e)

F32 = jnp.float32
BF16 = jnp.bfloat16
V7X_VMEM_LIMIT = 56 * 1024 * 1024
DSA_QB = 256
DSA_HEAD_GROUP = 8
ONES_ROWS = 16
INT_MIN = -(2 ** 31)
KEY_OF_NEG_INF = (0xFF800000 ^ 0x7FFFFFFF) - (1 << 32)
MOE_TILE = 512
ROUTE_TM = 512
MOE_FF_CHUNK = 1792

_NT = (((1,), (1,)), ((), ()))


def _params(*sem):
    return pltpu.CompilerParams(dimension_semantics=sem, vmem_limit_bytes=V7X_VMEM_LIMIT)


def _rms(x, g):
    return x * lax.rsqrt(jnp.mean(x * x, axis=-1, keepdims=True) + RMS_EPS) * g


def _norm_mm_kernel(x_ref, g_ref, w_ref, o_ref, xn_ref):
    @pl.when(pl.program_id(1) == 0)
    def _():
        xn_ref[...] = _rms(x_ref[...].astype(F32), g_ref[...]).astype(BF16)

    o_ref[...] = jnp.dot(xn_ref[...], w_ref[...], preferred_element_type=F32).astype(o_ref.dtype)


def norm_matmul(x, g, w, out_dtype, tm, tn):
    t, k = x.shape
    n = w.shape[1]
    return pl.pallas_call(
        _norm_mm_kernel,
        grid=(t // tm, n // tn),
        in_specs=[pl.BlockSpec((tm, k), lambda i, j: (i, 0)),
                  pl.BlockSpec((1, k), lambda i, j: (0, 0)),
                  pl.BlockSpec((k, tn), lambda i, j: (0, j))],
        out_specs=pl.BlockSpec((tm, tn), lambda i, j: (i, j)),
        out_shape=jax.ShapeDtypeStruct((t, n), out_dtype),
        scratch_shapes=[pltpu.VMEM((tm, k), BF16)],
        compiler_params=_params("parallel", "arbitrary"),
        name="norm_matmul",
    )(x, g.reshape(1, k), w)


def _even_proj_kernel(x_ref, g_ref, w_ref, q1_ref, q4_ref, q16_ref, k1_ref, k4_ref, k16_ref,
                      v1_ref, v4_ref, v16_ref, xn_ref, y_ref):
    j = pl.program_id(1)
    tm = x_ref.shape[0]

    @pl.when(j == 0)
    def _():
        xn_ref[...] = _rms(x_ref[...], g_ref[...]).astype(BF16)

    y = jnp.dot(xn_ref[...], w_ref[...], preferred_element_type=F32)
    n_lane_blocks = y.shape[1] // LANES
    for c in range(n_lane_blocks):
        y_ref[c] = y[:, c * LANES:(c + 1) * LANES]

    def put(dst_ref, d):
        if d == 1:
            dst_ref[0, 0] = y.astype(BF16)
            return
        for r in range(d):
            for c in range(n_lane_blocks):
                rows = y_ref[c, pl.ds(r, tm // d, stride=d), :]
                dst_ref[0, r, :, c * LANES:(c + 1) * LANES] = rows.astype(BF16)

    column_dsts = (((q1_ref, 1),), ((q4_ref, 4),), ((q16_ref, 16),),
                   ((k1_ref, 1), (k4_ref, 4), (k16_ref, 16)),
                   ((v1_ref, 1), (v4_ref, 4), (v16_ref, 16)))
    for col, dsts in enumerate(column_dsts):
        @pl.when(j == col)
        def _(dsts=dsts):
            for ref, d in dsts:
                put(ref, d)


def even_in_projection(x, g, w, b, s, tm):
    t, kdim = x.shape
    per_b = s // tm
    dils = [dl for _, dl in DIL_PAIRS]
    layouts = dils + dils + dils
    spec = lambda dl: pl.BlockSpec((1, dl, tm // dl, ATTN_W), lambda i, j: (i // per_b, 0, i % per_b, 0))
    return pl.pallas_call(
        _even_proj_kernel,
        grid=(t // tm, 5),
        in_specs=[pl.BlockSpec((tm, kdim), lambda i, j: (i, 0)),
                  pl.BlockSpec((1, kdim), lambda i, j: (0, 0)),
                  pl.BlockSpec((kdim, ATTN_W), lambda i, j: (0, j))],
        out_specs=[spec(dl) for dl in layouts],
        out_shape=[jax.ShapeDtypeStruct((b, dl, s // dl, ATTN_W), BF16) for dl in layouts],
        scratch_shapes=[pltpu.VMEM((tm, kdim), BF16), pltpu.VMEM((ATTN_W // LANES, tm, LANES), F32)],
        compiler_params=_params("parallel", "arbitrary"),
        name="even_in_projection",
    )(x, g.reshape(1, kdim), w)


def _dil_block(q_at, kprev_at, kcur_at, vprev_at, vcur_at, first, put, tab_ref, s_ref, p_ref, m_ref):
    pair = 2 * HEAD_DIM
    lo_q = lax.broadcasted_iota(jnp.int32, (BLOCK, pair), 1) < HEAD_DIM
    lo_k = lax.broadcasted_iota(jnp.int32, (2 * BLOCK, pair), 1) < HEAD_DIM
    ones_bd = jnp.concatenate([jnp.where(lo_k, 1.0, 0.0), jnp.where(lo_k, 0.0, 1.0)], axis=0).astype(BF16)
    scale = jnp.asarray(HEAD_DIM ** -0.5, BF16)
    zero = jnp.zeros((), BF16)
    n_pairs = N_HEADS // 2
    for p in range(n_pairs):
        cols = slice(p * pair, (p + 1) * pair)
        q = q_at(cols) * scale
        k = jnp.concatenate([kprev_at(cols), kcur_at(cols)], axis=0)
        q_ab = jnp.concatenate([jnp.where(lo_q, q, zero), jnp.where(lo_q, zero, q)], axis=0)
        s_ab = lax.dot_general(q_ab, k, _NT, preferred_element_type=F32)
        s_ref[2 * p] = s_ab[:BLOCK] + tab_ref[first, 2 * p]
        s_ref[2 * p + 1] = s_ab[BLOCK:] + tab_ref[first, 2 * p + 1]
    for p in range(n_pairs):
        sa = s_ref[2 * p]
        sb = s_ref[2 * p + 1]
        ma = jnp.max(sa, axis=-1, keepdims=True)
        mb = jnp.max(sb, axis=-1, keepdims=True)
        p_ref[p, :, :2 * BLOCK] = jnp.exp(sa - ma).astype(BF16)
        p_ref[p, :, 2 * BLOCK:] = jnp.exp(sb - mb).astype(BF16)
        m_ref[p] = jnp.where(lo_q, ma, mb)
    for p in range(n_pairs):
        cols = slice(p * pair, (p + 1) * pair)
        v = jnp.concatenate([vprev_at(cols), vcur_at(cols)], axis=0)
        v_bd = jnp.concatenate([jnp.where(lo_k, v, zero), jnp.where(lo_k, zero, v)], axis=0)
        ol = jnp.dot(p_ref[p], jnp.concatenate([v_bd, ones_bd], axis=1), preferred_element_type=F32)
        o, l = ol[:, :pair], ol[:, pair:]
        put(cols, o / l, m_ref[p] + jnp.log(l))


def _dil_kernel(q_ref, kp_ref, kc_ref, vp_ref, vc_ref, tab_ref, o_ref, lse_ref, s_ref, p_ref, m_ref,
                *, pair_residues):
    lo_rows, hi_rows = slice(0, BLOCK), slice(BLOCK, 2 * BLOCK)
    for u in range(2):
        if pair_residues:
            at = lambda ref: (lambda cols: ref[0, u, :, cols])
            q_at, kprev_at, kcur_at, vprev_at, vcur_at = at(q_ref), at(kp_ref), at(kc_ref), at(vp_ref), at(vc_ref)
            first = 1

            def put(cols, o, lse):
                o_ref[0, u, :, cols] = o.astype(o_ref.dtype)
                lse_ref[0, u, :, cols] = lse
        else:
            rows = lo_rows if u == 0 else hi_rows
            q_at = lambda cols, rows=rows: q_ref[0, 0, rows, cols]
            if u == 0:
                kprev_at = lambda cols: kp_ref[0, 0, :, cols]
                vprev_at = lambda cols: vp_ref[0, 0, :, cols]
                first = (pl.program_id(2) == 0).astype(jnp.int32)
            else:
                kprev_at = lambda cols: kc_ref[0, 0, lo_rows, cols]
                vprev_at = lambda cols: vc_ref[0, 0, lo_rows, cols]
                first = 0
            kcur_at = lambda cols, rows=rows: kc_ref[0, 0, rows, cols]
            vcur_at = lambda cols, rows=rows: vc_ref[0, 0, rows, cols]

            def put(cols, o, lse, rows=rows):
                o_ref[0, 0, rows, cols] = o.astype(o_ref.dtype)
                lse_ref[0, 0, rows, cols] = lse
        _dil_block(q_at, kprev_at, kcur_at, vprev_at, vcur_at, first, put, tab_ref,
                   s_ref.at[u], p_ref.at[u], m_ref.at[u])


def dilated_branch(q, k, v, table):
    b, dilation, ln, _ = q.shape
    nb = ln // BLOCK
    pair_residues = nb == 1
    if pair_residues:
        grid = (b, dilation // 2, 1)
        cur = pl.BlockSpec((1, 2, BLOCK, ATTN_W), lambda bi, r, n: (bi, r, 0, 0))
        prev = cur
    else:
        grid = (b, dilation, nb // 2)
        cur = pl.BlockSpec((1, 1, 2 * BLOCK, ATTN_W), lambda bi, r, n: (bi, r, n, 0))
        prev = pl.BlockSpec((1, 1, BLOCK, ATTN_W), lambda bi, r, n: (bi, r, jnp.maximum(2 * n - 1, 0), 0))
    return pl.pallas_call(
        functools.partial(_dil_kernel, pair_residues=pair_residues),
        grid=grid,
        in_specs=[cur, prev, cur, prev, cur,
                  pl.BlockSpec(table.shape, lambda bi, r, n: (0, 0, 0, 0))],
        out_specs=[cur, cur],
        out_shape=[jax.ShapeDtypeStruct(q.shape, BF16), jax.ShapeDtypeStruct(q.shape, F32)],
        scratch_shapes=[pltpu.VMEM((2, N_HEADS, BLOCK, 2 * BLOCK), F32),
                        pltpu.VMEM((2, N_HEADS // 2, BLOCK, 4 * BLOCK), BF16),
                        pltpu.VMEM((2, N_HEADS // 2, BLOCK, 2 * HEAD_DIM), F32)],
        compiler_params=_params("parallel", "parallel", "arbitrary"),
        name=f"dilated_attn_d{dilation}",
    )(q, k, k, v, v, table)


def _combine_kernel(o1, o4, o16, l1, l4, l16, out_ref, so4, sl4, so16, sl16):
    tm = out_ref.shape[1]
    n_lane_blocks = out_ref.shape[2] // LANES

    def to_token_order(src_ref, dst_ref, d):
        for r in range(d):
            for c in range(n_lane_blocks):
                rows = src_ref[0, r, :, c * LANES:(c + 1) * LANES]
                dst_ref[c, pl.ds(r, tm // d, stride=d), :] = rows.astype(F32)

    to_token_order(o4, so4, 4)
    to_token_order(l4, sl4, 4)
    to_token_order(o16, so16, 16)
    to_token_order(l16, sl16, 16)
    for c in range(n_lane_blocks):
        cols = slice(c * LANES, (c + 1) * LANES)
        a1, a2, a3 = l1[0, 0, :, cols], sl4[c], sl16[c]
        m = jnp.maximum(jnp.maximum(a1, a2), a3)
        w1, w2, w3 = jnp.exp(a1 - m), jnp.exp(a2 - m), jnp.exp(a3 - m)
        num = w1 * o1[0, 0, :, cols] + w2 * so4[c] + w3 * so16[c]
        out_ref[0, :, cols] = (num / (w1 + w2 + w3)).astype(out_ref.dtype)


def combine_groups(outs, lses, tm):
    b, _, s, w = outs[0].shape
    spec = lambda dl: pl.BlockSpec((1, dl, tm // dl, w), lambda bi, i: (bi, 0, i, 0))
    specs = [spec(o.shape[1]) for o in outs]
    return pl.pallas_call(
        _combine_kernel,
        grid=(b, s // tm),
        in_specs=specs + specs,
        out_specs=pl.BlockSpec((1, tm, w), lambda bi, i: (bi, i, 0)),
        out_shape=jax.ShapeDtypeStruct((b, s, w), BF16),
        scratch_shapes=[pltpu.VMEM((w // LANES, tm, LANES), F32)] * 4,
        compiler_params=_params("parallel", "parallel"),
        name="combine_groups",
    )(*outs, *lses)


def _cross_kernel(q_ref, kv_ref, o_ref):
    for h in range(N_CROSS_HEADS):
        sl = slice(h * HEAD_DIM, (h + 1) * HEAD_DIM)
        vsl = slice(CROSS_WIDTH + h * HEAD_DIM, CROSS_WIDTH + (h + 1) * HEAD_DIM)
        s = lax.dot_general(q_ref[0, :, sl], kv_ref[0, :, sl], _NT,
                            preferred_element_type=F32) * (HEAD_DIM ** -0.5)
        m = jnp.max(s, axis=-1, keepdims=True)
        p = jnp.exp(s - m)
        l = jnp.sum(p, axis=-1, keepdims=True)
        o = jnp.dot(p.astype(BF16), kv_ref[0, :, vsl], preferred_element_type=F32)
        o_ref[0, :, sl] = (o / l).astype(o_ref.dtype)


def cross_attention(qsrc, qcol, kv, tm):
    b, s, _ = qsrc.shape
    return pl.pallas_call(
        _cross_kernel,
        grid=(b, s // tm),
        in_specs=[pl.BlockSpec((1, tm, CROSS_WIDTH), lambda bi, i: (bi, i, qcol)),
                  pl.BlockSpec((1, N_MEM, 2 * CROSS_WIDTH), lambda bi, i: (bi, 0, 0))],
        out_specs=pl.BlockSpec((1, tm, CROSS_WIDTH), lambda bi, i: (bi, i, 0)),
        out_shape=jax.ShapeDtypeStruct((b, s, CROSS_WIDTH), BF16),
        compiler_params=_params("parallel", "parallel"),
        name="cross_attn",
    )(qsrc, kv)


def _outproj_kernel(mix_ref, cr_ref, wa_ref, wb_ref, x_ref, o_ref):
    acc = jnp.dot(mix_ref[...], wa_ref[...], preferred_element_type=F32)
    acc = acc + jnp.dot(cr_ref[...], wb_ref[...], preferred_element_type=F32)
    o_ref[...] = x_ref[...] + acc


def out_projection(mix, cross, wa, wb, x, tm):
    t, d = x.shape
    return pl.pallas_call(
        _outproj_kernel,
        grid=(t // tm,),
        in_specs=[pl.BlockSpec((tm, ATTN_W), lambda i: (i, 0)),
                  pl.BlockSpec((tm, CROSS_WIDTH), lambda i: (i, 0)),
                  pl.BlockSpec(wa.shape, lambda i: (0, 0)),
                  pl.BlockSpec(wb.shape, lambda i: (0, 0)),
                  pl.BlockSpec((tm, d), lambda i: (i, 0))],
        out_specs=pl.BlockSpec((tm, d), lambda i: (i, 0)),
        out_shape=jax.ShapeDtypeStruct((t, d), F32),
        compiler_params=_params("parallel"),
        name="out_projection",
    )(mix, cross, wa, wb, x)


def _outproj_swiglu_kernel(mix_ref, cr_ref, wa_ref, wb_ref, x_ref, g_ref, wg_ref, wu_ref, wd_ref, o_ref):
    x1 = x_ref[...] + jnp.dot(mix_ref[...], wa_ref[...], preferred_element_type=F32)
    x1 = x1 + jnp.dot(cr_ref[...], wb_ref[...], preferred_element_type=F32)
    hn = _rms(x1, g_ref[...]).astype(BF16)
    a = jnp.dot(hn, wg_ref[...], preferred_element_type=F32)
    u = jnp.dot(hn, wu_ref[...], preferred_element_type=F32)
    act = (a * jax.nn.sigmoid(a) * u).astype(BF16)
    o_ref[...] = x1 + jnp.dot(act, wd_ref[...], preferred_element_type=F32)


def out_projection_swiglu(mix, cross, wa, wb, x, g, wg, wu, wd, tm):
    t, d = x.shape
    resident = lambda w: pl.BlockSpec(w.shape, lambda i: (0, 0), pipeline_mode=pl.Buffered(1))
    return pl.pallas_call(
        _outproj_swiglu_kernel,
        grid=(t // tm,),
        in_specs=[pl.BlockSpec((tm, ATTN_W), lambda i: (i, 0)),
                  pl.BlockSpec((tm, CROSS_WIDTH), lambda i: (i, 0)),
                  resident(wa), resident(wb),
                  pl.BlockSpec((tm, d), lambda i: (i, 0)),
                  pl.BlockSpec((1, d), lambda i: (0, 0)),
                  resident(wg), resident(wu), resident(wd)],
        out_specs=pl.BlockSpec((tm, d), lambda i: (i, 0)),
        out_shape=jax.ShapeDtypeStruct((t, d), F32),
        compiler_params=_params("parallel"),
        name="out_projection_swiglu",
    )(mix, cross, wa, wb, x, g.reshape(1, d), wg, wu, wd)


def _router_kernel(x_ref, g_ref, wr_ref, gates_ref, route_ref, counts_ref, run_ref):
    @pl.when(pl.program_id(0) == 0)
    def _():
        run_ref[...] = jnp.zeros(run_ref.shape, F32)

    hn = _rms(x_ref[...], g_ref[...])
    logits = jnp.dot(hn, wr_ref[...], preferred_element_type=F32,
                     precision=lax.Precision.HIGHEST)
    tm = logits.shape[0]
    lane = lax.broadcasted_iota(jnp.int32, logits.shape, 1)
    lg = jnp.where(lane < N_EXPERTS, logits, -jnp.inf)
    m1 = jnp.max(lg, axis=-1, keepdims=True)
    i1 = jnp.min(jnp.where(lg == m1, lane, 128), axis=-1, keepdims=True)
    lg2 = jnp.where(lane == i1, -jnp.inf, lg)
    m2 = jnp.max(lg2, axis=-1, keepdims=True)
    i2 = jnp.min(jnp.where(lg2 == m2, lane, 128), axis=-1, keepdims=True)
    e = jnp.exp(m2 - m1)
    gates_ref[...] = jnp.where(lane == 0, 1.0 / (1.0 + e), jnp.where(lane == 1, e / (1.0 + e), 0.0))

    assign = jnp.where((lane == i1) | (lane == i2), 1.0, 0.0)
    r = lax.broadcasted_iota(jnp.int32, (tm, tm), 0)
    c = lax.broadcasted_iota(jnp.int32, (tm, tm), 1)
    lower = jnp.where(r > c, 1.0, 0.0).astype(BF16)
    before = jnp.dot(lower, assign.astype(BF16), preferred_element_type=F32) + run_ref[...]
    rank1 = jnp.sum(jnp.where(lane == i1, before, 0.0), axis=-1, keepdims=True).astype(jnp.int32)
    rank2 = jnp.sum(jnp.where(lane == i2, before, 0.0), axis=-1, keepdims=True).astype(jnp.int32)
    route_ref[...] = jnp.where(lane == 0, i1, jnp.where(lane == 1, i2, jnp.where(
        lane == 2, rank1, jnp.where(lane == 3, rank2, 0))))
    run = run_ref[...] + jnp.sum(assign, axis=0, keepdims=True)
    run_ref[...] = run
    counts_ref[...] = run.astype(jnp.int32)


def router(x, g, wr_pad, tm):
    t, d = x.shape
    return pl.pallas_call(
        _router_kernel,
        grid=(t // tm,),
        in_specs=[pl.BlockSpec((tm, d), lambda i: (i, 0)),
                  pl.BlockSpec((1, d), lambda i: (0, 0)),
                  pl.BlockSpec((d, 128), lambda i: (0, 0))],
        out_specs=[pl.BlockSpec((tm, 128), lambda i: (i, 0)),
                   pl.BlockSpec((tm, 128), lambda i: (i, 0)),
                   pl.BlockSpec((1, 128), lambda i: (0, 0))],
        out_shape=[jax.ShapeDtypeStruct((t, 128), F32),
                   jax.ShapeDtypeStruct((t, 128), jnp.int32),
                   jax.ShapeDtypeStruct((1, 128), jnp.int32)],
        scratch_shapes=[pltpu.VMEM((1, 128), F32)],
        compiler_params=_params("arbitrary"),
        name="router",
    )(x, g.reshape(1, d), wr_pad)


def _row_copy(src, src_row, dst, dst_row, sem):
    return pltpu.make_async_copy(src.at[pl.ds(src_row, 1)], dst.at[pl.ds(dst_row, 1)], sem)


def _scatter_rows_kernel(d1_ref, d2_ref, x_ref, xs_in_ref, xs_ref, sem):
    del xs_in_ref
    tm = x_ref.shape[0]
    base = pl.program_id(0) * tm

    def issue(r, carry):
        _row_copy(x_ref, r, xs_ref, d1_ref[base + r], sem).start()
        _row_copy(x_ref, r, xs_ref, d2_ref[base + r], sem).start()
        return carry

    lax.fori_loop(0, tm, issue, 0, unroll=8)

    for _ in range(2):
        pltpu.make_async_copy(x_ref, xs_ref.at[pl.ds(0, tm)], sem).wait()


def scatter_rows(x, dest1, dest2, n_rows, tm):
    t, d = x.shape
    zeros = jnp.zeros((n_rows, d), x.dtype)
    return pl.pallas_call(
        _scatter_rows_kernel,
        grid_spec=pltpu.PrefetchScalarGridSpec(
            num_scalar_prefetch=2,
            grid=(t // tm,),
            in_specs=[pl.BlockSpec((tm, d), lambda i, d1, d2: (i, 0)),
                      pl.BlockSpec(memory_space=pl.ANY)],
            out_specs=pl.BlockSpec(memory_space=pl.ANY),
            scratch_shapes=[pltpu.SemaphoreType.DMA(())]),
        out_shape=jax.ShapeDtypeStruct((n_rows, d), x.dtype),
        input_output_aliases={3: 0},
        compiler_params=_params("arbitrary"),
        name="moe_scatter_rows",
    )(dest1, dest2, x, zeros)


def _moe_group_kernel(te_ref, nu_ref, x_ref, g_ref, wg_ref, wu_ref, wd_ref, o_ref, hn_ref):
    i = pl.program_id(0)
    j = pl.program_id(1)
    used = i < nu_ref[0]

    @pl.when(j == 0)
    def _():
        o_ref[...] = jnp.zeros(o_ref.shape, F32)

    @pl.when(used & (j == 0))
    def _():
        hn_ref[...] = _rms(x_ref[...], g_ref[...]).astype(BF16)

    @pl.when(used)
    def _():
        hn = hn_ref[...]
        a = jnp.dot(hn, wg_ref[...], preferred_element_type=F32)
        u = jnp.dot(hn, wu_ref[...], preferred_element_type=F32)
        act = (a * jax.nn.sigmoid(a) * u).astype(BF16)
        o_ref[...] += jnp.dot(act, wd_ref[...], preferred_element_type=F32)


def moe_group_ffn(xs, g, tile_expert, n_used, wg, wu, wd, tm, tf):
    p, d = xs.shape
    f = wg.shape[2]
    nj = f // tf
    chunk = lambda i, j, te, nu: jnp.where(i < nu[0], j, nj - 1)
    return pl.pallas_call(
        _moe_group_kernel,
        grid_spec=pltpu.PrefetchScalarGridSpec(
            num_scalar_prefetch=2,
            grid=(p // tm, nj),
            in_specs=[pl.BlockSpec((tm, d), lambda i, j, te, nu: (i, 0)),
                      pl.BlockSpec((1, d), lambda i, j, te, nu: (0, 0)),
                      pl.BlockSpec((None, d, tf), lambda i, j, te, nu: (te[i], 0, chunk(i, j, te, nu))),
                      pl.BlockSpec((None, d, tf), lambda i, j, te, nu: (te[i], 0, chunk(i, j, te, nu))),
                      pl.BlockSpec((None, tf, d), lambda i, j, te, nu: (te[i], chunk(i, j, te, nu), 0))],
            out_specs=pl.BlockSpec((tm, d), lambda i, j, te, nu: (i, 0)),
            scratch_shapes=[pltpu.VMEM((tm, d), BF16)]),
        out_shape=jax.ShapeDtypeStruct((p, d), F32),
        compiler_params=_params("arbitrary", "arbitrary"),
        name="moe_group_ffn",
    )(tile_expert, n_used, xs, g.reshape(1, d), wg, wu, wd)


def _moe_combine_kernel(d1_ref, d2_ref, x_ref, gates_ref, fg_ref, ys_ref, o_ref, y1_ref, y2_ref, sem,
                        *, final_norm):
    tm = x_ref.shape[0]
    i = pl.program_id(0)

    def issue(tile, slot):
        base = tile * tm

        def body(r, carry):
            _row_copy(ys_ref, d1_ref[base + r], y1_ref.at[slot], r, sem.at[slot]).start()
            _row_copy(ys_ref, d2_ref[base + r], y2_ref.at[slot], r, sem.at[slot]).start()
            return carry

        lax.fori_loop(0, tm, body, 0, unroll=8)

    @pl.when(i == 0)
    def _():
        issue(0, 0)

    @pl.when(i + 1 < pl.num_programs(0))
    def _():
        issue(i + 1, (i + 1) % 2)

    slot = i % 2
    pltpu.make_async_copy(ys_ref.at[pl.ds(0, tm)], y1_ref.at[slot], sem.at[slot]).wait()
    pltpu.make_async_copy(ys_ref.at[pl.ds(0, tm)], y2_ref.at[slot], sem.at[slot]).wait()
    gates = gates_ref[...]
    out = x_ref[...] + gates[:, 0:1] * y1_ref[slot] + gates[:, 1:2] * y2_ref[slot]
    if final_norm:
        out = _rms(out, fg_ref[...])
    o_ref[...] = out


def moe_combine(x, gates, ys, dest1, dest2, tm, final_gain=None):
    t, d = x.shape
    fg = jnp.ones((1, d), F32) if final_gain is None else final_gain.reshape(1, d)
    return pl.pallas_call(
        functools.partial(_moe_combine_kernel, final_norm=final_gain is not None),
        grid_spec=pltpu.PrefetchScalarGridSpec(
            num_scalar_prefetch=2,
            grid=(t // tm,),
            in_specs=[pl.BlockSpec((tm, d), lambda i, d1, d2: (i, 0)),
                      pl.BlockSpec((tm, 128), lambda i, d1, d2: (i, 0)),
                      pl.BlockSpec((1, d), lambda i, d1, d2: (0, 0)),
                      pl.BlockSpec(memory_space=pl.ANY)],
            out_specs=pl.BlockSpec((tm, d), lambda i, d1, d2: (i, 0)),
            scratch_shapes=[pltpu.VMEM((2, tm, d), F32), pltpu.VMEM((2, tm, d), F32),
                            pltpu.SemaphoreType.DMA((2,))]),
        out_shape=jax.ShapeDtypeStruct((t, d), F32),
        compiler_params=_params("arbitrary"),
        name="moe_combine",
    )(dest1, dest2, x, gates, fg, ys)


def moe_ffn(x, g, wr_pad, wg, wu, wd, final_gain=None):
    t, d = x.shape
    ne = wg.shape[0]
    gates, route, counts = router(x, g, wr_pad, ROUTE_TM)
    counts = counts[0, :ne]
    padded = (counts + MOE_TILE - 1) // MOE_TILE * MOE_TILE
    ends = jnp.cumsum(padded)
    starts = ends - padded
    expert_ids = jnp.arange(ne, dtype=jnp.int32)[None, :]
    start_of = lambda e: jnp.sum(jnp.where(e[:, None] == expert_ids, starts[None, :], 0), axis=1)
    dest1 = (start_of(route[:, 0]) + route[:, 2]).astype(jnp.int32)
    dest2 = (start_of(route[:, 1]) + route[:, 3]).astype(jnp.int32)
    n_tiles = (2 * t) // MOE_TILE + ne
    tile_start = jnp.arange(n_tiles, dtype=jnp.int32) * MOE_TILE
    tile_expert = jnp.minimum(jnp.sum(tile_start[:, None] >= ends[None, :], axis=1), ne - 1).astype(jnp.int32)
    n_used = (ends[-1:] // MOE_TILE).astype(jnp.int32)
    xs = scatter_rows(x, dest1, dest2, n_tiles * MOE_TILE, ROUTE_TM)
    ys = moe_group_ffn(xs, g, tile_expert, n_used, wg, wu, wd, MOE_TILE, MOE_FF_CHUNK)
    return moe_combine(x, gates, ys, dest1, dest2, ROUTE_TM, final_gain)


def _kvnorm_kernel(c_ref, g_ref, o_ref, ot_ref):
    for j in range(o_ref.shape[1]):
        y = _rms(c_ref[0, j * DSA_QB:(j + 1) * DSA_QB].astype(F32), g_ref[...])
        o_ref[0, j] = y.astype(BF16)
        ot_ref[0, j, :KV_RANK] = y.T.astype(BF16)
        ot_ref[0, j, KV_RANK:] = jnp.ones((ONES_ROWS, DSA_QB), BF16)


def kv_latent_norm(p1, ccol, g):
    b, s, _ = p1.shape
    nc = s // DSA_QB
    out = jax.ShapeDtypeStruct((b, nc, DSA_QB, KV_RANK), BF16)
    out_t = jax.ShapeDtypeStruct((b, nc, KV_RANK + ONES_ROWS, DSA_QB), BF16)
    blk = pl.BlockSpec((1, nc, DSA_QB, KV_RANK), lambda bi: (bi, 0, 0, 0))
    blk_t = pl.BlockSpec((1, nc, KV_RANK + ONES_ROWS, DSA_QB), lambda bi: (bi, 0, 0, 0))
    return pl.pallas_call(
        _kvnorm_kernel,
        grid=(b,),
        in_specs=[pl.BlockSpec((1, s, KV_RANK), lambda bi: (bi, 0, ccol)),
                  pl.BlockSpec((1, KV_RANK), lambda bi: (0, 0))],
        out_specs=[blk, blk_t],
        out_shape=[out, out_t],
        compiler_params=_params("parallel"),
        name="kv_latent_norm",
    )(p1, g.reshape(1, KV_RANK))


def _select_kernel(qi_ref, kall_ref, kq_ref, mask_ref, sc_ref, jcut_ref, *, nchunks_total):
    qb = DSA_QB
    i = pl.program_id(1)
    nch = i + 1
    shape = (qb, qb)
    row = lax.broadcasted_iota(jnp.int32, shape, 0)
    qpos = i * qb + lax.broadcasted_iota(jnp.int32, shape, 1)

    wt = kq_ref[0].astype(F32).T
    wscale = (N_IDX_HEADS ** -0.5) * (IDX_DIM ** -0.5)

    def score_chunk(j, carry):
        start = pl.multiple_of(j * qb, qb)
        kc = kall_ref[0, pl.ds(start, qb), :][:, :IDX_DIM]
        acc = jnp.zeros(shape, F32)
        for h in range(N_IDX_HEADS):
            qh = qi_ref[0, :, h * IDX_DIM:(h + 1) * IDX_DIM]
            s = lax.dot_general(kc, qh, _NT, preferred_element_type=F32)
            acc = acc + jnp.maximum(s, 0.0) * (wt[IDX_DIM + h:IDX_DIM + h + 1, :] * wscale)
        sc_ref[j] = jnp.where(j * qb + row <= qpos, acc, NEG_INF)
        return carry

    lax.fori_loop(0, nch, score_chunk, 0)

    def count(pred):
        def body(j, c):
            hit = jnp.where(pred(sc_ref[j], j), 1, 0)
            return c + jnp.sum(hit.reshape(qb // 8, 8, qb), axis=0)
        c = lax.fori_loop(0, nch, body, jnp.zeros((8, qb), jnp.int32))
        return jnp.sum(c, axis=0, keepdims=True)

    def key_to_float(x):
        return pltpu.bitcast(x ^ ((x >> 31) & 0x7FFFFFFF), F32)

    def rank_reached(x):
        cand = key_to_float(x)
        return (count(lambda s, j: s >= cand) >= TOPK) | (x < KEY_OF_NEG_INF)

    thr_key = jnp.where(rank_reached(jnp.zeros((1, qb), jnp.int32)), 0, INT_MIN)

    def bit_body(t, thr_key):
        cand = thr_key + jnp.left_shift(jnp.int32(1), 30 - t)
        return jnp.where(rank_reached(cand), cand, thr_key)

    thr = key_to_float(lax.fori_loop(0, 31, bit_body, thr_key))

    n_gt = count(lambda s, j: s > thr)
    n_ge = count(lambda s, j: s >= thr)
    need = TOPK - n_gt

    jcut_ref[...] = jnp.full((1, qb), 1 << 30, jnp.int32)

    @pl.when(jnp.max(n_ge) > TOPK)
    def _():
        def idx_body(t, x):
            cand = x + jnp.left_shift(jnp.int32(1), 10 - t)
            below = count(lambda s, j: (s == thr) & (j * qb + row <= cand - 1))
            return jnp.where(below < need, cand, x)
        jcut_ref[...] = lax.fori_loop(0, 11, idx_body, jnp.zeros((1, qb), jnp.int32))

    jcut = jcut_ref[...]

    def write_chunk(j, carry):
        s = sc_ref[j]
        spos = j * qb + row
        sel = (s > thr) | ((s == thr) & (spos <= jcut))
        sel = sel & (spos <= qpos)
        start = pl.multiple_of(j * qb, qb)
        mask_ref[0, pl.ds(start, qb), :] = jnp.where(sel, 0.0, NEG_INF)
        return carry

    lax.fori_loop(0, nch, write_chunk, 0)

    def fill_chunk(j, carry):
        start = pl.multiple_of(j * qb, qb)
        mask_ref[0, pl.ds(start, qb), :] = jnp.full(shape, NEG_INF, F32)
        return carry

    lax.fori_loop(nch, nchunks_total, fill_chunk, 0)


def dsa_select(p1, qicol, kwcol):
    b, s, _ = p1.shape
    nq = s // DSA_QB
    return pl.pallas_call(
        functools.partial(_select_kernel, nchunks_total=nq),
        grid=(b, nq),
        in_specs=[pl.BlockSpec((1, DSA_QB, N_IDX_HEADS * IDX_DIM), lambda bi, i: (bi, i, qicol)),
                  pl.BlockSpec((1, s, 128), lambda bi, i: (bi, 0, kwcol)),
                  pl.BlockSpec((1, DSA_QB, 128), lambda bi, i: (bi, i, kwcol))],
        out_specs=pl.BlockSpec((1, s, DSA_QB), lambda bi, i: (bi, 0, i)),
        out_shape=jax.ShapeDtypeStruct((b, s, s), F32),
        scratch_shapes=[pltpu.VMEM((nq, DSA_QB, DSA_QB), F32),
                        pltpu.VMEM((1, DSA_QB), jnp.int32)],
        compiler_params=_params("parallel", "arbitrary"),
        name="dsa_select",
    )(p1, p1, p1)


def _dsa_attn_kernel(q_ref, c_ref, ct_ref, mask_ref, tz_ref, wuk_ref, wuvt_ref, o_ref,
                     ql_ref, ot_ref, acc_ref):
    qb = DSA_QB
    hg = DSA_HEAD_GROUP
    width = hg * qb
    i = pl.program_id(1)

    for h in range(N_HEADS):
        qh = q_ref[0, :, h * HEAD_DIM:(h + 1) * HEAD_DIM]
        ql = lax.dot_general(wuk_ref[h], qh, _NT, preferred_element_type=F32)
        ql_ref[h // hg, :, (h % hg) * qb:(h % hg + 1) * qb] = (
            ql * (HEAD_DIM ** -0.5 * LOG2_E)).astype(BF16)

    n_groups = N_HEADS // hg
    acc_ref[...] = jnp.zeros((n_groups, KV_RANK, width), F32)

    def chunk(j, carry):
        ms, ls = carry
        d = 2 * (i - j)
        start = pl.multiple_of(j * qb, qb)
        mk = mask_ref[0, pl.ds(start, qb), :]
        c_j = c_ref[0, j]
        ct_j = ct_ref[0, j]
        scores = []
        for g in range(n_groups):
            tiles = []
            for hh in range(hg):
                h = g * hg + hh
                t_diag = tz_ref[d, h]
                tiles.append(mk + jnp.concatenate(
                    [jnp.concatenate([t_diag, tz_ref[d + 1, h]], axis=1),
                     jnp.concatenate([tz_ref[jnp.maximum(d - 1, 0), h], t_diag], axis=1)], axis=0))
            s = jnp.dot(c_j, ql_ref[g], preferred_element_type=F32)
            scores.append(s + jnp.concatenate(tiles, axis=1))
        new_ms, alphas, probs = [], [], []
        for g in range(n_groups):
            m_new = jnp.maximum(ms[g], jnp.max(scores[g], axis=0, keepdims=True))
            probs.append(jnp.exp2(scores[g] - m_new).astype(BF16))
            alphas.append(jnp.exp2(ms[g] - m_new))
            new_ms.append(m_new)
        new_ls = []
        for g in range(n_groups):
            pv = jnp.dot(ct_j, probs[g], preferred_element_type=F32)
            new_ls.append(alphas[g] * ls[g] + pv[KV_RANK:KV_RANK + 1])
            acc_ref[g] = alphas[g] * acc_ref[g] + pv[:KV_RANK]
        return tuple(new_ms), tuple(new_ls)

    init = (tuple(jnp.full((1, width), M_INIT, F32) for _ in range(n_groups)),
            tuple(jnp.zeros((1, width), F32) for _ in range(n_groups)))
    _, ls = lax.fori_loop(0, i + 1, chunk, init)
    for g in range(n_groups):
        o_lat = (acc_ref[g] / ls[g]).astype(BF16)
        for hh in range(hg):
            h = g * hg + hh
            ot_ref[h] = jnp.dot(wuvt_ref[h], o_lat[:, hh * qb:(hh + 1) * qb],
                                preferred_element_type=F32)
    o_ref[0] = ot_ref[...].reshape(ATTN_W, qb).T.astype(o_ref.dtype)


def dsa_attention(p1, qcol, c, ct, mask, tz, wuk, wuvt):
    b, s, _ = p1.shape
    nq = s // DSA_QB
    const = lambda shape: pl.BlockSpec(shape, lambda bi, i: (0,) * len(shape))
    return pl.pallas_call(
        _dsa_attn_kernel,
        grid=(b, nq),
        in_specs=[pl.BlockSpec((1, DSA_QB, ATTN_W), lambda bi, i: (bi, i, qcol)),
                  pl.BlockSpec((1, nq, DSA_QB, KV_RANK), lambda bi, i: (bi, 0, 0, 0)),
                  pl.BlockSpec((1, nq, KV_RANK + ONES_ROWS, DSA_QB), lambda bi, i: (bi, 0, 0, 0)),
                  pl.BlockSpec((1, s, DSA_QB), lambda bi, i: (bi, 0, i)),
                  pl.BlockSpec(tz.shape, lambda bi, i: (0, 0, 0, 0), pipeline_mode=pl.Buffered(1)),
                  const(wuk.shape), const(wuvt.shape)],
        out_specs=pl.BlockSpec((1, DSA_QB, ATTN_W), lambda bi, i: (bi, i, 0)),
        out_shape=jax.ShapeDtypeStruct((b, s, ATTN_W), BF16),
        scratch_shapes=[pltpu.VMEM((N_HEADS // DSA_HEAD_GROUP, KV_RANK, DSA_HEAD_GROUP * DSA_QB), BF16),
                        pltpu.VMEM((N_HEADS, HEAD_DIM, DSA_QB), F32),
                        pltpu.VMEM((N_HEADS // DSA_HEAD_GROUP, KV_RANK, DSA_HEAD_GROUP * DSA_QB), F32)],
        compiler_params=_params("parallel", "arbitrary"),
        name="dsa_attention",
    )(p1, c, ct, mask, tz, wuk, wuvt)


def _rmsnorm_kernel(x_ref, g_ref, o_ref):
    o_ref[...] = _rms(x_ref[...], g_ref[...])


def rmsnorm(x, g, tm):
    t, d = x.shape
    return pl.pallas_call(
        _rmsnorm_kernel,
        grid=(t // tm,),
        in_specs=[pl.BlockSpec((tm, d), lambda i: (i, 0)), pl.BlockSpec((1, d), lambda i: (0, 0))],
        out_specs=pl.BlockSpec((tm, d), lambda i: (i, 0)),
        out_shape=jax.ShapeDtypeStruct((t, d), F32),
        compiler_params=_params("parallel"),
        name="final_rmsnorm",
    )(x, g.reshape(1, d))


def _rel_bucket(dist):
    n = np.maximum(dist, 0)
    max_exact = REL_BUCKETS // 2
    nf = np.maximum(n, 1).astype(np.float32)
    large = max_exact + (np.log(nf / np.float32(max_exact)) / np.float32(math.log(REL_MAX_DIST / max_exact))
                         * np.float32(REL_BUCKETS - max_exact)).astype(np.int32)
    large = np.minimum(large, REL_BUCKETS - 1)
    return np.where(n < max_exact, n, large)


def _bias_table_kernel(bkt_ref, rb_ref, o_ref):
    bkt = bkt_ref[0]
    for h in range(N_HEADS):
        acc = jnp.full(bkt.shape, NEG_INF, F32)
        for k in range(REL_BUCKETS):
            acc = jnp.where(bkt == k, rb_ref[k, h], acc)
        o_ref[0, h] = acc


def bias_table(bkt, rel_bias):
    n, r, c = bkt.shape
    return pl.pallas_call(
        _bias_table_kernel,
        grid=(n,),
        in_specs=[pl.BlockSpec((1, r, c), lambda i: (i, 0, 0)),
                  pl.BlockSpec(memory_space=pltpu.SMEM)],
        out_specs=pl.BlockSpec((1, N_HEADS, r, c), lambda i: (i, 0, 0, 0)),
        out_shape=jax.ShapeDtypeStruct((n, N_HEADS, r, c), F32),
        compiler_params=_params("parallel"),
        name="bias_table",
    )(bkt, rel_bias.astype(F32))


def _dilated_buckets(window, dilation):
    qi = np.arange(BLOCK)[:, None]
    ki = np.arange(2 * BLOCK)[None, :]
    rel = qi + BLOCK - ki
    bkt = _rel_bucket(rel * dilation)
    band = (rel >= 0) & (rel <= window // dilation)
    return np.stack([np.where(band, bkt, -1),
                     np.where(band & (ki >= BLOCK), bkt, -1)]).astype(np.int32)


def _dsa_buckets(s):
    nd = s // BLOCK
    key = np.arange(BLOCK)[:, None]
    qry = np.arange(BLOCK)[None, :]
    dist = np.arange(nd)[:, None, None] * BLOCK + (qry - key)[None]
    return _rel_bucket(dist).astype(np.int32)


def kernel(x, mem, rel_bias, mem_norm, final_norm, mixer_norm, ffn_norm, w_mem_kv, w_out,
           even_w_in, even_w_gate, even_w_up, even_w_down,
           odd_w_in, odd_kv_norm, odd_w_uk, odd_w_uv,
           odd_w_router, odd_w_gate, odd_w_up, odd_w_down):
    b, s, d = x.shape
    t = b * s
    depth = mixer_norm.shape[0]
    xt = x.reshape(t, d)
    mem2 = mem.reshape(b * N_MEM, d)
    dil_tables = [bias_table(_dilated_buckets(w, dl), rel_bias) for w, dl in DIL_PAIRS]
    tz = bias_table(_dsa_buckets(s), rel_bias * LOG2_E)

    for i in range(depth):
        j = i // 2
        kvm = norm_matmul(mem2, mem_norm, w_mem_kv[i].astype(BF16), BF16, 512, 512)
        kvm = kvm.reshape(b, N_MEM, 2 * CROSS_WIDTH)
        wo = w_out[i].astype(BF16)
        if i % 2 == 0:
            w_in = even_w_in[j].astype(BF16)
            proj = even_in_projection(xt, mixer_norm[i], w_in[:, :5 * ATTN_W], b, s, 512)
            qc = norm_matmul(xt, mixer_norm[i], w_in[:, 5 * ATTN_W:], BF16, 512, CROSS_WIDTH)
            outs, lses = [], []
            for grp in range(len(DIL_PAIRS)):
                o, lse = dilated_branch(proj[grp], proj[3 + grp], proj[6 + grp], dil_tables[grp])
                outs.append(o)
                lses.append(lse)
            mix = combine_groups(outs, lses, 256).reshape(t, ATTN_W)
            cross = cross_attention(qc.reshape(b, s, CROSS_WIDTH), 0, kvm, 1024)
        else:
            o_q, o_c, o_qi, o_ki, o_wi, o_qc = 0, 1024, 1280, 1792, 1856, 1864
            w = odd_w_in[j]
            pad = jnp.zeros((d, 128 - IDX_DIM - N_IDX_HEADS), w.dtype)
            w_in = jnp.concatenate(
                [w[:, o_q:o_c], w[:, o_qi:o_ki], w[:, o_c:o_qi], w[:, o_qc:],
                 w[:, o_ki:o_wi], w[:, o_wi:o_qc], pad], axis=1).astype(BF16)
            p1 = norm_matmul(xt, mixer_norm[i], w_in, BF16, 512, w_in.shape[1])
            p1 = p1.reshape(b, s, w_in.shape[1])
            c, ct = kv_latent_norm(p1, 6, odd_kv_norm[j])
            mask = dsa_select(p1, 2, 16)
            wuk = jnp.transpose(odd_w_uk[j], (1, 0, 2)).astype(BF16)
            wuvt = jnp.transpose(odd_w_uv[j], (1, 2, 0)).astype(BF16)
            mix = dsa_attention(p1, 0, c, ct, mask, tz, wuk, wuvt).reshape(t, ATTN_W)
            cross = cross_attention(p1, 7, kvm, 1024)
        cross = cross.reshape(t, CROSS_WIDTH)
        if i % 2 == 0:
            xt = out_projection_swiglu(mix, cross, wo[:ATTN_W], wo[ATTN_W:], xt, ffn_norm[i],
                                       even_w_gate[j].astype(BF16), even_w_up[j].astype(BF16),
                                       even_w_down[j].astype(BF16), 512)
        else:
            xt = out_projection(mix, cross, wo[:ATTN_W], wo[ATTN_W:], xt, 512)
            wr = jnp.pad(odd_w_router[j], ((0, 0), (0, 128 - N_EXPERTS)))
            xt = moe_ffn(xt, ffn_norm[i], wr, odd_w_gate[j].astype(BF16), odd_w_up[j].astype(BF16),
                         odd_w_down[j].astype(BF16), final_norm if i == depth - 1 else None)
    if depth % 2 == 1:
        xt = rmsnorm(xt, final_norm, 512)
    return xt.reshape(b, s, d)
```

```python
import functools
import math

import jax
import jax.numpy as jnp
import numpy as np
from jax import lax
from jax.experimental import pallas as pl
from jax.experimental.pallas import tpu as pltpu

D_MODEL = 1024
N_HEADS = 16
HEAD_DIM = 64
ATTN_W = N_HEADS * HEAD_DIM
DIL_PAIRS = ((128, 1), (512, 4), (2048, 16))
BLOCK = 128
LANES = 128
DIL_STEP_BLOCKS = 4
KV_RANK = 256
N_IDX_HEADS = 8
IDX_DIM = 64
TOPK = 256
N_MEM = 256
N_CROSS_HEADS = 4
CROSS_WIDTH = 256
REL_BUCKETS = 32
REL_MAX_DIST = 2048
N_EXPERTS = 8
RMS_EPS = 1e-6
NEG_INF = -1e30
M_INIT = -5e29
LOG2_E = math.log2(math.e)

F32 = jnp.float32
BF16 = jnp.bfloat16
V7X_VMEM_LIMIT = 56 * 1024 * 1024
DSA_QB = 256
DSA_HEAD_GROUP = 8
ONES_ROWS = 16
INT_MIN = -(2 ** 31)
KEY_OF_NEG_INF = (0xFF800000 ^ 0x7FFFFFFF) - (1 << 32)
MOE_TILE = 512
ROUTE_TM = 512
MOE_FF_CHUNK = 1792

_NT = (((1,), (1,)), ((), ()))


def _params(*sem):
    return pltpu.CompilerParams(dimension_semantics=sem, vmem_limit_bytes=V7X_VMEM_LIMIT)


def _rms(x, g):
    return x * lax.rsqrt(jnp.mean(x * x, axis=-1, keepdims=True) + RMS_EPS) * g


def _norm_mm_kernel(x_ref, g_ref, w_ref, o_ref, xn_ref):
    @pl.when(pl.program_id(1) == 0)
    def _():
        xn_ref[...] = _rms(x_ref[...].astype(F32), g_ref[...]).astype(BF16)

    o_ref[...] = jnp.dot(xn_ref[...], w_ref[...], preferred_element_type=F32).astype(o_ref.dtype)


def norm_matmul(x, g, w, out_dtype, tm, tn):
    t, k = x.shape
    n = w.shape[1]
    return pl.pallas_call(
        _norm_mm_kernel,
        grid=(t // tm, n // tn),
        in_specs=[pl.BlockSpec((tm, k), lambda i, j: (i, 0)),
                  pl.BlockSpec((1, k), lambda i, j: (0, 0)),
                  pl.BlockSpec((k, tn), lambda i, j: (0, j))],
        out_specs=pl.BlockSpec((tm, tn), lambda i, j: (i, j)),
        out_shape=jax.ShapeDtypeStruct((t, n), out_dtype),
        scratch_shapes=[pltpu.VMEM((tm, k), BF16)],
        compiler_params=_params("parallel", "arbitrary"),
        name="norm_matmul",
    )(x, g.reshape(1, k), w)


def _even_proj_kernel(x_ref, g_ref, w_ref, q1_ref, q4_ref, q16_ref, k1_ref, k4_ref, k16_ref,
                      v1_ref, v4_ref, v16_ref, xn_ref, y_ref):
    j = pl.program_id(1)
    tm = x_ref.shape[0]

    @pl.when(j == 0)
    def _():
        xn_ref[...] = _rms(x_ref[...], g_ref[...]).astype(BF16)

    y = jnp.dot(xn_ref[...], w_ref[...], preferred_element_type=F32)
    n_lane_blocks = y.shape[1] // LANES
    for c in range(n_lane_blocks):
        y_ref[c] = y[:, c * LANES:(c + 1) * LANES]

    def put(dst_ref, d):
        if d == 1:
            dst_ref[0, 0] = y.astype(BF16)
            return
        for r in range(d):
            for c in range(n_lane_blocks):
                rows = y_ref[c, pl.ds(r, tm // d, stride=d), :]
                dst_ref[0, r, :, c * LANES:(c + 1) * LANES] = rows.astype(BF16)

    column_dsts = (((q1_ref, 1),), ((q4_ref, 4),), ((q16_ref, 16),),
                   ((k1_ref, 1), (k4_ref, 4), (k16_ref, 16)),
                   ((v1_ref, 1), (v4_ref, 4), (v16_ref, 16)))
    for col, dsts in enumerate(column_dsts):
        @pl.when(j == col)
        def _(dsts=dsts):
            for ref, d in dsts:
                put(ref, d)


def even_in_projection(x, g, w, b, s, tm):
    t, kdim = x.shape
    per_b = s // tm
    dils = [dl for _, dl in DIL_PAIRS]
    layouts = dils + dils + dils
    spec = lambda dl: pl.BlockSpec((1, dl, tm // dl, ATTN_W), lambda i, j: (i // per_b, 0, i % per_b, 0))
    return pl.pallas_call(
        _even_proj_kernel,
        grid=(t // tm, 5),
        in_specs=[pl.BlockSpec((tm, kdim), lambda i, j: (i, 0)),
                  pl.BlockSpec((1, kdim), lambda i, j: (0, 0)),
                  pl.BlockSpec((kdim, ATTN_W), lambda i, j: (0, j))],
        out_specs=[spec(dl) for dl in layouts],
        out_shape=[jax.ShapeDtypeStruct((b, dl, s // dl, ATTN_W), BF16) for dl in layouts],
        scratch_shapes=[pltpu.VMEM((tm, kdim), BF16), pltpu.VMEM((ATTN_W // LANES, tm, LANES), F32)],
        compiler_params=_params("parallel", "arbitrary"),
        name="even_in_projection",
    )(x, g.reshape(1, kdim), w)


def _dil_block(q_at, kprev_at, kcur_at, vprev_at, vcur_at, first, put, tab_ref, s_ref, p_ref, m_ref):
    pair = 2 * HEAD_DIM
    lo_q = lax.broadcasted_iota(jnp.int32, (BLOCK, pair), 1) < HEAD_DIM
    lo_k = lax.broadcasted_iota(jnp.int32, (2 * BLOCK, pair), 1) < HEAD_DIM
    ones_bd = jnp.concatenate([jnp.where(lo_k, 1.0, 0.0), jnp.where(lo_k, 0.0, 1.0)], axis=0).astype(BF16)
    scale = jnp.asarray(HEAD_DIM ** -0.5, BF16)
    zero = jnp.zeros((), BF16)
    n_pairs = N_HEADS // 2
    for p in range(n_pairs):
        cols = slice(p * pair, (p + 1) * pair)
        q = q_at(cols) * scale
        k = jnp.concatenate([kprev_at(cols), kcur_at(cols)], axis=0)
        q_ab = jnp.concatenate([jnp.where(lo_q, q, zero), jnp.where(lo_q, zero, q)], axis=0)
        s_ab = lax.dot_general(q_ab, k, _NT, preferred_element_type=F32)
        s_ref[2 * p] = s_ab[:BLOCK] + tab_ref[first, 2 * p]
        s_ref[2 * p + 1] = s_ab[BLOCK:] + tab_ref[first, 2 * p + 1]
    for p in range(n_pairs):
        sa = s_ref[2 * p]
        sb = s_ref[2 * p + 1]
        ma = jnp.max(sa, axis=-1, keepdims=True)
        mb = jnp.max(sb, axis=-1, keepdims=True)
        p_ref[p, :, :2 * BLOCK] = jnp.exp(sa - ma).astype(BF16)
        p_ref[p, :, 2 * BLOCK:] = jnp.exp(sb - mb).astype(BF16)
        m_ref[p] = jnp.where(lo_q, ma, mb)
    for p in range(n_pairs):
        cols = slice(p * pair, (p + 1) * pair)
        v = jnp.concatenate([vprev_at(cols), vcur_at(cols)], axis=0)
        v_bd = jnp.concatenate([jnp.where(lo_k, v, zero), jnp.where(lo_k, zero, v)], axis=0)
        ol = jnp.dot(p_ref[p], jnp.concatenate([v_bd, ones_bd], axis=1), preferred_element_type=F32)
        o, l = ol[:, :pair], ol[:, pair:]
        put(cols, o / l, m_ref[p] + jnp.log(l))


def _dil_kernel(q_ref, kp_ref, kc_ref, vp_ref, vc_ref, tab_ref, o_ref, lse_ref, s_ref, p_ref, m_ref,
                *, pair_residues):
    for u in range(DIL_STEP_BLOCKS):
        if pair_residues:
            at = lambda ref: (lambda cols: ref[0, u, :, cols])
            q_at, kprev_at, kcur_at, vprev_at, vcur_at = at(q_ref), at(kp_ref), at(kc_ref), at(vp_ref), at(vc_ref)
            first = 1

            def put(cols, o, lse):
                o_ref[0, u, :, cols] = o.astype(o_ref.dtype)
                lse_ref[0, u, :, cols] = lse
        else:
            rows = slice(u * BLOCK, (u + 1) * BLOCK)
            q_at = lambda cols, rows=rows: q_ref[0, 0, rows, cols]
            if u == 0:
                kprev_at = lambda cols: kp_ref[0, 0, :, cols]
                vprev_at = lambda cols: vp_ref[0, 0, :, cols]
                first = (pl.program_id(2) == 0).astype(jnp.int32)
            else:
                before = slice((u - 1) * BLOCK, u * BLOCK)
                kprev_at = lambda cols, before=before: kc_ref[0, 0, before, cols]
                vprev_at = lambda cols, before=before: vc_ref[0, 0, before, cols]
                first = 0
            kcur_at = lambda cols, rows=rows: kc_ref[0, 0, rows, cols]
            vcur_at = lambda cols, rows=rows: vc_ref[0, 0, rows, cols]

            def put(cols, o, lse, rows=rows):
                o_ref[0, 0, rows, cols] = o.astype(o_ref.dtype)
                lse_ref[0, 0, rows, cols] = lse
        _dil_block(q_at, kprev_at, kcur_at, vprev_at, vcur_at, first, put, tab_ref,
                   s_ref.at[u], p_ref.at[u], m_ref.at[u])


def dilated_branch(q, k, v, table):
    b, dilation, ln, _ = q.shape
    nb = ln // BLOCK
    pair_residues = nb == 1
    nsb = DIL_STEP_BLOCKS
    if pair_residues:
        grid = (b, dilation // nsb, 1)
        cur = pl.BlockSpec((1, nsb, BLOCK, ATTN_W), lambda bi, r, n: (bi, r, 0, 0))
        prev = cur
    else:
        grid = (b, dilation, nb // nsb)
        cur = pl.BlockSpec((1, 1, nsb * BLOCK, ATTN_W), lambda bi, r, n: (bi, r, n, 0))
        prev = pl.BlockSpec((1, 1, BLOCK, ATTN_W), lambda bi, r, n: (bi, r, jnp.maximum(nsb * n - 1, 0), 0))
    return pl.pallas_call(
        functools.partial(_dil_kernel, pair_residues=pair_residues),
        grid=grid,
        in_specs=[cur, prev, cur, prev, cur,
                  pl.BlockSpec(table.shape, lambda bi, r, n: (0, 0, 0, 0))],
        out_specs=[cur, cur],
        out_shape=[jax.ShapeDtypeStruct(q.shape, BF16), jax.ShapeDtypeStruct(q.shape, F32)],
        scratch_shapes=[pltpu.VMEM((nsb, N_HEADS, BLOCK, 2 * BLOCK), F32),
                        pltpu.VMEM((nsb, N_HEADS // 2, BLOCK, 4 * BLOCK), BF16),
                        pltpu.VMEM((nsb, N_HEADS // 2, BLOCK, 2 * HEAD_DIM), F32)],
        compiler_params=_params("parallel", "parallel", "arbitrary"),
        name=f"dilated_attn_d{dilation}",
    )(q, k, k, v, v, table)


def _combine_kernel(o1, o4, o16, l1, l4, l16, out_ref, so4, sl4, so16, sl16):
    tm = out_ref.shape[1]
    n_lane_blocks = out_ref.shape[2] // LANES

    def to_token_order(src_ref, dst_ref, d):
        for r in range(d):
            for c in range(n_lane_blocks):
                rows = src_ref[0, r, :, c * LANES:(c + 1) * LANES]
                dst_ref[c, pl.ds(r, tm // d, stride=d), :] = rows.astype(F32)

    to_token_order(o4, so4, 4)
    to_token_order(l4, sl4, 4)
    to_token_order(o16, so16, 16)
    to_token_order(l16, sl16, 16)
    for c in range(n_lane_blocks):
        cols = slice(c * LANES, (c + 1) * LANES)
        a1, a2, a3 = l1[0, 0, :, cols], sl4[c], sl16[c]
        m = jnp.maximum(jnp.maximum(a1, a2), a3)
        w1, w2, w3 = jnp.exp(a1 - m), jnp.exp(a2 - m), jnp.exp(a3 - m)
        num = w1 * o1[0, 0, :, cols] + w2 * so4[c] + w3 * so16[c]
        out_ref[0, :, cols] = (num / (w1 + w2 + w3)).astype(out_ref.dtype)


def combine_groups(outs, lses, tm):
    b, _, s, w = outs[0].shape
    spec = lambda dl: pl.BlockSpec((1, dl, tm // dl, w), lambda bi, i: (bi, 0, i, 0))
    specs = [spec(o.shape[1]) for o in outs]
    return pl.pallas_call(
        _combine_kernel,
        grid=(b, s // tm),
        in_specs=specs + specs,
        out_specs=pl.BlockSpec((1, tm, w), lambda bi, i: (bi, i, 0)),
        out_shape=jax.ShapeDtypeStruct((b, s, w), BF16),
        scratch_shapes=[pltpu.VMEM((w // LANES, tm, LANES), F32)] * 4,
        compiler_params=_params("parallel", "parallel"),
        name="combine_groups",
    )(*outs, *lses)


def _cross_kernel(q_ref, kv_ref, o_ref):
    for h in range(N_CROSS_HEADS):
        sl = slice(h * HEAD_DIM, (h + 1) * HEAD_DIM)
        vsl = slice(CROSS_WIDTH + h * HEAD_DIM, CROSS_WIDTH + (h + 1) * HEAD_DIM)
        s = lax.dot_general(q_ref[0, :, sl], kv_ref[0, :, sl], _NT,
                            preferred_element_type=F32) * (HEAD_DIM ** -0.5)
        m = jnp.max(s, axis=-1, keepdims=True)
        p = jnp.exp(s - m)
        l = jnp.sum(p, axis=-1, keepdims=True)
        o = jnp.dot(p.astype(BF16), kv_ref[0, :, vsl], preferred_element_type=F32)
        o_ref[0, :, sl] = (o / l).astype(o_ref.dtype)


def cross_attention(qsrc, qcol, kv, tm):
    b, s, _ = qsrc.shape
    return pl.pallas_call(
        _cross_kernel,
        grid=(b, s // tm),
        in_specs=[pl.BlockSpec((1, tm, CROSS_WIDTH), lambda bi, i: (bi, i, qcol)),
                  pl.BlockSpec((1, N_MEM, 2 * CROSS_WIDTH), lambda bi, i: (bi, 0, 0))],
        out_specs=pl.BlockSpec((1, tm, CROSS_WIDTH), lambda bi, i: (bi, i, 0)),
        out_shape=jax.ShapeDtypeStruct((b, s, CROSS_WIDTH), BF16),
        compiler_params=_params("parallel", "parallel"),
        name="cross_attn",
    )(qsrc, kv)


def _outproj_kernel(mix_ref, cr_ref, wa_ref, wb_ref, x_ref, o_ref):
    acc = jnp.dot(mix_ref[...], wa_ref[...], preferred_element_type=F32)
    acc = acc + jnp.dot(cr_ref[...], wb_ref[...], preferred_element_type=F32)
    o_ref[...] = x_ref[...] + acc


def out_projection(mix, cross, wa, wb, x, tm):
    t, d = x.shape
    return pl.pallas_call(
        _outproj_kernel,
        grid=(t // tm,),
        in_specs=[pl.BlockSpec((tm, ATTN_W), lambda i: (i, 0)),
                  pl.BlockSpec((tm, CROSS_WIDTH), lambda i: (i, 0)),
                  pl.BlockSpec(wa.shape, lambda i: (0, 0)),
                  pl.BlockSpec(wb.shape, lambda i: (0, 0)),
                  pl.BlockSpec((tm, d), lambda i: (i, 0))],
        out_specs=pl.BlockSpec((tm, d), lambda i: (i, 0)),
        out_shape=jax.ShapeDtypeStruct((t, d), F32),
        compiler_params=_params("parallel"),
        name="out_projection",
    )(mix, cross, wa, wb, x)


def _outproj_swiglu_kernel(mix_ref, cr_ref, wa_ref, wb_ref, x_ref, g_ref, wg_ref, wu_ref, wd_ref, o_ref):
    x1 = x_ref[...] + jnp.dot(mix_ref[...], wa_ref[...], preferred_element_type=F32)
    x1 = x1 + jnp.dot(cr_ref[...], wb_ref[...], preferred_element_type=F32)
    hn = _rms(x1, g_ref[...]).astype(BF16)
    a = jnp.dot(hn, wg_ref[...], preferred_element_type=F32)
    u = jnp.dot(hn, wu_ref[...], preferred_element_type=F32)
    act = (a * jax.nn.sigmoid(a) * u).astype(BF16)
    o_ref[...] = x1 + jnp.dot(act, wd_ref[...], preferred_element_type=F32)


def out_projection_swiglu(mix, cross, wa, wb, x, g, wg, wu, wd, tm):
    t, d = x.shape
    resident = lambda w: pl.BlockSpec(w.shape, lambda i: (0, 0), pipeline_mode=pl.Buffered(1))
    return pl.pallas_call(
        _outproj_swiglu_kernel,
        grid=(t // tm,),
        in_specs=[pl.BlockSpec((tm, ATTN_W), lambda i: (i, 0)),
                  pl.BlockSpec((tm, CROSS_WIDTH), lambda i: (i, 0)),
                  resident(wa), resident(wb),
                  pl.BlockSpec((tm, d), lambda i: (i, 0)),
                  pl.BlockSpec((1, d), lambda i: (0, 0)),
                  resident(wg), resident(wu), resident(wd)],
        out_specs=pl.BlockSpec((tm, d), lambda i: (i, 0)),
        out_shape=jax.ShapeDtypeStruct((t, d), F32),
        compiler_params=_params("parallel"),
        name="out_projection_swiglu",
    )(mix, cross, wa, wb, x, g.reshape(1, d), wg, wu, wd)


def _router_kernel(x_ref, g_ref, wr_ref, gates_ref, route_ref, counts_ref, run_ref):
    @pl.when(pl.program_id(0) == 0)
    def _():
        run_ref[...] = jnp.zeros(run_ref.shape, F32)

    hn = _rms(x_ref[...], g_ref[...])
    logits = jnp.dot(hn, wr_ref[...], preferred_element_type=F32,
                     precision=lax.Precision.HIGHEST)
    tm = logits.shape[0]
    lane = lax.broadcasted_iota(jnp.int32, logits.shape, 1)
    lg = jnp.where(lane < N_EXPERTS, logits, -jnp.inf)
    m1 = jnp.max(lg, axis=-1, keepdims=True)
    i1 = jnp.min(jnp.where(lg == m1, lane, 128), axis=-1, keepdims=True)
    lg2 = jnp.where(lane == i1, -jnp.inf, lg)
    m2 = jnp.max(lg2, axis=-1, keepdims=True)
    i2 = jnp.min(jnp.where(lg2 == m2, lane, 128), axis=-1, keepdims=True)
    e = jnp.exp(m2 - m1)
    gates_ref[...] = jnp.where(lane == 0, 1.0 / (1.0 + e), jnp.where(lane == 1, e / (1.0 + e), 0.0))

    assign = jnp.where((lane == i1) | (lane == i2), 1.0, 0.0)
    r = lax.broadcasted_iota(jnp.int32, (tm, tm), 0)
    c = lax.broadcasted_iota(jnp.int32, (tm, tm), 1)
    lower = jnp.where(r > c, 1.0, 0.0).astype(BF16)
    before = jnp.dot(lower, assign.astype(BF16), preferred_element_type=F32) + run_ref[...]
    rank1 = jnp.sum(jnp.where(lane == i1, before, 0.0), axis=-1, keepdims=True).astype(jnp.int32)
    rank2 = jnp.sum(jnp.where(lane == i2, before, 0.0), axis=-1, keepdims=True).astype(jnp.int32)
    route_ref[...] = jnp.where(lane == 0, i1, jnp.where(lane == 1, i2, jnp.where(
        lane == 2, rank1, jnp.where(lane == 3, rank2, 0))))
    run = run_ref[...] + jnp.sum(assign, axis=0, keepdims=True)
    run_ref[...] = run
    counts_ref[...] = run.astype(jnp.int32)


def router(x, g, wr_pad, tm):
    t, d = x.shape
    return pl.pallas_call(
        _router_kernel,
        grid=(t // tm,),
        in_specs=[pl.BlockSpec((tm, d), lambda i: (i, 0)),
                  pl.BlockSpec((1, d), lambda i: (0, 0)),
                  pl.BlockSpec((d, 128), lambda i: (0, 0))],
        out_specs=[pl.BlockSpec((tm, 128), lambda i: (i, 0)),
                   pl.BlockSpec((tm, 128), lambda i: (i, 0)),
                   pl.BlockSpec((1, 128), lambda i: (0, 0))],
        out_shape=[jax.ShapeDtypeStruct((t, 128), F32),
                   jax.ShapeDtypeStruct((t, 128), jnp.int32),
                   jax.ShapeDtypeStruct((1, 128), jnp.int32)],
        scratch_shapes=[pltpu.VMEM((1, 128), F32)],
        compiler_params=_params("arbitrary"),
        name="router",
    )(x, g.reshape(1, d), wr_pad)


def _row_copy(src, src_row, dst, dst_row, sem):
    return pltpu.make_async_copy(src.at[pl.ds(src_row, 1)], dst.at[pl.ds(dst_row, 1)], sem)


def _scatter_rows_kernel(d1_ref, d2_ref, x_ref, xs_in_ref, xs_ref, sem):
    del xs_in_ref
    tm = x_ref.shape[0]
    base = pl.program_id(0) * tm

    def issue(r, carry):
        _row_copy(x_ref, r, xs_ref, d1_ref[base + r], sem).start()
        _row_copy(x_ref, r, xs_ref, d2_ref[base + r], sem).start()
        return carry

    lax.fori_loop(0, tm, issue, 0, unroll=8)

    for _ in range(2):
        pltpu.make_async_copy(x_ref, xs_ref.at[pl.ds(0, tm)], sem).wait()


def scatter_rows(x, dest1, dest2, n_rows, tm):
    t, d = x.shape
    zeros = jnp.zeros((n_rows, d), x.dtype)
    return pl.pallas_call(
        _scatter_rows_kernel,
        grid_spec=pltpu.PrefetchScalarGridSpec(
            num_scalar_prefetch=2,
            grid=(t // tm,),
            in_specs=[pl.BlockSpec((tm, d), lambda i, d1, d2: (i, 0)),
                      pl.BlockSpec(memory_space=pl.ANY)],
            out_specs=pl.BlockSpec(memory_space=pl.ANY),
            scratch_shapes=[pltpu.SemaphoreType.DMA(())]),
        out_shape=jax.ShapeDtypeStruct((n_rows, d), x.dtype),
        input_output_aliases={3: 0},
        compiler_params=_params("arbitrary"),
        name="moe_scatter_rows",
    )(dest1, dest2, x, zeros)


def _moe_group_kernel(te_ref, nu_ref, x_ref, g_ref, wg_ref, wu_ref, wd_ref, o_ref, hn_ref):
    i = pl.program_id(0)
    j = pl.program_id(1)
    used = i < nu_ref[0]

    @pl.when(j == 0)
    def _():
        o_ref[...] = jnp.zeros(o_ref.shape, F32)

    @pl.when(used & (j == 0))
    def _():
        hn_ref[...] = _rms(x_ref[...], g_ref[...]).astype(BF16)

    @pl.when(used)
    def _():
        hn = hn_ref[...]
        a = jnp.dot(hn, wg_ref[...], preferred_element_type=F32)
        u = jnp.dot(hn, wu_ref[...], preferred_element_type=F32)
        act = (a * jax.nn.sigmoid(a) * u).astype(BF16)
        o_ref[...] += jnp.dot(act, wd_ref[...], preferred_element_type=F32)


def moe_group_ffn(xs, g, tile_expert, n_used, wg, wu, wd, tm, tf):
    p, d = xs.shape
    f = wg.shape[2]
    nj = f // tf
    chunk = lambda i, j, te, nu: jnp.where(i < nu[0], j, nj - 1)
    return pl.pallas_call(
        _moe_group_kernel,
        grid_spec=pltpu.PrefetchScalarGridSpec(
            num_scalar_prefetch=2,
            grid=(p // tm, nj),
            in_specs=[pl.BlockSpec((tm, d), lambda i, j, te, nu: (i, 0)),
                      pl.BlockSpec((1, d), lambda i, j, te, nu: (0, 0)),
                      pl.BlockSpec((None, d, tf), lambda i, j, te, nu: (te[i], 0, chunk(i, j, te, nu))),
                      pl.BlockSpec((None, d, tf), lambda i, j, te, nu: (te[i], 0, chunk(i, j, te, nu))),
                      pl.BlockSpec((None, tf, d), lambda i, j, te, nu: (te[i], chunk(i, j, te, nu), 0))],
            out_specs=pl.BlockSpec((tm, d), lambda i, j, te, nu: (i, 0)),
            scratch_shapes=[pltpu.VMEM((tm, d), BF16)]),
        out_shape=jax.ShapeDtypeStruct((p, d), F32),
        compiler_params=_params("arbitrary", "arbitrary"),
        name="moe_group_ffn",
    )(tile_expert, n_used, xs, g.reshape(1, d), wg, wu, wd)


def _moe_combine_kernel(d1_ref, d2_ref, x_ref, gates_ref, fg_ref, ys_ref, o_ref, y1_ref, y2_ref, sem,
                        *, final_norm):
    tm = x_ref.shape[0]
    i = pl.program_id(0)

    def issue(tile, slot):
        base = tile * tm

        def body(r, carry):
            _row_copy(ys_ref, d1_ref[base + r], y1_ref.at[slot], r, sem.at[slot]).start()
            _row_copy(ys_ref, d2_ref[base + r], y2_ref.at[slot], r, sem.at[slot]).start()
            return carry

        lax.fori_loop(0, tm, body, 0, unroll=8)

    @pl.when(i == 0)
    def _():
        issue(0, 0)

    @pl.when(i + 1 < pl.num_programs(0))
    def _():
        issue(i + 1, (i + 1) % 2)

    slot = i % 2
    pltpu.make_async_copy(ys_ref.at[pl.ds(0, tm)], y1_ref.at[slot], sem.at[slot]).wait()
    pltpu.make_async_copy(ys_ref.at[pl.ds(0, tm)], y2_ref.at[slot], sem.at[slot]).wait()
    gates = gates_ref[...]
    out = x_ref[...] + gates[:, 0:1] * y1_ref[slot] + gates[:, 1:2] * y2_ref[slot]
    if final_norm:
        out = _rms(out, fg_ref[...])
    o_ref[...] = out


def moe_combine(x, gates, ys, dest1, dest2, tm, final_gain=None):
    t, d = x.shape
    fg = jnp.ones((1, d), F32) if final_gain is None else final_gain.reshape(1, d)
    return pl.pallas_call(
        functools.partial(_moe_combine_kernel, final_norm=final_gain is not None),
        grid_spec=pltpu.PrefetchScalarGridSpec(
            num_scalar_prefetch=2,
            grid=(t // tm,),
            in_specs=[pl.BlockSpec((tm, d), lambda i, d1, d2: (i, 0)),
                      pl.BlockSpec((tm, 128), lambda i, d1, d2: (i, 0)),
                      pl.BlockSpec((1, d), lambda i, d1, d2: (0, 0)),
                      pl.BlockSpec(memory_space=pl.ANY)],
            out_specs=pl.BlockSpec((tm, d), lambda i, d1, d2: (i, 0)),
            scratch_shapes=[pltpu.VMEM((2, tm, d), F32), pltpu.VMEM((2, tm, d), F32),
                            pltpu.SemaphoreType.DMA((2,))]),
        out_shape=jax.ShapeDtypeStruct((t, d), F32),
        compiler_params=_params("arbitrary"),
        name="moe_combine",
    )(dest1, dest2, x, gates, fg, ys)


def moe_ffn(x, g, wr_pad, wg, wu, wd, final_gain=None):
    t, d = x.shape
    ne = wg.shape[0]
    gates, route, counts = router(x, g, wr_pad, ROUTE_TM)
    counts = counts[0, :ne]
    padded = (counts + MOE_TILE - 1) // MOE_TILE * MOE_TILE
    ends = jnp.cumsum(padded)
    starts = ends - padded
    expert_ids = jnp.arange(ne, dtype=jnp.int32)[None, :]
    start_of = lambda e: jnp.sum(jnp.where(e[:, None] == expert_ids, starts[None, :], 0), axis=1)
    dest1 = (start_of(route[:, 0]) + route[:, 2]).astype(jnp.int32)
    dest2 = (start_of(route[:, 1]) + route[:, 3]).astype(jnp.int32)
    n_tiles = (2 * t) // MOE_TILE + ne
    tile_start = jnp.arange(n_tiles, dtype=jnp.int32) * MOE_TILE
    tile_expert = jnp.minimum(jnp.sum(tile_start[:, None] >= ends[None, :], axis=1), ne - 1).astype(jnp.int32)
    n_used = (ends[-1:] // MOE_TILE).astype(jnp.int32)
    xs = scatter_rows(x, dest1, dest2, n_tiles * MOE_TILE, ROUTE_TM)
    ys = moe_group_ffn(xs, g, tile_expert, n_used, wg, wu, wd, MOE_TILE, MOE_FF_CHUNK)
    return moe_combine(x, gates, ys, dest1, dest2, ROUTE_TM, final_gain)


def _kvnorm_kernel(c_ref, g_ref, o_ref, ot_ref):
    for j in range(o_ref.shape[1]):
        y = _rms(c_ref[0, j * DSA_QB:(j + 1) * DSA_QB].astype(F32), g_ref[...])
        o_ref[0, j] = y.astype(BF16)
        ot_ref[0, j, :KV_RANK] = y.T.astype(BF16)
        ot_ref[0, j, KV_RANK:] = jnp.ones((ONES_ROWS, DSA_QB), BF16)


def kv_latent_norm(p1, ccol, g):
    b, s, _ = p1.shape
    nc = s // DSA_QB
    out = jax.ShapeDtypeStruct((b, nc, DSA_QB, KV_RANK), BF16)
    out_t = jax.ShapeDtypeStruct((b, nc, KV_RANK + ONES_ROWS, DSA_QB), BF16)
    blk = pl.BlockSpec((1, nc, DSA_QB, KV_RANK), lambda bi: (bi, 0, 0, 0))
    blk_t = pl.BlockSpec((1, nc, KV_RANK + ONES_ROWS, DSA_QB), lambda bi: (bi, 0, 0, 0))
    return pl.pallas_call(
        _kvnorm_kernel,
        grid=(b,),
        in_specs=[pl.BlockSpec((1, s, KV_RANK), lambda bi: (bi, 0, ccol)),
                  pl.BlockSpec((1, KV_RANK), lambda bi: (0, 0))],
        out_specs=[blk, blk_t],
        out_shape=[out, out_t],
        compiler_params=_params("parallel"),
        name="kv_latent_norm",
    )(p1, g.reshape(1, KV_RANK))


def _select_kernel(qi_ref, kall_ref, kq_ref, mask_ref, sc_ref, jcut_ref, *, nchunks_total):
    qb = DSA_QB
    i = pl.program_id(1)
    nch = i + 1
    shape = (qb, qb)
    row = lax.broadcasted_iota(jnp.int32, shape, 0)
    qpos = i * qb + lax.broadcasted_iota(jnp.int32, shape, 1)

    wt = kq_ref[0].astype(F32).T
    wscale = (N_IDX_HEADS ** -0.5) * (IDX_DIM ** -0.5)

    def score_chunk(j, carry):
        start = pl.multiple_of(j * qb, qb)
        kc = kall_ref[0, pl.ds(start, qb), :][:, :IDX_DIM]
        acc = jnp.zeros(shape, F32)
        for h in range(N_IDX_HEADS):
            qh = qi_ref[0, :, h * IDX_DIM:(h + 1) * IDX_DIM]
            s = lax.dot_general(kc, qh, _NT, preferred_element_type=F32)
            acc = acc + jnp.maximum(s, 0.0) * (wt[IDX_DIM + h:IDX_DIM + h + 1, :] * wscale)
        sc_ref[j] = jnp.where(j * qb + row <= qpos, acc, NEG_INF)
        return carry

    lax.fori_loop(0, nch, score_chunk, 0)

    def count(pred):
        def body(j, c):
            hit = jnp.where(pred(sc_ref[j], j), 1, 0)
            return c + jnp.sum(hit.reshape(qb // 8, 8, qb), axis=0)
        c = lax.fori_loop(0, nch, body, jnp.zeros((8, qb), jnp.int32))
        return jnp.sum(c, axis=0, keepdims=True)

    def key_to_float(x):
        return pltpu.bitcast(x ^ ((x >> 31) & 0x7FFFFFFF), F32)

    def rank_reached(x):
        cand = key_to_float(x)
        return (count(lambda s, j: s >= cand) >= TOPK) | (x < KEY_OF_NEG_INF)

    thr_key = jnp.where(rank_reached(jnp.zeros((1, qb), jnp.int32)), 0, INT_MIN)

    def bit_body(t, thr_key):
        cand = thr_key + jnp.left_shift(jnp.int32(1), 30 - t)
        return jnp.where(rank_reached(cand), cand, thr_key)

    thr = key_to_float(lax.fori_loop(0, 31, bit_body, thr_key))

    n_gt = count(lambda s, j: s > thr)
    n_ge = count(lambda s, j: s >= thr)
    need = TOPK - n_gt

    jcut_ref[...] = jnp.full((1, qb), 1 << 30, jnp.int32)

    @pl.when(jnp.max(n_ge) > TOPK)
    def _():
        def idx_body(t, x):
            cand = x + jnp.left_shift(jnp.int32(1), 10 - t)
            below = count(lambda s, j: (s == thr) & (j * qb + row <= cand - 1))
            return jnp.where(below < need, cand, x)
        jcut_ref[...] = lax.fori_loop(0, 11, idx_body, jnp.zeros((1, qb), jnp.int32))

    jcut = jcut_ref[...]

    def write_chunk(j, carry):
        s = sc_ref[j]
        spos = j * qb + row
        sel = (s > thr) | ((s == thr) & (spos <= jcut))
        sel = sel & (spos <= qpos)
        start = pl.multiple_of(j * qb, qb)
        mask_ref[0, pl.ds(start, qb), :] = jnp.where(sel, 0.0, NEG_INF)
        return carry

    lax.fori_loop(0, nch, write_chunk, 0)

    def fill_chunk(j, carry):
        start = pl.multiple_of(j * qb, qb)
        mask_ref[0, pl.ds(start, qb), :] = jnp.full(shape, NEG_INF, F32)
        return carry

    lax.fori_loop(nch, nchunks_total, fill_chunk, 0)


def dsa_select(p1, qicol, kwcol):
    b, s, _ = p1.shape
    nq = s // DSA_QB
    return pl.pallas_call(
        functools.partial(_select_kernel, nchunks_total=nq),
        grid=(b, nq),
        in_specs=[pl.BlockSpec((1, DSA_QB, N_IDX_HEADS * IDX_DIM), lambda bi, i: (bi, i, qicol)),
                  pl.BlockSpec((1, s, 128), lambda bi, i: (bi, 0, kwcol)),
                  pl.BlockSpec((1, DSA_QB, 128), lambda bi, i: (bi, i, kwcol))],
        out_specs=pl.BlockSpec((1, s, DSA_QB), lambda bi, i: (bi, 0, i)),
        out_shape=jax.ShapeDtypeStruct((b, s, s), F32),
        scratch_shapes=[pltpu.VMEM((nq, DSA_QB, DSA_QB), F32),
                        pltpu.VMEM((1, DSA_QB), jnp.int32)],
        compiler_params=_params("parallel", "arbitrary"),
        name="dsa_select",
    )(p1, p1, p1)


def _dsa_attn_kernel(q_ref, c_ref, ct_ref, mask_ref, tz_ref, wuk_ref, wuvt_ref, o_ref,
                     ql_ref, ot_ref, acc_ref):
    qb = DSA_QB
    hg = DSA_HEAD_GROUP
    width = hg * qb
    i = pl.program_id(1)

    for h in range(N_HEADS):
        qh = q_ref[0, :, h * HEAD_DIM:(h + 1) * HEAD_DIM]
        ql = lax.dot_general(wuk_ref[h], qh, _NT, preferred_element_type=F32)
        ql_ref[h // hg, :, (h % hg) * qb:(h % hg + 1) * qb] = (
            ql * (HEAD_DIM ** -0.5 * LOG2_E)).astype(BF16)

    n_groups = N_HEADS // hg
    acc_ref[...] = jnp.zeros((n_groups, KV_RANK, width), F32)

    def chunk(j, carry):
        ms, ls = carry
        d = 2 * (i - j)
        start = pl.multiple_of(j * qb, qb)
        mk = mask_ref[0, pl.ds(start, qb), :]
        c_j = c_ref[0, j]
        ct_j = ct_ref[0, j]
        scores = []
        for g in range(n_groups):
            tiles = []
            for hh in range(hg):
                h = g * hg + hh
                t_diag = tz_ref[d, h]
                tiles.append(mk + jnp.concatenate(
                    [jnp.concatenate([t_diag, tz_ref[d + 1, h]], axis=1),
                     jnp.concatenate([tz_ref[jnp.maximum(d - 1, 0), h], t_diag], axis=1)], axis=0))
            s = jnp.dot(c_j, ql_ref[g], preferred_element_type=F32)
            scores.append(s + jnp.concatenate(tiles, axis=1))
        new_ms, alphas, probs = [], [], []
        for g in range(n_groups):
            m_new = jnp.maximum(ms[g], jnp.max(scores[g], axis=0, keepdims=True))
            probs.append(jnp.exp2(scores[g] - m_new).astype(BF16))
            alphas.append(jnp.exp2(ms[g] - m_new))
            new_ms.append(m_new)
        new_ls = []
        for g in range(n_groups):
            pv = jnp.dot(ct_j, probs[g], preferred_element_type=F32)
            new_ls.append(alphas[g] * ls[g] + pv[KV_RANK:KV_RANK + 1])
            acc_ref[g] = alphas[g] * acc_ref[g] + pv[:KV_RANK]
        return tuple(new_ms), tuple(new_ls)

    init = (tuple(jnp.full((1, width), M_INIT, F32) for _ in range(n_groups)),
            tuple(jnp.zeros((1, width), F32) for _ in range(n_groups)))
    _, ls = lax.fori_loop(0, i + 1, chunk, init)
    for g in range(n_groups):
        o_lat = (acc_ref[g] / ls[g]).astype(BF16)
        for hh in range(hg):
            h = g * hg + hh
            ot_ref[h] = jnp.dot(wuvt_ref[h], o_lat[:, hh * qb:(hh + 1) * qb],
                                preferred_element_type=F32)
    o_ref[0] = ot_ref[...].reshape(ATTN_W, qb).T.astype(o_ref.dtype)


def dsa_attention(p1, qcol, c, ct, mask, tz, wuk, wuvt):
    b, s, _ = p1.shape
    nq = s // DSA_QB
    const = lambda shape: pl.BlockSpec(shape, lambda bi, i: (0,) * len(shape))
    return pl.pallas_call(
        _dsa_attn_kernel,
        grid=(b, nq),
        in_specs=[pl.BlockSpec((1, DSA_QB, ATTN_W), lambda bi, i: (bi, i, qcol)),
                  pl.BlockSpec((1, nq, DSA_QB, KV_RANK), lambda bi, i: (bi, 0, 0, 0)),
                  pl.BlockSpec((1, nq, KV_RANK + ONES_ROWS, DSA_QB), lambda bi, i: (bi, 0, 0, 0)),
                  pl.BlockSpec((1, s, DSA_QB), lambda bi, i: (bi, 0, i)),
                  pl.BlockSpec(tz.shape, lambda bi, i: (0, 0, 0, 0), pipeline_mode=pl.Buffered(1)),
                  const(wuk.shape), const(wuvt.shape)],
        out_specs=pl.BlockSpec((1, DSA_QB, ATTN_W), lambda bi, i: (bi, i, 0)),
        out_shape=jax.ShapeDtypeStruct((b, s, ATTN_W), BF16),
        scratch_shapes=[pltpu.VMEM((N_HEADS // DSA_HEAD_GROUP, KV_RANK, DSA_HEAD_GROUP * DSA_QB), BF16),
                        pltpu.VMEM((N_HEADS, HEAD_DIM, DSA_QB), F32),
                        pltpu.VMEM((N_HEADS // DSA_HEAD_GROUP, KV_RANK, DSA_HEAD_GROUP * DSA_QB), F32)],
        compiler_params=_params("parallel", "arbitrary"),
        name="dsa_attention",
    )(p1, c, ct, mask, tz, wuk, wuvt)


def _rmsnorm_kernel(x_ref, g_ref, o_ref):
    o_ref[...] = _rms(x_ref[...], g_ref[...])


def rmsnorm(x, g, tm):
    t, d = x.shape
    return pl.pallas_call(
        _rmsnorm_kernel,
        grid=(t // tm,),
        in_specs=[pl.BlockSpec((tm, d), lambda i: (i, 0)), pl.BlockSpec((1, d), lambda i: (0, 0))],
        out_specs=pl.BlockSpec((tm, d), lambda i: (i, 0)),
        out_shape=jax.ShapeDtypeStruct((t, d), F32),
        compiler_params=_params("parallel"),
        name="final_rmsnorm",
    )(x, g.reshape(1, d))


def _rel_bucket(dist):
    n = np.maximum(dist, 0)
    max_exact = REL_BUCKETS // 2
    nf = np.maximum(n, 1).astype(np.float32)
    large = max_exact + (np.log(nf / np.float32(max_exact)) / np.float32(math.log(REL_MAX_DIST / max_exact))
                         * np.float32(REL_BUCKETS - max_exact)).astype(np.int32)
    large = np.minimum(large, REL_BUCKETS - 1)
    return np.where(n < max_exact, n, large)


def _bias_table_kernel(bkt_ref, rb_ref, o_ref):
    bkt = bkt_ref[0]
    for h in range(N_HEADS):
        acc = jnp.full(bkt.shape, NEG_INF, F32)
        for k in range(REL_BUCKETS):
            acc = jnp.where(bkt == k, rb_ref[k, h], acc)
        o_ref[0, h] = acc


def bias_table(bkt, rel_bias):
    n, r, c = bkt.shape
    return pl.pallas_call(
        _bias_table_kernel,
        grid=(n,),
        in_specs=[pl.BlockSpec((1, r, c), lambda i: (i, 0, 0)),
                  pl.BlockSpec(memory_space=pltpu.SMEM)],
        out_specs=pl.BlockSpec((1, N_HEADS, r, c), lambda i: (i, 0, 0, 0)),
        out_shape=jax.ShapeDtypeStruct((n, N_HEADS, r, c), F32),
        compiler_params=_params("parallel"),
        name="bias_table",
    )(bkt, rel_bias.astype(F32))


def _dilated_buckets(window, dilation):
    qi = np.arange(BLOCK)[:, None]
    ki = np.arange(2 * BLOCK)[None, :]
    rel = qi + BLOCK - ki
    bkt = _rel_bucket(rel * dilation)
    band = (rel >= 0) & (rel <= window // dilation)
    return np.stack([np.where(band, bkt, -1),
                     np.where(band & (ki >= BLOCK), bkt, -1)]).astype(np.int32)


def _dsa_buckets(s):
    nd = s // BLOCK
    key = np.arange(BLOCK)[:, None]
    qry = np.arange(BLOCK)[None, :]
    dist = np.arange(nd)[:, None, None] * BLOCK + (qry - key)[None]
    return _rel_bucket(dist).astype(np.int32)


def kernel(x, mem, rel_bias, mem_norm, final_norm, mixer_norm, ffn_norm, w_mem_kv, w_out,
           even_w_in, even_w_gate, even_w_up, even_w_down,
           odd_w_in, odd_kv_norm, odd_w_uk, odd_w_uv,
           odd_w_router, odd_w_gate, odd_w_up, odd_w_down):
    b, s, d = x.shape
    t = b * s
    depth = mixer_norm.shape[0]
    xt = x.reshape(t, d)
    mem2 = mem.reshape(b * N_MEM, d)
    dil_tables = [bias_table(_dilated_buckets(w, dl), rel_bias) for w, dl in DIL_PAIRS]
    tz = bias_table(_dsa_buckets(s), rel_bias * LOG2_E)

    for i in range(depth):
        j = i // 2
        kvm = norm_matmul(mem2, mem_norm, w_mem_kv[i].astype(BF16), BF16, 512, 512)
        kvm = kvm.reshape(b, N_MEM, 2 * CROSS_WIDTH)
        wo = w_out[i].astype(BF16)
        if i % 2 == 0:
            w_in = even_w_in[j].astype(BF16)
            proj = even_in_projection(xt, mixer_norm[i], w_in[:, :5 * ATTN_W], b, s, 512)
            qc = norm_matmul(xt, mixer_norm[i], w_in[:, 5 * ATTN_W:], BF16, 512, CROSS_WIDTH)
            outs, lses = [], []
            for grp in range(len(DIL_PAIRS)):
                o, lse = dilated_branch(proj[grp], proj[3 + grp], proj[6 + grp], dil_tables[grp])
                outs.append(o)
                lses.append(lse)
            mix = combine_groups(outs, lses, 512).reshape(t, ATTN_W)
            cross = cross_attention(qc.reshape(b, s, CROSS_WIDTH), 0, kvm, 1024)
        else:
            o_q, o_c, o_qi, o_ki, o_wi, o_qc = 0, 1024, 1280, 1792, 1856, 1864
            w = odd_w_in[j]
            pad = jnp.zeros((d, 128 - IDX_DIM - N_IDX_HEADS), w.dtype)
            w_in = jnp.concatenate(
                [w[:, o_q:o_c], w[:, o_qi:o_ki], w[:, o_c:o_qi], w[:, o_qc:],
                 w[:, o_ki:o_wi], w[:, o_wi:o_qc], pad], axis=1).astype(BF16)
            p1 = norm_matmul(xt, mixer_norm[i], w_in, BF16, 512, w_in.shape[1])
            p1 = p1.reshape(b, s, w_in.shape[1])
            c, ct = kv_latent_norm(p1, 6, odd_kv_norm[j])
            mask = dsa_select(p1, 2, 16)
            wuk = jnp.transpose(odd_w_uk[j], (1, 0, 2)).astype(BF16)
            wuvt = jnp.transpose(odd_w_uv[j], (1, 2, 0)).astype(BF16)
            mix = dsa_attention(p1, 0, c, ct, mask, tz, wuk, wuvt).reshape(t, ATTN_W)
            cross = cross_attention(p1, 7, kvm, 1024)
        cross = cross.reshape(t, CROSS_WIDTH)
        if i % 2 == 0:
            xt = out_projection_swiglu(mix, cross, wo[:ATTN_W], wo[ATTN_W:], xt, ffn_norm[i],
                                       even_w_gate[j].astype(BF16), even_w_up[j].astype(BF16),
                                       even_w_down[j].astype(BF16), 512)
        else:
            xt = out_projection(mix, cross, wo[:ATTN_W], wo[ATTN_W:], xt, 512)
            wr = jnp.pad(odd_w_router[j], ((0, 0), (0, 128 - N_EXPERTS)))
            xt = moe_ffn(xt, ffn_norm[i], wr, odd_w_gate[j].astype(BF16), odd_w_up[j].astype(BF16),
                         odd_w_down[j].astype(BF16), final_norm if i == depth - 1 else None)
    if depth % 2 == 1:
        xt = rmsnorm(xt, final_norm, 512)
    return xt.reshape(b, s, d)
```
